```python
import math
import jax, jax.numpy as jnp
from jax import lax
import numpy as np

D_MODEL = 4096
BATCH = 4
SEQ = 2048
DEPTH = 2
DEC_BATCH = 8
DEC_SEQ = 1
PAST_LEN = 16384
PAGE_SIZE = 128

HEAD_DIM = 128
MIX_WIDTH = D_MODEL
H_A = (MIX_WIDTH // 2) // HEAD_DIM
H_B = (MIX_WIDTH // 4) // HEAD_DIM
H_C = (MIX_WIDTH // 4) // HEAD_DIM
A_W = H_A * HEAD_DIM
B_W = H_B * HEAD_DIM
C_W = H_C * HEAD_DIM
DH_C = HEAD_DIM // 2
CONV_W = 4
DELTA_CHUNK = 64
RET_CHUNK = 128
Q_BLOCK = 128
D_FF = 4 * D_MODEL
N_IN = 4 * A_W + 2 * H_A + 4 * B_W + 3 * C_W
DEEPNORM_ALPHA = (2 * DEPTH) ** 0.25
DEEPNORM_BETA = (8 * DEPTH) ** -0.25
EPS = 1e-5

kernel_name = 'hybrid_delta_retention_diffattn_step'


def _split(z, sizes):
    out, start = [], 0
    for s in sizes:
        out.append(z[..., start:start + s])
        start += s
    return out


def _rms(x):
    xf = x.astype(jnp.float32)
    return xf * lax.rsqrt(jnp.mean(xf * xf, axis=-1, keepdims=True) + EPS)


def _layernorm(x, g, b):
    xf = x.astype(jnp.float32)
    xc = xf - jnp.mean(xf, axis=-1, keepdims=True)
    var = jnp.mean(xc * xc, axis=-1, keepdims=True)
    return (xc * lax.rsqrt(var + EPS) * g + b).astype(x.dtype)


def _l2norm(x):
    xf = x.astype(jnp.float32)
    return xf * lax.rsqrt(jnp.sum(xf * xf, axis=-1, keepdims=True) + 1e-6)


def _to_chunks(x, c):
    b, t = x.shape[:2]
    n = -(-t // c)
    x = jnp.pad(x.astype(jnp.float32), [(0, 0), (0, n * c - t)] + [(0, 0)] * (x.ndim - 2))
    x = x.reshape((b, n, c) + x.shape[2:])
    return jnp.transpose(x, (1, 0, 3, 2) + tuple(range(4, x.ndim)))


def _from_chunks(o, t):
    n, b, h, c, d = o.shape
    return jnp.transpose(o, (1, 0, 3, 2, 4)).reshape(b, n * c, h, d)[:, :t]


def _decay_mask(gc):
    c = gc.shape[-1]
    tri = jnp.tril(jnp.ones((c, c), bool))
    return jnp.exp(jnp.where(tri, gc[..., :, None] - gc[..., None, :], -jnp.inf))


def _gated_delta(q, k, v, g, beta, s0):
    t = q.shape[1]
    c = min(DELTA_CHUNK, t)
    q, k, v, g, beta = (_to_chunks(a, c) for a in (q, k, v, g, beta))
    gc = jnp.cumsum(g, axis=-1)
    decay = _decay_mask(gc)
    kb = k * beta[..., None]
    strict = jnp.tril(jnp.ones((c, c), bool), -1)
    m = jnp.where(strict, jnp.einsum('nbhik,nbhjk->nbhij', kb, k) * decay, 0.0)
    eye = jnp.eye(c, dtype=jnp.float32)
    t_inv = lax.linalg.triangular_solve(eye + m, jnp.broadcast_to(eye, m.shape),
                                        left_side=True, lower=True, unit_diagonal=True)
    u = t_inv @ (v * beta[..., None])
    w = t_inv @ (kb * jnp.exp(gc)[..., None])
    attn = jnp.einsum('nbhik,nbhjk->nbhij', q, k) * decay
    q_dec = q * jnp.exp(gc)[..., None]
    g_last = gc[..., -1]
    k_dec = k * jnp.exp(g_last[..., None] - gc)[..., None]

    def step(s, xs):
        u_c, w_c, a_c, qd_c, kd_c, gl_c = xs
        v_new = u_c - w_c @ s
        o = qd_c @ s + a_c @ v_new
        s = s * jnp.exp(gl_c)[..., None, None] + jnp.einsum('bhck,bhcv->bhkv', kd_c, v_new)
        return s, o

    s, o = lax.scan(step, s0.astype(jnp.float32), (u, w, attn, q_dec, k_dec, g_last))
    return _from_chunks(o, t), s


def _retention(q, k, v, log_gamma, s0):
    b, t, h, _ = q.shape
    c = min(RET_CHUNK, t)
    g = jnp.broadcast_to(log_gamma, (b, t, h))
    q, k, v, g = (_to_chunks(a, c) for a in (q, k, v, g))
    gc = jnp.cumsum(g, axis=-1)
    intra = (jnp.einsum('nbhik,nbhjk->nbhij', q, k) * _decay_mask(gc)) @ v
    q_dec = q * jnp.exp(gc)[..., None]
    g_last = gc[..., -1]
    k_dec = k * jnp.exp(g_last[..., None] - gc)[..., None]

    def step(s, xs):
        i_c, qd_c, kd_c, v_c, gl_c = xs
        o = i_c + qd_c @ s
        s = s * jnp.exp(gl_c)[..., None, None] + jnp.einsum('bhck,bhcv->bhkv', kd_c, v_c)
        return s, o

    s, o = lax.scan(step, s0.astype(jnp.float32), (intra, q_dec, k_dec, v, g_last))
    return _from_chunks(o, t), s


def _xpos_rotate(x, pos):
    half = HEAD_DIM // 2
    angle = jnp.repeat(1.0 / (10000.0 ** jnp.linspace(0.0, 1.0, half, dtype=jnp.float32)), 2)
    ph = pos.astype(jnp.float32)[:, None] * angle
    sin, cos = jnp.sin(ph)[:, None, :], jnp.cos(ph)[:, None, :]
    xf = x.astype(jnp.float32)
    rot = jnp.stack([-xf[..., 1::2], xf[..., 0::2]], axis=-1).reshape(xf.shape)
    return xf * cos + rot * sin


def _diff_attend(q, k, v, q_pos, k_pos, lam):
    s = jnp.einsum('bqhmd,bkhmd->bhmqk', q, k, preferred_element_type=jnp.float32) * (DH_C ** -0.5)
    s = jnp.where(k_pos[None, :] <= q_pos[:, None], s, -jnp.inf)
    p = jax.nn.softmax(s, axis=-1)
    a = p[:, :, 0] - lam * p[:, :, 1]
    return jnp.einsum('bhqk,bkhd->bqhd', a.astype(v.dtype), v)


def _diff_attn_prompt(q, k, v, lam):
    b, s = q.shape[:2]
    qb_len = min(Q_BLOCK, s)
    nb = s // qb_len
    qb = jnp.moveaxis(q.reshape((b, nb, qb_len) + q.shape[2:]), 1, 0)
    pos = jnp.arange(s)
    o = lax.map(lambda blk: _diff_attend(blk[0], k, v, blk[1], pos, lam),
                (qb, pos.reshape(nb, qb_len)))
    return jnp.moveaxis(o, 0, 1).reshape((b, s) + o.shape[3:])


def _mix(h, pos, conv_buf, s_delta, s_ret, k_past, v_past, p, lam_init):
    b, t, _ = h.shape
    z = h @ p['w_in']
    (qkv_a, z_a, a_a, b_a, q_b, k_b, v_b, g_b, q_c, k_c, v_c) = _split(
        z, [3 * A_W, A_W, H_A, H_A, B_W, B_W, B_W, B_W, C_W, C_W, C_W])

    xp = jnp.concatenate([conv_buf.astype(h.dtype), qkv_a], axis=1)
    conv = sum(xp[:, j:j + t] * p['conv_a'][j] for j in range(CONV_W))
    new_buf = xp[:, t:]
    qa, ka, va = jnp.split(jax.nn.silu(conv), 3, axis=-1)
    qa = _l2norm(qa.reshape(b, t, H_A, HEAD_DIM)) * (HEAD_DIM ** -0.5)
    ka = _l2norm(ka.reshape(b, t, H_A, HEAD_DIM))
    va = va.reshape(b, t, H_A, HEAD_DIM)
    g = -jnp.exp(p['a_log']) * jax.nn.softplus(a_a.astype(jnp.float32) + p['dt_bias'])
    beta = jax.nn.sigmoid(b_a.astype(jnp.float32))
    o_a, s_delta_new = _gated_delta(qa, ka, va, g, beta, s_delta)
    o_a = (_rms(o_a) * p['norm_a'] * jax.nn.silu(z_a.reshape(b, t, H_A, HEAD_DIM).astype(jnp.float32))
           ).reshape(b, t, A_W)

    log_gamma = jnp.log1p(-jnp.exp2(-5.0 - jnp.arange(H_B, dtype=jnp.float32)))
    qr = _xpos_rotate(q_b.reshape(b, t, H_B, HEAD_DIM), pos)
    kr = _xpos_rotate(k_b.reshape(b, t, H_B, HEAD_DIM), pos) * (HEAD_DIM ** -0.5)
    o_b, s_ret_new = _retention(qr, kr, v_b.reshape(b, t, H_B, HEAD_DIM), log_gamma, s_ret)
    o_b = _rms(o_b).reshape(b, t, B_W) * jax.nn.silu(g_b.astype(jnp.float32))

    k_row = k_c.reshape(b, t, H_C, HEAD_DIM)
    v_row = v_c.reshape(b, t, H_C, HEAD_DIM)
    qc = q_c.reshape(b, t, H_C, 2, DH_C)
    kc = k_c.reshape(b, t, H_C, 2, DH_C)
    lq = p['lambda_qk'].astype(jnp.float32)
    lam = jnp.exp(jnp.sum(lq[0] * lq[1])) - jnp.exp(jnp.sum(lq[2] * lq[3])) + lam_init
    if k_past is None:
        o_c = _diff_attn_prompt(qc, kc, v_row, lam)
    else:
        past = k_past.shape[1]
        k_all = jnp.concatenate([k_past.astype(kc.dtype), kc], axis=1)
        v_all = jnp.concatenate([v_past.astype(v_row.dtype), v_row], axis=1)
        o_c = _diff_attend(qc, k_all, v_all, past + jnp.arange(t), jnp.arange(past + t), lam)
    o_c = (_rms(o_c) * p['subln_c'] * (1.0 - lam_init)).reshape(b, t, C_W)

    y = jnp.concatenate([o_a, o_b, o_c], axis=-1).astype(h.dtype) @ p['w_out']
    return y, (new_buf, s_delta_new, s_ret_new, k_row, v_row)


def _layer(x, c, pos, conv_buf, s_delta, s_ret, k_past, v_past, p, lam_init):
    mod = (jax.nn.silu(c) @ p['w_ada'] + p['b_ada'])[:, None, :]
    sh1, sc1, g1, sh2, sc2, g2 = jnp.split(mod, 6, axis=-1)
    h = x * (1 + sc1) + sh1
    m, state = _mix(h, pos, conv_buf, s_delta, s_ret, k_past, v_past, p, lam_init)
    x = _layernorm(DEEPNORM_ALPHA * x + (1 + g1) * m, p['ln1_g'], p['ln1_b'])
    h = x * (1 + sc2) + sh2
    f = jnp.square(jax.nn.relu(h @ p['w_up'])) @ p['w_down']
    x = _layernorm(DEEPNORM_ALPHA * x + (1 + g2) * f, p['ln2_g'], p['ln2_b'])
    return x, state


def setup_inputs(seed: int = 0) -> dict:
    key = jax.random.key(seed)
    ks = jax.random.split(key, 32)
    d = D_MODEL
    n_pages = PAST_LEN // PAGE_SIZE
    n_used = DEC_BATCH * n_pages
    n_pool = n_used + max(1, n_used // 4)

    def nrm(k, shape, scale):
        return jax.random.normal(k, shape, jnp.float32) * scale

    dt = jnp.exp(jax.random.uniform(ks[14], (DEPTH, H_A), jnp.float32, math.log(1e-3), math.log(1e-1)))
    return {
        'x_prompt': nrm(ks[0], (BATCH, SEQ, d), 1.0),
        'x_sample': nrm(ks[1], (DEC_BATCH, DEC_SEQ, d), 1.0),
        'state_conv_a': nrm(ks[2], (DEC_BATCH, DEPTH, CONV_W - 1, 3 * A_W), 1.0),
        'state_delta': nrm(ks[3], (DEC_BATCH, DEPTH, H_A, HEAD_DIM, HEAD_DIM), 0.1),
        'state_ret': nrm(ks[4], (DEC_BATCH, DEPTH, H_B, HEAD_DIM, HEAD_DIM), 0.1),
        'cache_k': nrm(ks[5], (n_pool, DEPTH, PAGE_SIZE, H_C, HEAD_DIM), 1.0),
        'cache_v': nrm(ks[6], (n_pool, DEPTH, PAGE_SIZE, H_C, HEAD_DIM), 1.0),
        'page_table': jax.random.permutation(ks[7], n_pool)[:n_used].reshape(DEC_BATCH, n_pages).astype(jnp.int32),
        'c_prompt': nrm(ks[8], (BATCH, d), 1.0),
        'c_sample': nrm(ks[9], (DEC_BATCH, d), 1.0),
        'w_ada': nrm(ks[10], (DEPTH, d, 6 * d), 0.1 * d ** -0.5),
        'b_ada': nrm(ks[11], (DEPTH, 6 * d), 0.01),
        'w_in': nrm(ks[12], (DEPTH, d, N_IN), d ** -0.5),
        'conv_a': nrm(ks[13], (DEPTH, CONV_W, 3 * A_W), CONV_W ** -0.5),
        'a_log': jnp.log(jax.random.uniform(ks[15], (DEPTH, H_A), jnp.float32, 1.0, 16.0)),
        'dt_bias': dt + jnp.log(-jnp.expm1(-dt)),
        'norm_a': 1.0 + nrm(ks[16], (DEPTH, HEAD_DIM), 0.02),
        'lambda_qk': nrm(ks[17], (DEPTH, 4, DH_C), 0.1),
        'subln_c': 1.0 + nrm(ks[18], (DEPTH, HEAD_DIM), 0.02),
        'w_out': nrm(ks[19], (DEPTH, MIX_WIDTH, d), DEEPNORM_BETA * MIX_WIDTH ** -0.5),
        'ln1_g': 1.0 + nrm(ks[20], (DEPTH, d), 0.02),
        'ln1_b': nrm(ks[21], (DEPTH, d), 0.02),
        'w_up': nrm(ks[22], (DEPTH, d, D_FF), d ** -0.5),
        'w_down': nrm(ks[23], (DEPTH, D_FF, d), DEEPNORM_BETA * D_FF ** -0.5),
        'ln2_g': 1.0 + nrm(ks[24], (DEPTH, d), 0.02),
        'ln2_b': nrm(ks[25], (DEPTH, d), 0.02),
    }


def reference(x_prompt, x_sample, state_conv_a, state_delta, state_ret, cache_k, cache_v, page_table,
              c_prompt, c_sample, w_ada, b_ada, w_in, conv_a, a_log, dt_bias, norm_a, lambda_qk,
              subln_c, w_out, ln1_g, ln1_b, w_up, w_down, ln2_g, ln2_b):
    bp, dbs = x_prompt.shape[0], x_sample.shape[0]
    past_len = page_table.shape[1] * PAGE_SIZE
    pos_p = jnp.arange(x_prompt.shape[1])
    pos_s = past_len + jnp.arange(x_sample.shape[1])
    conv0 = jnp.zeros((bp, CONV_W - 1, 3 * A_W), x_prompt.dtype)
    sd0 = jnp.zeros((bp, H_A, HEAD_DIM, HEAD_DIM), jnp.float32)
    sr0 = jnp.zeros((bp, H_B, HEAD_DIM, HEAD_DIM), jnp.float32)
    xp, xs = x_prompt, x_sample
    st_p, st_s = [], []
    for l in range(DEPTH):
        p = {'w_ada': w_ada[l], 'b_ada': b_ada[l], 'w_in': w_in[l], 'conv_a': conv_a[l],
             'a_log': a_log[l], 'dt_bias': dt_bias[l], 'norm_a': norm_a[l], 'lambda_qk': lambda_qk[l],
             'subln_c': subln_c[l], 'w_out': w_out[l], 'ln1_g': ln1_g[l], 'ln1_b': ln1_b[l],
             'w_up': w_up[l], 'w_down': w_down[l], 'ln2_g': ln2_g[l], 'ln2_b': ln2_b[l]}
        lam_init = 0.8 - 0.6 * math.exp(-0.3 * l)
        k_past = cache_k[page_table, l].reshape(dbs, past_len, H_C, 2, DH_C)
        v_past = cache_v[page_table, l].reshape(dbs, past_len, H_C, HEAD_DIM)
        xp, sp = _layer(xp, c_prompt, pos_p, conv0, sd0, sr0, None, None, p, lam_init)
        xs, ss = _layer(xs, c_sample, pos_s, state_conv_a[:, l], state_delta[:, l], state_ret[:, l],
                        k_past, v_past, p, lam_init)
        st_p.append(sp)
        st_s.append(ss)
    conv_p, delta_p, ret_p, k_p, v_p = (jnp.stack([s[i] for s in st_p], axis=1) for i in range(5))
    conv_s, delta_s, ret_s, k_s, v_s = (jnp.stack([s[i] for s in st_s], axis=1) for i in range(5))
    return (xp, xs, conv_p, delta_p, ret_p, k_p, v_p, conv_s, delta_s, ret_s, k_s, v_s)
```

```python
import functools
import math

import jax
import jax.numpy as jnp
from jax import lax
from jax.experimental import pallas as pl
from jax.experimental.pallas import tpu as pltpu

F32 = jnp.float32
BF16 = jnp.bfloat16

HEAD_DIM = 128
HALF_DIM = HEAD_DIM // 2
CONV_W = 4
DELTA_CHUNK = 64
DELTA_SUPER = 256
RET_CHUNK = 128
PAGE_SIZE = 128
EPS = 1e-5
SAMPLE_ROWS = 16
VMEM_LIMIT = 56 * 1024 * 1024


def _cparams(*sem):
    return pltpu.CompilerParams(dimension_semantics=sem, vmem_limit_bytes=VMEM_LIMIT)


def _sigmoid(x):
    return 1.0 / (1.0 + jnp.exp(-x))


def _silu(x):
    return x * _sigmoid(x)


def _softplus(x):
    return jnp.maximum(x, 0.0) + jnp.log1p(jnp.exp(-jnp.abs(x)))


def _dot(a, b):
    return jnp.dot(a.astype(BF16), b.astype(BF16), preferred_element_type=F32)


def _dot_nt(a, b):
    return lax.dot_general(a.astype(BF16), b.astype(BF16), (((1,), (1,)), ((), ())),
                           preferred_element_type=F32)


def _rms_rows(x):
    return x * lax.rsqrt(jnp.mean(x * x, axis=-1, keepdims=True) + EPS)


def _ada_kernel(c_ref, w_ref, b_ref, o_ref):
    o_ref[...] = _dot(_silu(c_ref[...]), w_ref[...]) + b_ref[...]


def _ada(c_all, w_ada, b_ada, tn=1024):
    depth, d, n = w_ada.shape
    rows = c_all.shape[0]
    return pl.pallas_call(
        _ada_kernel,
        out_shape=jax.ShapeDtypeStruct((depth, rows, n), F32),
        grid=(depth, n // tn),
        in_specs=[pl.BlockSpec((rows, d), lambda l, j: (0, 0)),
                  pl.BlockSpec((None, d, tn), lambda l, j: (l, 0, j)),
                  pl.BlockSpec((None, 1, tn), lambda l, j: (l, 0, j))],
        out_specs=pl.BlockSpec((None, rows, tn), lambda l, j: (l, 0, j)),
        compiler_params=_cparams("arbitrary", "arbitrary"),
        name="ada_mod",
    )(c_all, w_ada, b_ada.reshape(depth, 1, n))


def _modulate_kernel(x_ref, sc_ref, sh_ref, o_ref):
    o_ref[...] = (x_ref[...] * (1.0 + sc_ref[...]) + sh_ref[...]).astype(o_ref.dtype)


def _modulate(x, sc, sh, tm):
    g, r, d = x.shape
    rm = sc.shape[1]
    mspec = pl.BlockSpec((None, rm, d), lambda a, i: (a, 0, 0))
    return pl.pallas_call(
        _modulate_kernel,
        out_shape=jax.ShapeDtypeStruct((g, r, d), BF16),
        grid=(g, r // tm),
        in_specs=[pl.BlockSpec((None, tm, d), lambda a, i: (a, i, 0)), mspec, mspec],
        out_specs=pl.BlockSpec((None, tm, d), lambda a, i: (a, i, 0)),
        compiler_params=_cparams("arbitrary", "arbitrary"),
        name="modulate",
    )(x, sc, sh)


def _postnorm_kernel(alpha, with_h, x_ref, y_ref, gate_ref, lg_ref, lb_ref, sc_ref, sh_ref, xo_ref, *h_ref):
    v = alpha * x_ref[...] + (1.0 + gate_ref[...]) * y_ref[...]
    vc = v - jnp.mean(v, axis=-1, keepdims=True)
    var = jnp.mean(vc * vc, axis=-1, keepdims=True)
    xn = vc * lax.rsqrt(var + EPS) * lg_ref[...] + lb_ref[...]
    xo_ref[...] = xn
    if with_h:
        h_ref[0][...] = (xn * (1.0 + sc_ref[...]) + sh_ref[...]).astype(BF16)


def _postnorm(x, y, gate, ln_g, ln_b, sc, sh, alpha, tm, with_h=True):
    g, r, d = x.shape
    rm = gate.shape[1]
    row = pl.BlockSpec((None, tm, d), lambda a, i: (a, i, 0))
    mspec = pl.BlockSpec((None, rm, d), lambda a, i: (a, 0, 0))
    vec = pl.BlockSpec((1, d), lambda a, i: (0, 0))
    out_shape = [jax.ShapeDtypeStruct((g, r, d), F32)]
    out_specs = [row]
    if with_h:
        out_shape.append(jax.ShapeDtypeStruct((g, r, d), BF16))
        out_specs.append(row)
    res = pl.pallas_call(
        functools.partial(_postnorm_kernel, alpha, with_h),
        out_shape=out_shape,
        grid=(g, r // tm),
        in_specs=[row, row, mspec, vec, vec, mspec, mspec],
        out_specs=out_specs,
        compiler_params=_cparams("arbitrary", "arbitrary"),
        name="postnorm",
    )(x, y, gate, ln_g.reshape(1, d), ln_b.reshape(1, d), sc, sh)
    return (res[0], res[1]) if with_h else (res[0], None)


def _matmul_kernel(relu2, nk, a_ref, as_ref, w_ref, o_ref, os_ref, *scratch):
    i = pl.program_id(1)
    k = pl.program_id(2)

    def finish(v):
        return jnp.square(jnp.maximum(v, 0.0)) if relu2 else v

    if nk == 1:
        o_ref[...] = finish(jnp.dot(a_ref[...], w_ref[...], preferred_element_type=F32)).astype(o_ref.dtype)

        @pl.when(i == 0)
        def _():
            os_ref[...] = finish(jnp.dot(as_ref[...], w_ref[...], preferred_element_type=F32)).astype(os_ref.dtype)
    else:
        acc_ref, accs_ref = scratch

        @pl.when(k == 0)
        def _():
            acc_ref[...] = jnp.zeros_like(acc_ref)

        acc_ref[...] += jnp.dot(a_ref[...], w_ref[...], preferred_element_type=F32)

        @pl.when(k == nk - 1)
        def _():
            o_ref[...] = finish(acc_ref[...]).astype(o_ref.dtype)

        @pl.when(i == 0)
        def _():
            @pl.when(k == 0)
            def _():
                accs_ref[...] = jnp.zeros_like(accs_ref)

            accs_ref[...] += jnp.dot(as_ref[...], w_ref[...], preferred_element_type=F32)

            @pl.when(k == nk - 1)
            def _():
                os_ref[...] = finish(accs_ref[...]).astype(os_ref.dtype)


def _matmul(a, a_s, w, out_dtype=F32, relu2=False, tm=1024, tn=1024, tk=4096):
    m, kdim = a.shape
    n = w.shape[1]
    rs = a_s.shape[0]
    tn = min(tn, n)
    tk = min(tk, kdim)
    nk = kdim // tk
    scratch = [] if nk == 1 else [pltpu.VMEM((tm, tn), F32), pltpu.VMEM((rs, tn), F32)]
    return pl.pallas_call(
        functools.partial(_matmul_kernel, relu2, nk),
        out_shape=[jax.ShapeDtypeStruct((m, n), out_dtype), jax.ShapeDtypeStruct((rs, n), out_dtype)],
        grid=(n // tn, m // tm, nk),
        in_specs=[pl.BlockSpec((tm, tk), lambda j, i, k: (i, k)),
                  pl.BlockSpec((rs, tk), lambda j, i, k: (0, k)),
                  pl.BlockSpec((tk, tn), lambda j, i, k: (k, j))],
        out_specs=[pl.BlockSpec((tm, tn), lambda j, i, k: (i, j)),
                   pl.BlockSpec((rs, tn), lambda j, i, k: (0, j))],
        scratch_shapes=scratch,
        compiler_params=_cparams("arbitrary", "arbitrary", "arbitrary"),
        name="matmul_relu2" if relu2 else "matmul",
    )(a, a_s, w)


def _gates_kernel(n_heads, zg_ref, alog_ref, dtb_ref, gc_ref, beta_ref):
    zg = zg_ref[...]
    rows = zg.shape[0]
    g = -jnp.exp(alog_ref[...]) * _softplus(zg[:, :n_heads] + dtb_ref[...])
    beta = _sigmoid(zg[:, n_heads:])
    pos = lax.broadcasted_iota(jnp.int32, (rows, HEAD_DIM), 0) % DELTA_CHUNK
    for h in range(n_heads):
        gh = jnp.broadcast_to(g[:, h:h + 1], (rows, HEAD_DIM))
        s = 1
        while s < DELTA_CHUNK:
            gh = gh + jnp.where(pos >= s, pltpu.roll(gh, s, 0), 0.0)
            s *= 2
        gc_ref[h] = gh
        beta_ref[h] = jnp.broadcast_to(beta[:, h:h + 1], (rows, HEAD_DIM))


def _gates(zg, a_log, dt_bias, rows=256):
    b, t, h2 = zg.shape
    nh = h2 // 2
    out = jax.ShapeDtypeStruct((b, nh, t, HEAD_DIM), F32)
    ospec = pl.BlockSpec((None, nh, rows, HEAD_DIM), lambda a, i: (a, 0, i, 0))
    vec = pl.BlockSpec((1, nh), lambda a, i: (0, 0))
    return pl.pallas_call(
        functools.partial(_gates_kernel, nh),
        out_shape=[out, out],
        grid=(b, t // rows),
        in_specs=[pl.BlockSpec((None, rows, h2), lambda a, i: (a, i, 0)), vec, vec],
        out_specs=[ospec, ospec],
        compiler_params=_cparams("arbitrary", "arbitrary"),
        name="delta_gates",
    )(zg, a_log.reshape(1, nh), dt_bias.reshape(1, nh))


def _delta_prompt_kernel(q_ref, k_ref, v_ref, za_ref, cq_ref, ck_ref, cv_ref, gc_ref, beta_ref, norm_ref,
                         o_ref, s_ref, qs_ref, ks_ref, vs_ref):
    t = q_ref.shape[0]
    row = lax.broadcasted_iota(jnp.int32, (t, HEAD_DIM), 0)

    def conv_silu(x_ref, w_ref):
        x = x_ref[...]
        w = w_ref[...]
        y = x * w[CONV_W - 1:CONV_W]
        for s in range(1, CONV_W):
            y = y + jnp.where(row >= s, pltpu.roll(x, s, 0), 0.0) * w[CONV_W - 1 - s:CONV_W - s]
        return _silu(y)

    def l2norm(x):
        return x * lax.rsqrt(jnp.sum(x * x, axis=-1, keepdims=True) + 1e-6)

    qs_ref[...] = l2norm(conv_silu(q_ref, cq_ref)) * (HEAD_DIM ** -0.5)
    ks_ref[...] = l2norm(conv_silu(k_ref, ck_ref))
    vs_ref[...] = conv_silu(v_ref, cv_ref)

    sb = DELTA_SUPER
    c = DELTA_CHUNK
    ii = lax.broadcasted_iota(jnp.int32, (sb, sb), 0)
    jj = lax.broadcasted_iota(jnp.int32, (sb, sb), 1)
    lower = ((ii // c) == (jj // c)) & (ii >= jj)
    strict = ii > jj
    eye = (ii == jj).astype(F32)
    pair_row = lax.broadcasted_iota(jnp.int32, (2 * c, HEAD_DIM), 0)
    norm_w = norm_ref[...]

    def body(n, s):
        r0 = pl.multiple_of(n * sb, sb)
        rows = pl.ds(r0, sb)
        q = qs_ref[rows]
        k = ks_ref[rows]
        v = vs_ref[rows]
        gc = gc_ref[rows]
        beta = beta_ref[rows]
        kb = k * beta
        eg = jnp.exp(gc)
        half = sb // 2
        gct = jnp.concatenate([gc[:half].T, gc[half:].T], axis=1)
        gc_j = jnp.broadcast_to(gct[0:1], (sb, sb))
        gc_i = jnp.concatenate([gc] * (sb // HEAD_DIM), axis=1)
        dec = jnp.exp(jnp.where(lower, gc_i - gc_j, -jnp.inf))
        mm = jnp.where(strict, _dot_nt(kb, k) * dec, 0.0)
        attn = _dot_nt(q, k) * dec
        x = -mm
        p = eye + x
        steps = int(math.log2(c)) - 1
        for _ in range(steps):
            x = _dot(x, x)
            p = p + _dot(p, x)
        uw = _dot(p, jnp.concatenate([v * beta, kb * eg], axis=1))
        u = uw[:, :HEAD_DIM]
        w = uw[:, HEAD_DIM:]
        qd = q * eg
        outs = []
        for ci in range(sb // c):
            cr = slice(ci * c, (ci + 1) * c)
            gl = gc[ci * c + c - 1:ci * c + c]
            r1 = _dot(jnp.concatenate([w[cr], qd[cr]], axis=0), s)
            v_new = u[cr] - r1[:c]
            blk = slice((ci // 2) * 2 * c, (ci // 2 + 1) * 2 * c)
            o_c = r1[c:] + _dot(attn[cr, blk], jnp.concatenate([v_new, v_new], axis=0))
            in_c = (pair_row // c) == (ci % 2)
            kd = jnp.where(in_c, k[blk] * jnp.exp(jnp.where(in_c, gl - gc[blk], 0.0)), 0.0)
            s = s * jnp.exp(gl) + _dot(kd.T, jnp.concatenate([v_new, v_new], axis=0))
            outs.append(o_c)
        o = jnp.concatenate(outs, axis=0)
        o_ref[rows] = (_rms_rows(o) * norm_w * _silu(za_ref[rows])).astype(o_ref.dtype)
        return s

    s_ref[...] = lax.fori_loop(0, t // sb, body, jnp.zeros((HEAD_DIM, HEAD_DIM), F32))


def _delta_prompt(z_main, conv_w, gc, beta, norm_w, n_heads):
    b, t, _ = z_main.shape
    col = lambda off: pl.BlockSpec((None, t, HEAD_DIM), lambda a, h: (a, 0, off + h))
    cw = lambda off: pl.BlockSpec((CONV_W, HEAD_DIM), lambda a, h: (0, off + h))
    gate = pl.BlockSpec((None, None, t, HEAD_DIM), lambda a, h: (a, h, 0, 0))
    return pl.pallas_call(
        _delta_prompt_kernel,
        out_shape=[jax.ShapeDtypeStruct((b, t, n_heads * HEAD_DIM), BF16),
                   jax.ShapeDtypeStruct((b, n_heads, HEAD_DIM, HEAD_DIM), F32)],
        grid=(b, n_heads),
        in_specs=[col(0), col(n_heads), col(2 * n_heads), col(3 * n_heads),
                  cw(0), cw(n_heads), cw(2 * n_heads), gate, gate,
                  pl.BlockSpec((1, HEAD_DIM), lambda a, h: (0, 0))],
        out_specs=[pl.BlockSpec((None, t, HEAD_DIM), lambda a, h: (a, 0, h)),
                   pl.BlockSpec((None, None, HEAD_DIM, HEAD_DIM), lambda a, h: (a, h, 0, 0))],
        scratch_shapes=[pltpu.VMEM((t, HEAD_DIM), F32)] * 3,
        compiler_params=_cparams("arbitrary", "arbitrary"),
        name="delta_prompt",
    )(z_main, z_main, z_main, z_main, conv_w, conv_w, conv_w, gc, beta, norm_w.reshape(1, HEAD_DIM))


def _xpos(x, cos, sin_e, sin_o):
    return x * cos + pltpu.roll(x, HEAD_DIM - 1, 1) * sin_e + pltpu.roll(x, 1, 1) * sin_o


def _log_gamma(h):
    return jnp.log1p(-jnp.exp2(-5.0 - h))


def _ret_prompt_kernel(q_ref, k_ref, v_ref, g_ref, cos_ref, sine_ref, sino_ref, o_ref, s_ref, qs_ref, ks_ref):
    t = q_ref.shape[0]
    c = RET_CHUNK
    cos, sin_e, sin_o = cos_ref[...], sine_ref[...], sino_ref[...]
    qs_ref[...] = _xpos(q_ref[...], cos, sin_e, sin_o)
    ks_ref[...] = _xpos(k_ref[...], cos, sin_e, sin_o) * (HEAD_DIM ** -0.5)

    h = jnp.full((1, 1), pl.program_id(1), jnp.int32).astype(F32)
    lg = _log_gamma(h)
    ii = lax.broadcasted_iota(jnp.int32, (c, c), 0)
    jj = lax.broadcasted_iota(jnp.int32, (c, c), 1)
    dec = jnp.exp(jnp.where(ii >= jj, (ii - jj).astype(F32) * lg, -jnp.inf))
    pos = lax.broadcasted_iota(jnp.int32, (c, HEAD_DIM), 0).astype(F32)
    e_in = jnp.exp((pos + 1.0) * lg)
    e_out = jnp.exp((c - 1.0 - pos) * lg)
    e_all = jnp.exp(c * lg)

    def body(n, s):
        rows = pl.ds(pl.multiple_of(n * c, c), c)
        q = qs_ref[rows]
        k = ks_ref[rows]
        v = v_ref[rows]
        a = _dot_nt(q, k) * dec
        o = _dot(a, v) + _dot(q * e_in, s)
        s = s * e_all + _dot((k * e_out).T, v)
        o_ref[rows] = (_rms_rows(o) * _silu(g_ref[rows])).astype(o_ref.dtype)
        return s

    s_ref[...] = lax.fori_loop(0, t // c, body, jnp.zeros((HEAD_DIM, HEAD_DIM), F32))


def _ret_prompt(z_tail, cos, sin_e, sin_o, n_heads):
    b, t, _ = z_tail.shape
    col = lambda off: pl.BlockSpec((None, t, HEAD_DIM), lambda a, h: (a, 0, off + h))
    tab = pl.BlockSpec((t, HEAD_DIM), lambda a, h: (0, 0))
    return pl.pallas_call(
        _ret_prompt_kernel,
        out_shape=[jax.ShapeDtypeStruct((b, t, n_heads * HEAD_DIM), BF16),
                   jax.ShapeDtypeStruct((b, n_heads, HEAD_DIM, HEAD_DIM), F32)],
        grid=(b, n_heads),
        in_specs=[col(0), col(n_heads), col(2 * n_heads), col(3 * n_heads), tab, tab, tab],
        out_specs=[pl.BlockSpec((None, t, HEAD_DIM), lambda a, h: (a, 0, h)),
                   pl.BlockSpec((None, None, HEAD_DIM, HEAD_DIM), lambda a, h: (a, h, 0, 0))],
        scratch_shapes=[pltpu.VMEM((t, HEAD_DIM), F32)] * 2,
        compiler_params=_cparams("arbitrary", "arbitrary"),
        name="ret_prompt",
    )(z_tail, z_tail, z_tail, z_tail, cos, sin_e, sin_o)


def _lambda(lq, lam_init):
    l01 = jnp.sum(lq[0:1] * lq[1:2], axis=-1, keepdims=True)
    l23 = jnp.sum(lq[2:3] * lq[3:4], axis=-1, keepdims=True)
    return jnp.exp(l01) - jnp.exp(l23) + lam_init


def _attn_prompt_kernel(lam_init, tq, tk, q_ref, k_ref, v_ref, lq_ref, sub_ref, o_ref):
    i = pl.program_id(2)
    q = q_ref[...]
    lane = lax.broadcasted_iota(jnp.int32, (tq, HEAD_DIM), 1)
    scale = HALF_DIM ** -0.5
    q2 = jnp.concatenate([jnp.where(lane < HALF_DIM, q, 0.0), jnp.where(lane >= HALF_DIM, q, 0.0)],
                         axis=0) * scale
    q2 = q2.astype(BF16)
    rpos = i * tq + lax.broadcasted_iota(jnp.int32, (2 * tq, tk), 0) % tq
    cidx = lax.broadcasted_iota(jnp.int32, (2 * tq, tk), 1)

    def body(j, carry):
        m, l, acc = carry
        rows = pl.ds(pl.multiple_of(j * tk, tk), tk)
        s = _dot_nt(q2, k_ref[rows])
        s = jnp.where(j * tk + cidx <= rpos, s, -jnp.inf)
        m_new = jnp.maximum(m, jnp.max(s, axis=-1, keepdims=True))
        alpha = jnp.exp(m - m_new)
        p = jnp.exp(s - m_new)
        l = alpha * l + jnp.sum(p, axis=-1, keepdims=True)
        acc = alpha * acc + _dot(p, v_ref[rows])
        return m_new, l, acc

    n_kv = (i * tq + tq + tk - 1) // tk
    init = (jnp.full((2 * tq, 1), -jnp.inf, F32), jnp.zeros((2 * tq, 1), F32),
            jnp.zeros((2 * tq, HEAD_DIM), F32))
    m, l, acc = lax.fori_loop(0, n_kv, body, init)
    o2 = acc / l
    lam = _lambda(lq_ref[...], lam_init)
    o = o2[:tq] - lam * o2[tq:]
    o_ref[...] = (_rms_rows(o) * sub_ref[...] * (1.0 - lam_init)).astype(o_ref.dtype)


def _attn_prompt(z_tail, lambda_qk, subln, lam_init, n_heads, col0, tq=256, tk=256):
    b, t, _ = z_tail.shape
    return pl.pallas_call(
        functools.partial(_attn_prompt_kernel, lam_init, tq, tk),
        out_shape=jax.ShapeDtypeStruct((b, t, n_heads * HEAD_DIM), BF16),
        grid=(b, n_heads, t // tq),
        in_specs=[pl.BlockSpec((None, tq, HEAD_DIM), lambda a, h, i: (a, i, col0 + h)),
                  pl.BlockSpec((None, t, HEAD_DIM), lambda a, h, i: (a, 0, col0 + n_heads + h)),
                  pl.BlockSpec((None, t, HEAD_DIM), lambda a, h, i: (a, 0, col0 + 2 * n_heads + h)),
                  pl.BlockSpec((4, HALF_DIM), lambda a, h, i: (0, 0)),
                  pl.BlockSpec((1, HEAD_DIM), lambda a, h, i: (0, 0))],
        out_specs=pl.BlockSpec((None, tq, HEAD_DIM), lambda a, h, i: (a, i, h)),
        compiler_params=_cparams("arbitrary", "arbitrary", "arbitrary"),
        name="attn_prompt",
    )(z_tail, z_tail, z_tail, lambda_qk, subln.reshape(1, HEAD_DIM))


def _lane_pick(x, h):
    lane = lax.broadcasted_iota(jnp.int32, x.shape, 1)
    return jnp.sum(jnp.where(lane == h, x, 0.0), axis=-1, keepdims=True)


def _row_to_col(x):
    ii = lax.broadcasted_iota(jnp.int32, (HEAD_DIM, HEAD_DIM), 0)
    jj = lax.broadcasted_iota(jnp.int32, (HEAD_DIM, HEAD_DIM), 1)
    return jnp.sum(jnp.where(ii == jj, jnp.broadcast_to(x, (HEAD_DIM, HEAD_DIM)), 0.0), axis=-1, keepdims=True)


def _rows8(*rows):
    pad = jnp.zeros((8 - len(rows), HEAD_DIM), F32)
    return jnp.concatenate(list(rows) + [pad], axis=0)


def _delta_sample_kernel(q_ref, k_ref, v_ref, za_ref, bq_ref, bk_ref, bv_ref, cq_ref, ck_ref, cv_ref,
                         zg_ref, alog_ref, dtb_ref, norm_ref, s_ref, o_ref, so_ref):
    h = pl.program_id(1)
    n_heads = alog_ref.shape[1]

    def conv_silu(x_ref, buf_ref, w_ref):
        w = w_ref[...]
        buf = buf_ref[...]
        y = x_ref[...] * w[CONV_W - 1:CONV_W]
        for j in range(CONV_W - 1):
            y = y + buf[j:j + 1] * w[j:j + 1]
        return _silu(y)

    def l2norm(x):
        return x * lax.rsqrt(jnp.sum(x * x, axis=-1, keepdims=True) + 1e-6)

    q = l2norm(conv_silu(q_ref, bq_ref, cq_ref)) * (HEAD_DIM ** -0.5)
    k = l2norm(conv_silu(k_ref, bk_ref, ck_ref))
    v = conv_silu(v_ref, bv_ref, cv_ref)
    zg = zg_ref[...]
    a_a = _lane_pick(zg, h)
    b_a = _lane_pick(zg, h + n_heads)
    g = -jnp.exp(_lane_pick(alog_ref[...], h)) * _softplus(a_a + _lane_pick(dtb_ref[...], h))
    beta = _sigmoid(b_a)
    eg = jnp.exp(g)
    s = s_ref[...]
    ks_qs = _dot(_rows8(k, q), s)
    v_new = beta * (v - eg * ks_qs[0:1])
    qk = jnp.sum(q * k, axis=-1, keepdims=True)
    o = eg * ks_qs[1:2] + qk * v_new
    so_ref[...] = s * eg + _row_to_col(k) * v_new
    o_ref[...] = (_rms_rows(o) * norm_ref[...] * _silu(za_ref[...])).astype(o_ref.dtype)


def _delta_sample(zs_main, zs_gate, conv_buf, conv_w, a_log, dt_bias, norm_w, state, layer, n_heads):
    nb = state.shape[0]
    col = lambda off: pl.BlockSpec((None, 1, HEAD_DIM), lambda a, h: (a, 0, off + h))
    buf = lambda off: pl.BlockSpec((None, None, CONV_W - 1, HEAD_DIM), lambda a, h: (a, layer, 0, off + h))
    cw = lambda off: pl.BlockSpec((CONV_W, HEAD_DIM), lambda a, h: (0, off + h))
    vec = pl.BlockSpec((1, n_heads), lambda a, h: (0, 0))
    return pl.pallas_call(
        _delta_sample_kernel,
        out_shape=[jax.ShapeDtypeStruct((nb, 1, n_heads * HEAD_DIM), BF16),
                   jax.ShapeDtypeStruct((nb, n_heads, HEAD_DIM, HEAD_DIM), F32)],
        grid=(nb, n_heads),
        in_specs=[col(0), col(n_heads), col(2 * n_heads), col(3 * n_heads),
                  buf(0), buf(n_heads), buf(2 * n_heads), cw(0), cw(n_heads), cw(2 * n_heads),
                  pl.BlockSpec((None, 1, 2 * n_heads), lambda a, h: (a, 0, 0)), vec, vec,
                  pl.BlockSpec((1, HEAD_DIM), lambda a, h: (0, 0)),
                  pl.BlockSpec((None, None, None, HEAD_DIM, HEAD_DIM), lambda a, h: (a, layer, h, 0, 0))],
        out_specs=[pl.BlockSpec((None, 1, HEAD_DIM), lambda a, h: (a, 0, h)),
                   pl.BlockSpec((None, None, HEAD_DIM, HEAD_DIM), lambda a, h: (a, h, 0, 0))],
        compiler_params=_cparams("arbitrary", "arbitrary"),
        name="delta_sample",
    )(zs_main, zs_main, zs_main, zs_main, conv_buf, conv_buf, conv_buf, conv_w, conv_w, conv_w,
      zs_gate, a_log.reshape(1, n_heads), dt_bias.reshape(1, n_heads), norm_w.reshape(1, HEAD_DIM), state)


def _ret_sample_kernel(q_ref, k_ref, v_ref, g_ref, cos_ref, sine_ref, sino_ref, s_ref, o_ref, so_ref):
    h = jnp.full((1, 1), pl.program_id(1), jnp.int32).astype(F32)
    gamma = jnp.exp(_log_gamma(h))
    cos, sin_e, sin_o = cos_ref[...], sine_ref[...], sino_ref[...]
    q = _xpos(q_ref[...], cos, sin_e, sin_o)
    k = _xpos(k_ref[...], cos, sin_e, sin_o) * (HEAD_DIM ** -0.5)
    v = v_ref[...]
    s = s_ref[...]
    qs = _dot(_rows8(q), s)
    qk = jnp.sum(q * k, axis=-1, keepdims=True)
    o = qk * v + gamma * qs[0:1]
    so_ref[...] = s * gamma + _row_to_col(k) * v
    o_ref[...] = (_rms_rows(o) * _silu(g_ref[...])).astype(o_ref.dtype)


def _ret_sample(zs_tail, cos, sin_e, sin_o, state, layer, n_heads):
    nb = state.shape[0]
    col = lambda off: pl.BlockSpec((None, 1, HEAD_DIM), lambda a, h: (a, 0, off + h))
    tab = pl.BlockSpec((1, HEAD_DIM), lambda a, h: (0, 0))
    return pl.pallas_call(
        _ret_sample_kernel,
        out_shape=[jax.ShapeDtypeStruct((nb, 1, n_heads * HEAD_DIM), BF16),
                   jax.ShapeDtypeStruct((nb, n_heads, HEAD_DIM, HEAD_DIM), F32)],
        grid=(nb, n_heads),
        in_specs=[col(0), col(n_heads), col(2 * n_heads), col(3 * n_heads), tab, tab, tab,
                  pl.BlockSpec((None, None, None, HEAD_DIM, HEAD_DIM), lambda a, h: (a, layer, h, 0, 0))],
        out_specs=[pl.BlockSpec((None, 1, HEAD_DIM), lambda a, h: (a, 0, h)),
                   pl.BlockSpec((None, None, HEAD_DIM, HEAD_DIM), lambda a, h: (a, h, 0, 0))],
        compiler_params=_cparams("arbitrary", "arbitrary"),
        name="ret_sample",
    )(zs_tail, zs_tail, zs_tail, zs_tail, cos, sin_e, sin_o, state)


def _attn_decode_kernel(lam_init, pt_ref, q_ref, kn_ref, vn_ref, kc_ref, vc_ref, lq_ref, sub_ref, o_ref,
                        m_ref, l_ref, acc_ref):
    j = pl.program_id(1)
    n_pages = pl.num_programs(1)
    nh = q_ref.shape[0]
    scale = HALF_DIM ** -0.5
    di = lax.broadcasted_iota(jnp.int32, (HEAD_DIM, 2 * HEAD_DIM), 0)
    ci = lax.broadcasted_iota(jnp.int32, (HEAD_DIM, 2 * HEAD_DIM), 1)
    expand = ((di // HALF_DIM) == (ci // HEAD_DIM)).astype(BF16)
    q = q_ref[...] * scale

    @pl.when(j == 0)
    def _():
        m_ref[...] = jnp.full(m_ref.shape, -jnp.inf, F32)
        l_ref[...] = jnp.zeros_like(l_ref)
        acc_ref[...] = jnp.zeros_like(acc_ref)

    def update(kt, vt):
        n_tok = kt.shape[0]
        prod = (kt * q[None]).reshape(n_tok * nh, HEAD_DIM)
        s = jnp.dot(prod.astype(BF16), expand, preferred_element_type=F32)
        s = s.reshape(n_tok, nh, 2 * HEAD_DIM)
        m_old = m_ref[...]
        m_new = jnp.maximum(m_old, jnp.max(s, axis=0))
        alpha = jnp.exp(m_old - m_new)
        p = jnp.exp(s - m_new[None])
        l_ref[...] = alpha * l_ref[...] + jnp.sum(p, axis=0)
        pv = jnp.concatenate([jnp.sum(p[:, :, :HEAD_DIM] * vt, axis=0),
                              jnp.sum(p[:, :, HEAD_DIM:] * vt, axis=0)], axis=-1)
        acc_ref[...] = alpha * acc_ref[...] + pv
        m_ref[...] = m_new

    update(kc_ref[...], vc_ref[...])

    @pl.when(j == n_pages - 1)
    def _():
        update(kn_ref[...][None], vn_ref[...][None])
        o2 = acc_ref[...] / l_ref[...]
        lam = _lambda(lq_ref[...], lam_init)
        o = o2[:, :HEAD_DIM] - lam * o2[:, HEAD_DIM:]
        o_ref[...] = (_rms_rows(o) * sub_ref[...] * (1.0 - lam_init)).astype(o_ref.dtype)


def _attn_decode(q, k_new, v_new, cache_k, cache_v, page_table, lambda_qk, subln, lam_init, layer):
    nb, nh, _ = q.shape
    n_pages = page_table.shape[1]
    tok = pl.BlockSpec((None, nh, HEAD_DIM), lambda b, j, pt: (b, 0, 0))
    page = pl.BlockSpec((None, None, PAGE_SIZE, nh, HEAD_DIM), lambda b, j, pt: (pt[b, j], layer, 0, 0, 0))
    grid_spec = pltpu.PrefetchScalarGridSpec(
        num_scalar_prefetch=1,
        grid=(nb, n_pages),
        in_specs=[tok, tok, tok, page, page,
                  pl.BlockSpec((4, HALF_DIM), lambda b, j, pt: (0, 0)),
                  pl.BlockSpec((1, HEAD_DIM), lambda b, j, pt: (0, 0))],
        out_specs=tok,
        scratch_shapes=[pltpu.VMEM((nh, 2 * HEAD_DIM), F32)] * 3,
    )
    return pl.pallas_call(
        functools.partial(_attn_decode_kernel, lam_init),
        out_shape=jax.ShapeDtypeStruct((nb, nh, HEAD_DIM), BF16),
        grid_spec=grid_spec,
        compiler_params=_cparams("arbitrary", "arbitrary"),
        name="attn_decode",
    )(page_table, q, k_new, v_new, cache_k, cache_v, lambda_qk, subln.reshape(1, HEAD_DIM))


def _xpos_tables(pos):
    half = HEAD_DIM // 2
    angle = jnp.repeat(1.0 / (10000.0 ** jnp.linspace(0.0, 1.0, half, dtype=F32)), 2)
    ph = pos.astype(F32)[:, None] * angle
    sin, cos = jnp.sin(ph), jnp.cos(ph)
    even = (jnp.arange(HEAD_DIM) % 2) == 0
    return cos, jnp.where(even, -sin, 0.0), jnp.where(even, 0.0, sin)


def kernel(x_prompt, x_sample, state_conv_a, state_delta, state_ret, cache_k, cache_v, page_table, c_prompt, c_sample, w_ada, b_ada, w_in, conv_a, a_log, dt_bias, norm_a, lambda_qk, subln_c, w_out, ln1_g, ln1_b, w_up, w_down, ln2_g, ln2_b):
    bp, t, d = x_prompt.shape
    nb = x_sample.shape[0]
    depth = w_in.shape[0]
    h_a = state_delta.shape[2]
    h_b = state_ret.shape[2]
    h_c = cache_k.shape[3]
    a_w, b_w, c_w = h_a * HEAD_DIM, h_b * HEAD_DIM, h_c * HEAD_DIM
    n_main = 4 * a_w
    n_gate = 2 * h_a
    past_len = page_table.shape[1] * PAGE_SIZE
    alpha = (2 * depth) ** 0.25
    rs = SAMPLE_ROWS
    assert nb + bp <= rs and x_sample.shape[1] == 1

    c_all = jnp.concatenate([c_sample, c_prompt, jnp.zeros((rs - nb - bp, d), F32)], axis=0)
    mod = _ada(c_all, w_ada, b_ada)

    def mod_p(l, i):
        return mod[l, nb:nb + bp, None, i * d:(i + 1) * d]

    def mod_s(l, i):
        return mod[l][None, :, i * d:(i + 1) * d]

    cos_p, sine_p, sino_p = _xpos_tables(jnp.arange(t))
    cos_s, sine_s, sino_s = _xpos_tables(past_len + jnp.arange(1))

    xp = x_prompt
    xs = jnp.concatenate([x_sample.reshape(nb, d), jnp.zeros((rs - nb, d), F32)], axis=0)[None]
    hp = _modulate(xp, mod_p(0, 1), mod_p(0, 0), 256)
    hs = _modulate(xs, mod_s(0, 1), mod_s(0, 0), rs)

    st_p, st_s = [], []
    for l in range(depth):
        lam_init = 0.8 - 0.6 * math.exp(-0.3 * l)
        w_l = w_in[l]
        w_main = w_l[:, :n_main].astype(BF16)
        w_gate = w_l[:, n_main:n_main + n_gate].astype(BF16)
        w_tail = w_l[:, n_main + n_gate:].astype(BF16)
        hp2 = hp.reshape(bp * t, d)
        hs2 = hs.reshape(rs, d)
        zp_main, zs_main = _matmul(hp2, hs2, w_main)
        zp_gate, zs_gate = _matmul(hp2, hs2, w_gate)
        zp_tail, zs_tail = _matmul(hp2, hs2, w_tail)
        zp_main = zp_main.reshape(bp, t, n_main)
        zp_tail = zp_tail.reshape(bp, t, -1)

        gc, beta = _gates(zp_gate.reshape(bp, t, n_gate), a_log[l], dt_bias[l])
        oa_p, delta_p = _delta_prompt(zp_main, conv_a[l], gc, beta, norm_a[l], h_a)
        ob_p, ret_p = _ret_prompt(zp_tail, cos_p, sine_p, sino_p, h_b)
        oc_p = _attn_prompt(zp_tail, lambda_qk[l], subln_c[l], lam_init, h_c, 4 * h_b)
        k_off = 4 * b_w + c_w
        conv_p = zp_main[:, t - (CONV_W - 1):, :3 * a_w]
        k_p = zp_tail[:, :, k_off:k_off + c_w].reshape(bp, t, h_c, HEAD_DIM)
        v_p = zp_tail[:, :, k_off + c_w:k_off + 2 * c_w].reshape(bp, t, h_c, HEAD_DIM)
        st_p.append((conv_p, delta_p, ret_p, k_p, v_p))

        zs_m = zs_main[:nb]
        zs_t = zs_tail[:nb]
        oa_s, delta_s = _delta_sample(zs_m[:, None], zs_gate[:nb, None], state_conv_a, conv_a[l], a_log[l],
                                      dt_bias[l], norm_a[l], state_delta, l, h_a)
        ob_s, ret_s = _ret_sample(zs_t[:, None], cos_s, sine_s, sino_s, state_ret, l, h_b)
        q_s = zs_t[:, 4 * b_w:4 * b_w + c_w].reshape(nb, h_c, HEAD_DIM)
        k_s = zs_t[:, k_off:k_off + c_w].reshape(nb, h_c, HEAD_DIM)
        v_s = zs_t[:, k_off + c_w:k_off + 2 * c_w].reshape(nb, h_c, HEAD_DIM)
        oc_s = _attn_decode(q_s, k_s, v_s, cache_k, cache_v, page_table, lambda_qk[l], subln_c[l], lam_init, l)
        conv_s = jnp.concatenate([state_conv_a[:, l, 1:], zs_m[:, None, :3 * a_w]], axis=1)
        st_s.append((conv_s, delta_s, ret_s, k_s[:, None], v_s[:, None]))

        mix_p = jnp.concatenate([oa_p, ob_p, oc_p], axis=-1).reshape(bp * t, -1)
        mix_s = jnp.concatenate([oa_s.reshape(nb, a_w), ob_s.reshape(nb, b_w), oc_s.reshape(nb, c_w)], axis=-1)
        mix_s = jnp.concatenate([mix_s, jnp.zeros((rs - nb, mix_s.shape[1]), BF16)], axis=0)
        yp, ys = _matmul(mix_p, mix_s, w_out[l].astype(BF16))
        xp, hp = _postnorm(xp, yp.reshape(bp, t, d), mod_p(l, 2), ln1_g[l], ln1_b[l], mod_p(l, 4), mod_p(l, 3),
                           alpha, 256)
        xs, hs = _postnorm(xs, ys[None], mod_s(l, 2), ln1_g[l], ln1_b[l], mod_s(l, 4), mod_s(l, 3), alpha, rs)

        up, us = _matmul(hp.reshape(bp * t, d), hs.reshape(rs, d), w_up[l].astype(BF16), out_dtype=BF16, relu2=True)
        fp, fs = _matmul(up, us, w_down[l].astype(BF16))
        last = l == depth - 1
        nl = min(l + 1, depth - 1)
        xp, hp = _postnorm(xp, fp.reshape(bp, t, d), mod_p(l, 5), ln2_g[l], ln2_b[l], mod_p(nl, 1), mod_p(nl, 0),
                           alpha, 256, with_h=not last)
        xs, hs = _postnorm(xs, fs[None], mod_s(l, 5), ln2_g[l], ln2_b[l], mod_s(nl, 1), mod_s(nl, 0), alpha, rs,
                           with_h=not last)

    outs_p = tuple(jnp.stack([s[i] for s in st_p], axis=1) for i in range(5))
    outs_s = tuple(jnp.stack([s[i] for s in st_s], axis=1) for i in range(5))
    y_sample = xs[0, :nb].reshape(nb, 1, d)
    return (xp, y_sample) + outs_p + outs_s
```

```python
import functools
import math

import jax
import jax.numpy as jnp
from jax import lax
from jax.experimental import pallas as pl
from jax.experimental.pallas import tpu as pltpu

F32 = jnp.float32
BF16 = jnp.bfloat16

HEAD_DIM = 128
HALF_DIM = HEAD_DIM // 2
CONV_W = 4
DELTA_CHUNK = 64
DELTA_SUPER = 256
RET_CHUNK = 128
PAGE_SIZE = 128
EPS = 1e-5
SAMPLE_ROWS = 16
VMEM_LIMIT = 56 * 1024 * 1024


def _cparams(*sem):
    return pltpu.CompilerParams(dimension_semantics=sem, vmem_limit_bytes=VMEM_LIMIT)


def _sigmoid(x):
    return 1.0 / (1.0 + jnp.exp(-x))


def _silu(x):
    return x * _sigmoid(x)


def _softplus(x):
    return jnp.maximum(x, 0.0) + jnp.log1p(jnp.exp(-jnp.abs(x)))


def _dot(a, b):
    return jnp.dot(a.astype(BF16), b.astype(BF16), preferred_element_type=F32)


def _dot_nt(a, b):
    return lax.dot_general(a.astype(BF16), b.astype(BF16), (((1,), (1,)), ((), ())),
                           preferred_element_type=F32)


def _rms_rows(x):
    return x * lax.rsqrt(jnp.mean(x * x, axis=-1, keepdims=True) + EPS)


def _ada_kernel(c_ref, w0_ref, w1_ref, b_ref, o_ref):
    c = _silu(c_ref[...])
    dh = w0_ref.shape[0]
    o_ref[...] = _dot(c[:, :dh], w0_ref[...]) + _dot(c[:, dh:], w1_ref[...]) + b_ref[...]


def _ada(c_all, w_ada, b_ada, tn=1024):
    depth, d, n = w_ada.shape
    rows = c_all.shape[0]
    return pl.pallas_call(
        _ada_kernel,
        out_shape=jax.ShapeDtypeStruct((depth, rows, n), F32),
        grid=(depth, n // tn),
        in_specs=[pl.BlockSpec((rows, d), lambda l, j: (0, 0)),
                  pl.BlockSpec((None, d // 2, tn), lambda l, j: (l, 0, j)),
                  pl.BlockSpec((None, d // 2, tn), lambda l, j: (l, 1, j)),
                  pl.BlockSpec((None, 1, tn), lambda l, j: (l, 0, j))],
        out_specs=pl.BlockSpec((None, rows, tn), lambda l, j: (l, 0, j)),
        compiler_params=_cparams("arbitrary", "arbitrary"),
        name="ada_mod",
    )(c_all, w_ada, w_ada, b_ada.reshape(depth, 1, n))


def _modulate_kernel(x_ref, sc_ref, sh_ref, o_ref):
    o_ref[...] = (x_ref[...] * (1.0 + sc_ref[...]) + sh_ref[...]).astype(o_ref.dtype)


def _modulate(x, sc, sh, tm):
    g, r, d = x.shape
    rm = sc.shape[1]
    mspec = pl.BlockSpec((None, rm, d), lambda a, i: (a, 0, 0))
    return pl.pallas_call(
        _modulate_kernel,
        out_shape=jax.ShapeDtypeStruct((g, r, d), BF16),
        grid=(g, r // tm),
        in_specs=[pl.BlockSpec((None, tm, d), lambda a, i: (a, i, 0)), mspec, mspec],
        out_specs=pl.BlockSpec((None, tm, d), lambda a, i: (a, i, 0)),
        compiler_params=_cparams("arbitrary", "arbitrary"),
        name="modulate",
    )(x, sc, sh)


def _postnorm_kernel(alpha, with_h, x_ref, y_ref, gate_ref, lg_ref, lb_ref, sc_ref, sh_ref, xo_ref, *h_ref):
    v = alpha * x_ref[...] + (1.0 + gate_ref[...]) * y_ref[...]
    vc = v - jnp.mean(v, axis=-1, keepdims=True)
    var = jnp.mean(vc * vc, axis=-1, keepdims=True)
    xn = vc * lax.rsqrt(var + EPS) * lg_ref[...] + lb_ref[...]
    xo_ref[...] = xn
    if with_h:
        h_ref[0][...] = (xn * (1.0 + sc_ref[...]) + sh_ref[...]).astype(BF16)


def _postnorm(x, y, gate, ln_g, ln_b, sc, sh, alpha, tm, with_h=True):
    g, r, d = x.shape
    rm = gate.shape[1]
    row = pl.BlockSpec((None, tm, d), lambda a, i: (a, i, 0))
    mspec = pl.BlockSpec((None, rm, d), lambda a, i: (a, 0, 0))
    vec = pl.BlockSpec((1, d), lambda a, i: (0, 0))
    out_shape = [jax.ShapeDtypeStruct((g, r, d), F32)]
    out_specs = [row]
    if with_h:
        out_shape.append(jax.ShapeDtypeStruct((g, r, d), BF16))
        out_specs.append(row)
    res = pl.pallas_call(
        functools.partial(_postnorm_kernel, alpha, with_h),
        out_shape=out_shape,
        grid=(g, r // tm),
        in_specs=[row, row, mspec, vec, vec, mspec, mspec],
        out_specs=out_specs,
        compiler_params=_cparams("arbitrary", "arbitrary"),
        name="postnorm",
    )(x, y, gate, ln_g.reshape(1, d), ln_b.reshape(1, d), sc, sh)
    return (res[0], res[1]) if with_h else (res[0], None)


def _matmul_kernel(relu2, nk, a_ref, as_ref, w_ref, o_ref, os_ref, *scratch):
    i = pl.program_id(1)
    k = pl.program_id(2)

    def finish(v):
        return jnp.square(jnp.maximum(v, 0.0)) if relu2 else v

    if nk == 1:
        o_ref[...] = finish(jnp.dot(a_ref[...], w_ref[...], preferred_element_type=F32)).astype(o_ref.dtype)

        @pl.when(i == 0)
        def _():
            os_ref[...] = finish(jnp.dot(as_ref[...], w_ref[...], preferred_element_type=F32)).astype(os_ref.dtype)
    else:
        acc_ref, accs_ref = scratch

        @pl.when(k == 0)
        def _():
            acc_ref[...] = jnp.zeros_like(acc_ref)

        acc_ref[...] += jnp.dot(a_ref[...], w_ref[...], preferred_element_type=F32)

        @pl.when(k == nk - 1)
        def _():
            o_ref[...] = finish(acc_ref[...]).astype(o_ref.dtype)

        @pl.when(i == 0)
        def _():
            @pl.when(k == 0)
            def _():
                accs_ref[...] = jnp.zeros_like(accs_ref)

            accs_ref[...] += jnp.dot(as_ref[...], w_ref[...], preferred_element_type=F32)

            @pl.when(k == nk - 1)
            def _():
                os_ref[...] = finish(accs_ref[...]).astype(os_ref.dtype)


def _matmul(a, a_s, w, out_dtype=F32, relu2=False, tm=1024, tn=1024, tk=4096):
    m, kdim = a.shape
    n = w.shape[1]
    rs = a_s.shape[0]
    tn = min(tn, n)
    tk = min(tk, kdim)
    nk = kdim // tk
    scratch = [] if nk == 1 else [pltpu.VMEM((tm, tn), F32), pltpu.VMEM((rs, tn), F32)]
    return pl.pallas_call(
        functools.partial(_matmul_kernel, relu2, nk),
        out_shape=[jax.ShapeDtypeStruct((m, n), out_dtype), jax.ShapeDtypeStruct((rs, n), out_dtype)],
        grid=(n // tn, m // tm, nk),
        in_specs=[pl.BlockSpec((tm, tk), lambda j, i, k: (i, k)),
                  pl.BlockSpec((rs, tk), lambda j, i, k: (0, k)),
                  pl.BlockSpec((tk, tn), lambda j, i, k: (k, j))],
        out_specs=[pl.BlockSpec((tm, tn), lambda j, i, k: (i, j)),
                   pl.BlockSpec((rs, tn), lambda j, i, k: (0, j))],
        scratch_shapes=scratch,
        compiler_params=_cparams("arbitrary", "arbitrary", "arbitrary"),
        name="matmul_relu2" if relu2 else "matmul",
    )(a, a_s, w)


def _gates_kernel(n_heads, zg_ref, alog_ref, dtb_ref, gc_ref, beta_ref):
    zg = zg_ref[...]
    rows = zg.shape[0]
    g = -jnp.exp(alog_ref[...]) * _softplus(zg[:, :n_heads] + dtb_ref[...])
    beta = _sigmoid(zg[:, n_heads:])
    pos = lax.broadcasted_iota(jnp.int32, (rows, HEAD_DIM), 0) % DELTA_CHUNK
    for h in range(n_heads):
        gh = jnp.broadcast_to(g[:, h:h + 1], (rows, HEAD_DIM))
        s = 1
        while s < DELTA_CHUNK:
            gh = gh + jnp.where(pos >= s, pltpu.roll(gh, s, 0), 0.0)
            s *= 2
        gc_ref[h] = gh
        beta_ref[h] = jnp.broadcast_to(beta[:, h:h + 1], (rows, HEAD_DIM))


def _gates(zg, a_log, dt_bias, rows=256):
    b, t, h2 = zg.shape
    nh = h2 // 2
    out = jax.ShapeDtypeStruct((b, nh, t, HEAD_DIM), F32)
    ospec = pl.BlockSpec((None, nh, rows, HEAD_DIM), lambda a, i: (a, 0, i, 0))
    vec = pl.BlockSpec((1, nh), lambda a, i: (0, 0))
    return pl.pallas_call(
        functools.partial(_gates_kernel, nh),
        out_shape=[out, out],
        grid=(b, t // rows),
        in_specs=[pl.BlockSpec((None, rows, h2), lambda a, i: (a, i, 0)), vec, vec],
        out_specs=[ospec, ospec],
        compiler_params=_cparams("arbitrary", "arbitrary"),
        name="delta_gates",
    )(zg, a_log.reshape(1, nh), dt_bias.reshape(1, nh))


def _delta_prompt_kernel(nh, q_ref, k_ref, v_ref, za_ref, cq_ref, ck_ref, cv_ref, gc_ref, beta_ref, norm_ref,
                         o_ref, s_ref, wq_ref, ka_ref, u_ref, eg_ref):
    t = q_ref.shape[0]
    sb = DELTA_SUPER
    c = DELTA_CHUNK
    n_sb = t // sb
    cps = sb // c
    half = sb // 2
    ii = lax.broadcasted_iota(jnp.int32, (sb, sb), 0)
    jj = lax.broadcasted_iota(jnp.int32, (sb, sb), 1)
    lower = ((ii // c) == (jj // c)) & (ii >= jj)
    strict = ii > jj
    row = lax.broadcasted_iota(jnp.int32, (sb, HEAD_DIM), 0)
    pair_chunk = lax.broadcasted_iota(jnp.int32, (HEAD_DIM, 2 * c), 1) // c
    levels = int(math.log2(c))

    def l2norm(x):
        return x * lax.rsqrt(jnp.sum(x * x, axis=-1, keepdims=True) + 1e-6)

    def prepare(n, first, hh):
        r0 = 0 if first else pl.multiple_of(n * sb, sb)
        rows = pl.ds(r0, sb)
        lanes = slice(hh * HEAD_DIM, (hh + 1) * HEAD_DIM)

        def conv_silu(x_ref, w_ref):
            x = x_ref[rows, lanes]
            w = w_ref[:, lanes]
            if first:
                prev = jnp.zeros((8, HEAD_DIM), F32)
            else:
                prev = x_ref[pl.ds(pl.multiple_of(r0 - 8, 8), 8), lanes]
            y = x * w[CONV_W - 1:CONV_W]
            for s in range(1, CONV_W):
                head = jnp.broadcast_to(pltpu.roll(prev, s, 0)[None], (sb // 8, 8, HEAD_DIM))
                xs = jnp.where(row < s, head.reshape(sb, HEAD_DIM), pltpu.roll(x, s, 0))
                y = y + xs * w[CONV_W - 1 - s:CONV_W - s]
            return _silu(y)

        q = l2norm(conv_silu(q_ref, cq_ref)) * (HEAD_DIM ** -0.5)
        k = l2norm(conv_silu(k_ref, ck_ref))
        v = conv_silu(v_ref, cv_ref)
        gc = gc_ref[hh, rows]
        beta = beta_ref[hh, rows]
        kb = k * beta
        eg = jnp.exp(gc)
        gct = jnp.concatenate([gc[:half].T, gc[half:].T], axis=1)
        gc_j = jnp.broadcast_to(gct[0:1], (sb, sb))
        gc_i = jnp.concatenate([gc] * (sb // HEAD_DIM), axis=1)
        dec = jnp.exp(jnp.where(lower, gc_i - gc_j, -jnp.inf))
        kq = _dot_nt(jnp.concatenate([kb, q], axis=0), k)
        yield
        x = -jnp.where(strict, kq[:sb] * dec, 0.0)
        attn = kq[sb:] * dec
        p = jnp.where(ii == jj, 1.0, x)
        xb = x.astype(BF16)
        x = jnp.dot(xb, xb, preferred_element_type=F32)
        yield
        for lvl in range(1, levels):
            xb = x.astype(BF16)
            if lvl < levels - 1:
                y = jnp.dot(jnp.concatenate([p.astype(BF16), xb], axis=0), xb, preferred_element_type=F32)
                yield
                p = p + y[:sb]
                x = y[sb:]
            else:
                y = jnp.dot(p.astype(BF16), xb, preferred_element_type=F32)
                yield
                p = p + y
        uw = _dot(p, jnp.concatenate([v * beta, kb * eg], axis=1))
        yield
        u_ref[hh, rows] = uw[:, :HEAD_DIM]
        w = uw[:, HEAD_DIM:]
        qd = q * eg
        gl = jnp.concatenate([jnp.broadcast_to(gc[ci * c + c - 1:ci * c + c], (c, HEAD_DIM))
                              for ci in range(cps)], axis=0)
        kd = k * jnp.exp(gl - gc)
        kdt = jnp.concatenate([kd[:half].T, kd[half:].T], axis=1)
        for ci in range(cps):
            cc = n * cps + ci
            cr = slice(ci * c, (ci + 1) * c)
            pair = slice((ci // 2) * 2 * c, (ci // 2 + 1) * 2 * c)
            wq_ref[hh, cc] = jnp.concatenate([w[cr], qd[cr]], axis=0).astype(BF16)
            ka_ref[hh, cc] = jnp.concatenate([jnp.where(pair_chunk == ci % 2, kdt[:, pair], 0.0),
                                              attn[cr, pair]], axis=0).astype(BF16)
            eg_ref[hh, cc] = jnp.exp(gl[ci * c:ci * c + 8])

    def recur(n, states, static):
        for ci in range(cps):
            cc = n * cps + ci
            crow = pl.ds(cc * c if static else pl.multiple_of(cc * c, c), c)
            r1 = [jnp.dot(wq_ref[hh, cc], states[hh].astype(BF16), preferred_element_type=F32) for hh in range(nh)]
            yield
            r2 = []
            for hh in range(nh):
                v_new = (u_ref[hh, crow] - r1[hh][:c]).astype(BF16)
                r2.append(jnp.dot(ka_ref[hh, cc], jnp.concatenate([v_new, v_new], axis=0),
                                  preferred_element_type=F32))
            yield
            for hh in range(nh):
                lanes = slice(hh * HEAD_DIM, (hh + 1) * HEAD_DIM)
                o = r1[hh][c:] + r2[hh][HEAD_DIM:]
                states[hh] = states[hh] * eg_ref[hh, cc][0:1] + r2[hh][:HEAD_DIM]
                o_ref[crow, lanes] = (_rms_rows(o) * norm_ref[...] * _silu(za_ref[crow, lanes])).astype(o_ref.dtype)

    def interleave(gens):
        live = list(gens)
        while live:
            live = [g for g in live if next(g, live) is not live]

    def body(n, states):
        states = list(states)
        interleave([prepare(n, False, hh) for hh in range(nh)] + [recur(n - 1, states, False)])
        return tuple(states)

    interleave([prepare(0, True, hh) for hh in range(nh)])
    states = lax.fori_loop(1, n_sb, body, tuple(jnp.zeros((HEAD_DIM, HEAD_DIM), F32) for _ in range(nh)))
    states = list(states)
    interleave([recur(n_sb - 1, states, True)])
    for hh in range(nh):
        s_ref[hh] = states[hh]


def _delta_prompt(z_main, conv_w, gc, beta, norm_w, n_heads, hpb=2):
    b, t, _ = z_main.shape
    wl = hpb * HEAD_DIM
    ng = n_heads // hpb
    n_chunks = t // DELTA_CHUNK
    col = lambda off: pl.BlockSpec((None, t, wl), lambda a, h: (a, 0, off + h))
    cw = lambda off: pl.BlockSpec((CONV_W, wl), lambda a, h: (0, off + h))
    gate = pl.BlockSpec((None, hpb, t, HEAD_DIM), lambda a, h: (a, h, 0, 0))
    return pl.pallas_call(
        functools.partial(_delta_prompt_kernel, hpb),
        out_shape=[jax.ShapeDtypeStruct((b, t, n_heads * HEAD_DIM), BF16),
                   jax.ShapeDtypeStruct((b, n_heads, HEAD_DIM, HEAD_DIM), F32)],
        grid=(b, ng),
        in_specs=[col(0), col(ng), col(2 * ng), col(3 * ng),
                  cw(0), cw(ng), cw(2 * ng), gate, gate,
                  pl.BlockSpec((1, HEAD_DIM), lambda a, h: (0, 0))],
        out_specs=[pl.BlockSpec((None, t, wl), lambda a, h: (a, 0, h)),
                   pl.BlockSpec((None, hpb, HEAD_DIM, HEAD_DIM), lambda a, h: (a, h, 0, 0))],
        scratch_shapes=[pltpu.VMEM((hpb, n_chunks, 2 * DELTA_CHUNK, HEAD_DIM), BF16),
                        pltpu.VMEM((hpb, n_chunks, HEAD_DIM + DELTA_CHUNK, 2 * DELTA_CHUNK), BF16),
                        pltpu.VMEM((hpb, t, HEAD_DIM), F32),
                        pltpu.VMEM((hpb, n_chunks, 8, HEAD_DIM), F32)],
        compiler_params=_cparams("arbitrary", "arbitrary"),
        name="delta_prompt",
    )(z_main, z_main, z_main, z_main, conv_w, conv_w, conv_w, gc, beta, norm_w.reshape(1, HEAD_DIM))


def _xpos(x, cos, sin_e, sin_o):
    return x * cos + pltpu.roll(x, HEAD_DIM - 1, 1) * sin_e + pltpu.roll(x, 1, 1) * sin_o


def _log_gamma(h):
    return jnp.log1p(-jnp.exp2(-5.0 - h))


def _ret_prompt_kernel(q_ref, k_ref, v_ref, g_ref, cos_ref, sine_ref, sino_ref, o_ref, s_ref, qs_ref, ks_ref):
    t = q_ref.shape[0]
    c = RET_CHUNK
    cos, sin_e, sin_o = cos_ref[...], sine_ref[...], sino_ref[...]
    qs_ref[...] = _xpos(q_ref[...], cos, sin_e, sin_o)
    ks_ref[...] = _xpos(k_ref[...], cos, sin_e, sin_o) * (HEAD_DIM ** -0.5)

    h = jnp.full((1, 1), pl.program_id(1), jnp.int32).astype(F32)
    lg = _log_gamma(h)
    ii = lax.broadcasted_iota(jnp.int32, (c, c), 0)
    jj = lax.broadcasted_iota(jnp.int32, (c, c), 1)
    dec = jnp.exp(jnp.where(ii >= jj, (ii - jj).astype(F32) * lg, -jnp.inf))
    pos = lax.broadcasted_iota(jnp.int32, (c, HEAD_DIM), 0).astype(F32)
    e_in = jnp.exp((pos + 1.0) * lg)
    e_out = jnp.exp((c - 1.0 - pos) * lg)
    e_all = jnp.exp(c * lg)

    unroll = 2

    def body(n, s):
        rows = [pl.ds(pl.multiple_of((n * unroll + r) * c, c), c) for r in range(unroll)]
        q = [qs_ref[rw] for rw in rows]
        k = [ks_ref[rw] for rw in rows]
        v = [v_ref[rw].astype(BF16) for rw in rows]
        qk = [_dot_nt(q[r], k[r]) for r in range(unroll)]
        upd = [_dot((k[r] * e_out).T, v[r]) for r in range(unroll)]
        for r in range(unroll):
            o = _dot(q[r] * e_in, s) + _dot(qk[r] * dec, v[r])
            s = s * e_all + upd[r]
            o_ref[rows[r]] = (_rms_rows(o) * _silu(g_ref[rows[r]])).astype(o_ref.dtype)
        return s

    s_ref[...] = lax.fori_loop(0, t // (c * unroll), body, jnp.zeros((HEAD_DIM, HEAD_DIM), F32))


def _ret_prompt(z_tail, cos, sin_e, sin_o, n_heads):
    b, t, _ = z_tail.shape
    col = lambda off: pl.BlockSpec((None, t, HEAD_DIM), lambda a, h: (a, 0, off + h))
    tab = pl.BlockSpec((t, HEAD_DIM), lambda a, h: (0, 0))
    return pl.pallas_call(
        _ret_prompt_kernel,
        out_shape=[jax.ShapeDtypeStruct((b, t, n_heads * HEAD_DIM), BF16),
                   jax.ShapeDtypeStruct((b, n_heads, HEAD_DIM, HEAD_DIM), F32)],
        grid=(b, n_heads),
        in_specs=[col(0), col(n_heads), col(2 * n_heads), col(3 * n_heads), tab, tab, tab],
        out_specs=[pl.BlockSpec((None, t, HEAD_DIM), lambda a, h: (a, 0, h)),
                   pl.BlockSpec((None, None, HEAD_DIM, HEAD_DIM), lambda a, h: (a, h, 0, 0))],
        scratch_shapes=[pltpu.VMEM((t, HEAD_DIM), F32)] * 2,
        compiler_params=_cparams("arbitrary", "arbitrary"),
        name="ret_prompt",
    )(z_tail, z_tail, z_tail, z_tail, cos, sin_e, sin_o)


def _lambda(lq, lam_init):
    l01 = jnp.sum(lq[0:1] * lq[1:2], axis=-1, keepdims=True)
    l23 = jnp.sum(lq[2:3] * lq[3:4], axis=-1, keepdims=True)
    return jnp.exp(l01) - jnp.exp(l23) + lam_init


def _attn_prompt_kernel(lam_init, nh, tq, q_ref, k_ref, v_ref, lq_ref, sub_ref, o_ref):
    i = pl.program_id(2)
    lane = lax.broadcasted_iota(jnp.int32, (tq, HEAD_DIM), 1)
    scale = HALF_DIM ** -0.5
    q2 = []
    for hh in range(nh):
        q = q_ref[:, hh * HEAD_DIM:(hh + 1) * HEAD_DIM]
        q2.append((jnp.concatenate([jnp.where(lane < HALF_DIM, q, 0.0), jnp.where(lane >= HALF_DIM, q, 0.0)],
                                   axis=0) * scale).astype(BF16))
    causal = (lax.broadcasted_iota(jnp.int32, (2 * tq, tq), 1)
              <= lax.broadcasted_iota(jnp.int32, (2 * tq, tq), 0) % tq)

    def step(j, carry, diagonal):
        rows = pl.ds(pl.multiple_of(j * tq, tq), tq)
        out = []
        scores = [_dot_nt(q2[hh], k_ref[rows, hh * HEAD_DIM:(hh + 1) * HEAD_DIM]) for hh in range(nh)]
        for hh in range(nh):
            lanes = slice(hh * HEAD_DIM, (hh + 1) * HEAD_DIM)
            m, l, acc = carry[hh]
            s = scores[hh]
            if diagonal:
                s = jnp.where(causal, s, -jnp.inf)
            m_new = jnp.maximum(m, jnp.max(s, axis=-1, keepdims=True))
            alpha = jnp.exp(m - m_new)
            p = jnp.exp(s - m_new)
            l = alpha * l + jnp.sum(p, axis=-1, keepdims=True)
            acc = alpha * acc + _dot(p, v_ref[rows, lanes])
            out.append((m_new, l, acc))
        return tuple(out)

    init = tuple((jnp.full((2 * tq, 1), -jnp.inf, F32), jnp.zeros((2 * tq, 1), F32),
                  jnp.zeros((2 * tq, HEAD_DIM), F32)) for _ in range(nh))
    carry = lax.fori_loop(0, i, lambda j, cr: step(j, cr, False), init)
    carry = step(i, carry, True)
    lam = _lambda(lq_ref[...], lam_init)
    for hh in range(nh):
        m, l, acc = carry[hh]
        o2 = acc / l
        o = o2[:tq] - lam * o2[tq:]
        o_ref[:, hh * HEAD_DIM:(hh + 1) * HEAD_DIM] = (_rms_rows(o) * sub_ref[...] * (1.0 - lam_init)).astype(o_ref.dtype)


def _attn_prompt(z_tail, lambda_qk, subln, lam_init, n_heads, col0, tq=256, hpb=2):
    b, t, _ = z_tail.shape
    wl = hpb * HEAD_DIM
    ng = n_heads // hpb
    c0 = col0 // hpb
    return pl.pallas_call(
        functools.partial(_attn_prompt_kernel, lam_init, hpb, tq),
        out_shape=jax.ShapeDtypeStruct((b, t, n_heads * HEAD_DIM), BF16),
        grid=(b, ng, t // tq),
        in_specs=[pl.BlockSpec((None, tq, wl), lambda a, h, i: (a, i, c0 + h)),
                  pl.BlockSpec((None, t, wl), lambda a, h, i: (a, 0, c0 + ng + h)),
                  pl.BlockSpec((None, t, wl), lambda a, h, i: (a, 0, c0 + 2 * ng + h)),
                  pl.BlockSpec((4, HALF_DIM), lambda a, h, i: (0, 0)),
                  pl.BlockSpec((1, HEAD_DIM), lambda a, h, i: (0, 0))],
        out_specs=pl.BlockSpec((None, tq, wl), lambda a, h, i: (a, i, h)),
        compiler_params=_cparams("arbitrary", "arbitrary", "arbitrary"),
        name="attn_prompt",
    )(z_tail, z_tail, z_tail, lambda_qk, subln.reshape(1, HEAD_DIM))


def _lane_pick(x, h):
    lane = lax.broadcasted_iota(jnp.int32, x.shape, 1)
    return jnp.sum(jnp.where(lane == h, x, 0.0), axis=-1, keepdims=True)


def _row_to_col(x):
    ii = lax.broadcasted_iota(jnp.int32, (HEAD_DIM, HEAD_DIM), 0)
    jj = lax.broadcasted_iota(jnp.int32, (HEAD_DIM, HEAD_DIM), 1)
    return jnp.sum(jnp.where(ii == jj, jnp.broadcast_to(x, (HEAD_DIM, HEAD_DIM)), 0.0), axis=-1, keepdims=True)


def _rows8(*rows):
    pad = jnp.zeros((8 - len(rows), HEAD_DIM), F32)
    return jnp.concatenate(list(rows) + [pad], axis=0)


def _delta_sample_kernel(q_ref, k_ref, v_ref, za_ref, bq_ref, bk_ref, bv_ref, cq_ref, ck_ref, cv_ref,
                         zg_ref, alog_ref, dtb_ref, norm_ref, s_ref, o_ref, so_ref):
    h = pl.program_id(1)
    n_heads = alog_ref.shape[1]

    def conv_silu(x_ref, buf_ref, w_ref):
        w = w_ref[...]
        buf = buf_ref[...]
        y = x_ref[...] * w[CONV_W - 1:CONV_W]
        for j in range(CONV_W - 1):
            y = y + buf[j:j + 1] * w[j:j + 1]
        return _silu(y)

    def l2norm(x):
        return x * lax.rsqrt(jnp.sum(x * x, axis=-1, keepdims=True) + 1e-6)

    q = l2norm(conv_silu(q_ref, bq_ref, cq_ref)) * (HEAD_DIM ** -0.5)
    k = l2norm(conv_silu(k_ref, bk_ref, ck_ref))
    v = conv_silu(v_ref, bv_ref, cv_ref)
    zg = zg_ref[...]
    a_a = _lane_pick(zg, h)
    b_a = _lane_pick(zg, h + n_heads)
    g = -jnp.exp(_lane_pick(alog_ref[...], h)) * _softplus(a_a + _lane_pick(dtb_ref[...], h))
    beta = _sigmoid(b_a)
    eg = jnp.exp(g)
    s = s_ref[...]
    ks_qs = _dot(_rows8(k, q), s)
    v_new = beta * (v - eg * ks_qs[0:1])
    qk = jnp.sum(q * k, axis=-1, keepdims=True)
    o = eg * ks_qs[1:2] + qk * v_new
    so_ref[...] = s * eg + _row_to_col(k) * v_new
    o_ref[...] = (_rms_rows(o) * norm_ref[...] * _silu(za_ref[...])).astype(o_ref.dtype)


def _delta_sample(zs_main, zs_gate, conv_buf, conv_w, a_log, dt_bias, norm_w, state, layer, n_heads):
    nb = state.shape[0]
    col = lambda off: pl.BlockSpec((None, 1, HEAD_DIM), lambda a, h: (a, 0, off + h))
    buf = lambda off: pl.BlockSpec((None, None, CONV_W - 1, HEAD_DIM), lambda a, h: (a, layer, 0, off + h))
    cw = lambda off: pl.BlockSpec((CONV_W, HEAD_DIM), lambda a, h: (0, off + h))
    vec = pl.BlockSpec((1, n_heads), lambda a, h: (0, 0))
    return pl.pallas_call(
        _delta_sample_kernel,
        out_shape=[jax.ShapeDtypeStruct((nb, 1, n_heads * HEAD_DIM), BF16),
                   jax.ShapeDtypeStruct((nb, n_heads, HEAD_DIM, HEAD_DIM), F32)],
        grid=(nb, n_heads),
        in_specs=[col(0), col(n_heads), col(2 * n_heads), col(3 * n_heads),
                  buf(0), buf(n_heads), buf(2 * n_heads), cw(0), cw(n_heads), cw(2 * n_heads),
                  pl.BlockSpec((None, 1, 2 * n_heads), lambda a, h: (a, 0, 0)), vec, vec,
                  pl.BlockSpec((1, HEAD_DIM), lambda a, h: (0, 0)),
                  pl.BlockSpec((None, None, None, HEAD_DIM, HEAD_DIM), lambda a, h: (a, layer, h, 0, 0))],
        out_specs=[pl.BlockSpec((None, 1, HEAD_DIM), lambda a, h: (a, 0, h)),
                   pl.BlockSpec((None, None, HEAD_DIM, HEAD_DIM), lambda a, h: (a, h, 0, 0))],
        compiler_params=_cparams("arbitrary", "arbitrary"),
        name="delta_sample",
    )(zs_main, zs_main, zs_main, zs_main, conv_buf, conv_buf, conv_buf, conv_w, conv_w, conv_w,
      zs_gate, a_log.reshape(1, n_heads), dt_bias.reshape(1, n_heads), norm_w.reshape(1, HEAD_DIM), state)


def _ret_sample_kernel(q_ref, k_ref, v_ref, g_ref, cos_ref, sine_ref, sino_ref, s_ref, o_ref, so_ref):
    h = jnp.full((1, 1), pl.program_id(1), jnp.int32).astype(F32)
    gamma = jnp.exp(_log_gamma(h))
    cos, sin_e, sin_o = cos_ref[...], sine_ref[...], sino_ref[...]
    q = _xpos(q_ref[...], cos, sin_e, sin_o)
    k = _xpos(k_ref[...], cos, sin_e, sin_o) * (HEAD_DIM ** -0.5)
    v = v_ref[...]
    s = s_ref[...]
    qs = _dot(_rows8(q), s)
    qk = jnp.sum(q * k, axis=-1, keepdims=True)
    o = qk * v + gamma * qs[0:1]
    so_ref[...] = s * gamma + _row_to_col(k) * v
    o_ref[...] = (_rms_rows(o) * _silu(g_ref[...])).astype(o_ref.dtype)


def _ret_sample(zs_tail, cos, sin_e, sin_o, state, layer, n_heads):
    nb = state.shape[0]
    col = lambda off: pl.BlockSpec((None, 1, HEAD_DIM), lambda a, h: (a, 0, off + h))
    tab = pl.BlockSpec((1, HEAD_DIM), lambda a, h: (0, 0))
    return pl.pallas_call(
        _ret_sample_kernel,
        out_shape=[jax.ShapeDtypeStruct((nb, 1, n_heads * HEAD_DIM), BF16),
                   jax.ShapeDtypeStruct((nb, n_heads, HEAD_DIM, HEAD_DIM), F32)],
        grid=(nb, n_heads),
        in_specs=[col(0), col(n_heads), col(2 * n_heads), col(3 * n_heads), tab, tab, tab,
                  pl.BlockSpec((None, None, None, HEAD_DIM, HEAD_DIM), lambda a, h: (a, layer, h, 0, 0))],
        out_specs=[pl.BlockSpec((None, 1, HEAD_DIM), lambda a, h: (a, 0, h)),
                   pl.BlockSpec((None, None, HEAD_DIM, HEAD_DIM), lambda a, h: (a, h, 0, 0))],
        compiler_params=_cparams("arbitrary", "arbitrary"),
        name="ret_sample",
    )(zs_tail, zs_tail, zs_tail, zs_tail, cos, sin_e, sin_o, state)


def _attn_decode_kernel(lam_init, ppb, pt_ref, q_ref, kn_ref, vn_ref, *rest):
    kc_refs, vc_refs = rest[:ppb], rest[ppb:2 * ppb]
    lq_ref, sub_ref, o_ref, m_ref, l_ref, acc_ref = rest[2 * ppb:]
    j = pl.program_id(1)
    n_steps = pl.num_programs(1)
    nh = q_ref.shape[0]
    scale = HALF_DIM ** -0.5
    di = lax.broadcasted_iota(jnp.int32, (HEAD_DIM, 2 * HEAD_DIM), 0)
    ci = lax.broadcasted_iota(jnp.int32, (HEAD_DIM, 2 * HEAD_DIM), 1)
    expand = ((di // HALF_DIM) == (ci // HEAD_DIM)).astype(BF16)
    q = q_ref[...] * scale

    @pl.when(j == 0)
    def _():
        m_ref[...] = jnp.full(m_ref.shape, -jnp.inf, F32)
        l_ref[...] = jnp.zeros_like(l_ref)
        acc_ref[...] = jnp.zeros_like(acc_ref)

    def scores(kt):
        n_tok = kt.shape[0]
        prod = (kt * q[None]).reshape(n_tok * nh, HEAD_DIM)
        s = jnp.dot(prod.astype(BF16), expand, preferred_element_type=F32)
        return s.reshape(n_tok, nh, 2 * HEAD_DIM)

    def update(s, vt):
        m_old = m_ref[...]
        m_new = jnp.maximum(m_old, jnp.max(s, axis=0))
        alpha = jnp.exp(m_old - m_new)
        p = jnp.exp(s - m_new[None])
        l_ref[...] = alpha * l_ref[...] + jnp.sum(p, axis=0)
        pv = jnp.concatenate([jnp.sum(p[:, :, :HEAD_DIM] * vt, axis=0),
                              jnp.sum(p[:, :, HEAD_DIM:] * vt, axis=0)], axis=-1)
        acc_ref[...] = alpha * acc_ref[...] + pv
        m_ref[...] = m_new

    page_scores = [scores(kc_ref[...]) for kc_ref in kc_refs]
    for s, vc_ref in zip(page_scores, vc_refs):
        update(s, vc_ref[...])

    @pl.when(j == n_steps - 1)
    def _():
        update(scores(kn_ref[...][None]), vn_ref[...][None])
        o2 = acc_ref[...] / l_ref[...]
        lam = _lambda(lq_ref[...], lam_init)
        o = o2[:, :HEAD_DIM] - lam * o2[:, HEAD_DIM:]
        o_ref[...] = (_rms_rows(o) * sub_ref[...] * (1.0 - lam_init)).astype(o_ref.dtype)


def _attn_decode(q, k_new, v_new, cache_k, cache_v, page_table, lambda_qk, subln, lam_init, layer, ppb=4):
    nb, nh, _ = q.shape
    n_pages = page_table.shape[1]
    assert n_pages % ppb == 0
    tok = pl.BlockSpec((None, nh, HEAD_DIM), lambda b, j, pt: (b, 0, 0))

    def page(r):
        return pl.BlockSpec((None, None, PAGE_SIZE, nh, HEAD_DIM),
                            lambda b, j, pt: (pt[b, j * ppb + r], layer, 0, 0, 0))

    pages = [page(r) for r in range(ppb)]
    grid_spec = pltpu.PrefetchScalarGridSpec(
        num_scalar_prefetch=1,
        grid=(nb, n_pages // ppb),
        in_specs=[tok, tok, tok] + pages + pages +
                 [pl.BlockSpec((4, HALF_DIM), lambda b, j, pt: (0, 0)),
                  pl.BlockSpec((1, HEAD_DIM), lambda b, j, pt: (0, 0))],
        out_specs=tok,
        scratch_shapes=[pltpu.VMEM((nh, 2 * HEAD_DIM), F32)] * 3,
    )
    return pl.pallas_call(
        functools.partial(_attn_decode_kernel, lam_init, ppb),
        out_shape=jax.ShapeDtypeStruct((nb, nh, HEAD_DIM), BF16),
        grid_spec=grid_spec,
        compiler_params=_cparams("arbitrary", "arbitrary"),
        name="attn_decode",
    )(page_table, q, k_new, v_new, *([cache_k] * ppb), *([cache_v] * ppb), lambda_qk, subln.reshape(1, HEAD_DIM))


def _xpos_tables(pos):
    half = HEAD_DIM // 2
    angle = jnp.repeat(1.0 / (10000.0 ** jnp.linspace(0.0, 1.0, half, dtype=F32)), 2)
    ph = pos.astype(F32)[:, None] * angle
    sin, cos = jnp.sin(ph), jnp.cos(ph)
    even = (jnp.arange(HEAD_DIM) % 2) == 0
    return cos, jnp.where(even, -sin, 0.0), jnp.where(even, 0.0, sin)


def kernel(x_prompt, x_sample, state_conv_a, state_delta, state_ret, cache_k, cache_v, page_table, c_prompt, c_sample, w_ada, b_ada, w_in, conv_a, a_log, dt_bias, norm_a, lambda_qk, subln_c, w_out, ln1_g, ln1_b, w_up, w_down, ln2_g, ln2_b):
    bp, t, d = x_prompt.shape
    nb = x_sample.shape[0]
    depth = w_in.shape[0]
    h_a = state_delta.shape[2]
    h_b = state_ret.shape[2]
    h_c = cache_k.shape[3]
    a_w, b_w, c_w = h_a * HEAD_DIM, h_b * HEAD_DIM, h_c * HEAD_DIM
    n_main = 4 * a_w
    n_gate = 2 * h_a
    past_len = page_table.shape[1] * PAGE_SIZE
    alpha = (2 * depth) ** 0.25
    rs = SAMPLE_ROWS
    assert nb + bp <= rs and x_sample.shape[1] == 1

    c_all = jnp.concatenate([c_sample, c_prompt, jnp.zeros((rs - nb - bp, d), F32)], axis=0)
    mod = _ada(c_all, w_ada, b_ada)

    def mod_p(l, i):
        return mod[l, nb:nb + bp, None, i * d:(i + 1) * d]

    def mod_s(l, i):
        return mod[l][None, :, i * d:(i + 1) * d]

    cos_p, sine_p, sino_p = _xpos_tables(jnp.arange(t))
    cos_s, sine_s, sino_s = _xpos_tables(past_len + jnp.arange(1))

    xp = x_prompt
    xs = jnp.concatenate([x_sample.reshape(nb, d), jnp.zeros((rs - nb, d), F32)], axis=0)[None]
    hp = _modulate(xp, mod_p(0, 1), mod_p(0, 0), 256)
    hs = _modulate(xs, mod_s(0, 1), mod_s(0, 0), rs)

    st_p, st_s = [], []
    for l in range(depth):
        lam_init = 0.8 - 0.6 * math.exp(-0.3 * l)
        w_l = w_in[l]
        w_main = w_l[:, :n_main].astype(BF16)
        w_gate = w_l[:, n_main:n_main + n_gate].astype(BF16)
        w_tail = w_l[:, n_main + n_gate:].astype(BF16)
        hp2 = hp.reshape(bp * t, d)
        hs2 = hs.reshape(rs, d)
        zp_main, zs_main = _matmul(hp2, hs2, w_main)
        zp_gate, zs_gate = _matmul(hp2, hs2, w_gate)
        zp_tail, zs_tail = _matmul(hp2, hs2, w_tail)
        zp_main = zp_main.reshape(bp, t, n_main)
        zp_tail = zp_tail.reshape(bp, t, -1)

        gc, beta = _gates(zp_gate.reshape(bp, t, n_gate), a_log[l], dt_bias[l])
        oa_p, delta_p = _delta_prompt(zp_main, conv_a[l], gc, beta, norm_a[l], h_a)
        ob_p, ret_p = _ret_prompt(zp_tail, cos_p, sine_p, sino_p, h_b)
        oc_p = _attn_prompt(zp_tail, lambda_qk[l], subln_c[l], lam_init, h_c, 4 * h_b)
        k_off = 4 * b_w + c_w
        conv_p = zp_main[:, t - (CONV_W - 1):, :3 * a_w]
        k_p = zp_tail[:, :, k_off:k_off + c_w].reshape(bp, t, h_c, HEAD_DIM)
        v_p = zp_tail[:, :, k_off + c_w:k_off + 2 * c_w].reshape(bp, t, h_c, HEAD_DIM)
        st_p.append((conv_p, delta_p, ret_p, k_p, v_p))

        zs_m = zs_main[:nb]
        zs_t = zs_tail[:nb]
        oa_s, delta_s = _delta_sample(zs_m[:, None], zs_gate[:nb, None], state_conv_a, conv_a[l], a_log[l],
                                      dt_bias[l], norm_a[l], state_delta, l, h_a)
        ob_s, ret_s = _ret_sample(zs_t[:, None], cos_s, sine_s, sino_s, state_ret, l, h_b)
        q_s = zs_t[:, 4 * b_w:4 * b_w + c_w].reshape(nb, h_c, HEAD_DIM)
        k_s = zs_t[:, k_off:k_off + c_w].reshape(nb, h_c, HEAD_DIM)
        v_s = zs_t[:, k_off + c_w:k_off + 2 * c_w].reshape(nb, h_c, HEAD_DIM)
        oc_s = _attn_decode(q_s, k_s, v_s, cache_k, cache_v, page_table, lambda_qk[l], subln_c[l], lam_init, l)
        conv_s = jnp.concatenate([state_conv_a[:, l, 1:], zs_m[:, None, :3 * a_w]], axis=1)
        st_s.append((conv_s, delta_s, ret_s, k_s[:, None], v_s[:, None]))

        mix_p = jnp.concatenate([oa_p, ob_p, oc_p], axis=-1).reshape(bp * t, -1)
        mix_s = jnp.concatenate([oa_s.reshape(nb, a_w), ob_s.reshape(nb, b_w), oc_s.reshape(nb, c_w)], axis=-1)
        mix_s = jnp.concatenate([mix_s, jnp.zeros((rs - nb, mix_s.shape[1]), BF16)], axis=0)
        yp, ys = _matmul(mix_p, mix_s, w_out[l].astype(BF16))
        xp, hp = _postnorm(xp, yp.reshape(bp, t, d), mod_p(l, 2), ln1_g[l], ln1_b[l], mod_p(l, 4), mod_p(l, 3),
                           alpha, 256)
        xs, hs = _postnorm(xs, ys[None], mod_s(l, 2), ln1_g[l], ln1_b[l], mod_s(l, 4), mod_s(l, 3), alpha, rs)

        up, us = _matmul(hp.reshape(bp * t, d), hs.reshape(rs, d), w_up[l].astype(BF16), out_dtype=BF16, relu2=True)
        fp, fs = _matmul(up, us, w_down[l].astype(BF16))
        last = l == depth - 1
        nl = min(l + 1, depth - 1)
        xp, hp = _postnorm(xp, fp.reshape(bp, t, d), mod_p(l, 5), ln2_g[l], ln2_b[l], mod_p(nl, 1), mod_p(nl, 0),
                           alpha, 256, with_h=not last)
        xs, hs = _postnorm(xs, fs[None], mod_s(l, 5), ln2_g[l], ln2_b[l], mod_s(nl, 1), mod_s(nl, 0), alpha, rs,
                           with_h=not last)

    outs_p = tuple(jnp.stack([s[i] for s in st_p], axis=1) for i in range(5))
    outs_s = tuple(jnp.stack([s[i] for s in st_s], axis=1) for i in range(5))
    y_sample = xs[0, :nb].reshape(nb, 1, d)
    return (xp, y_sample) + outs_p + outs_s
```

```python
import functools
import math

import jax
import jax.numpy as jnp
from jax import lax
from jax.experimental import pallas as pl
from jax.experimental.pallas import tpu as pltpu

F32 = jnp.float32
BF16 = jnp.bfloat16

HEAD_DIM = 128
HALF_DIM = HEAD_DIM // 2
CONV_W = 4
DELTA_CHUNK = 64
DELTA_SUPER = 256
RET_CHUNK = 128
PAGE_SIZE = 128
EPS = 1e-5
SAMPLE_ROWS = 16
VMEM_LIMIT = 58 * 1024 * 1024


def _cparams(*sem):
    return pltpu.CompilerParams(dimension_semantics=sem, vmem_limit_bytes=VMEM_LIMIT)


def _sigmoid(x):
    return 1.0 / (1.0 + jnp.exp(-x))


def _silu(x):
    return x * _sigmoid(x)


def _softplus(x):
    return jnp.maximum(x, 0.0) + jnp.log1p(jnp.exp(-jnp.abs(x)))


def _dot(a, b):
    return jnp.dot(a.astype(BF16), b.astype(BF16), preferred_element_type=F32)


def _dot_nt(a, b):
    return lax.dot_general(a.astype(BF16), b.astype(BF16), (((1,), (1,)), ((), ())),
                           preferred_element_type=F32)


def _rms_rows(x):
    return x * lax.rsqrt(jnp.mean(x * x, axis=-1, keepdims=True) + EPS)


def _ada_kernel(c_ref, w0_ref, w1_ref, b_ref, o_ref):
    c = _silu(c_ref[...])
    dh = w0_ref.shape[0]
    o_ref[...] = _dot(c[:, :dh], w0_ref[...]) + _dot(c[:, dh:], w1_ref[...]) + b_ref[...]


def _ada(c_all, w_ada, b_ada, tn=1024):
    depth, d, n = w_ada.shape
    rows = c_all.shape[0]
    return pl.pallas_call(
        _ada_kernel,
        out_shape=jax.ShapeDtypeStruct((depth, rows, n), F32),
        grid=(depth, n // tn),
        in_specs=[pl.BlockSpec((rows, d), lambda l, j: (0, 0)),
                  pl.BlockSpec((None, d // 2, tn), lambda l, j: (l, 0, j)),
                  pl.BlockSpec((None, d // 2, tn), lambda l, j: (l, 1, j)),
                  pl.BlockSpec((None, 1, tn), lambda l, j: (l, 0, j))],
        out_specs=pl.BlockSpec((None, rows, tn), lambda l, j: (l, 0, j)),
        compiler_params=_cparams("arbitrary", "arbitrary"),
        name="ada_mod",
    )(c_all, w_ada, w_ada, b_ada.reshape(depth, 1, n))


def _modulate_kernel(x_ref, sc_ref, sh_ref, o_ref):
    o_ref[...] = (x_ref[...] * (1.0 + sc_ref[...]) + sh_ref[...]).astype(o_ref.dtype)


def _modulate(x, sc, sh, tm):
    g, r, d = x.shape
    rm = sc.shape[1]
    mspec = pl.BlockSpec((None, rm, d), lambda a, i: (a, 0, 0))
    return pl.pallas_call(
        _modulate_kernel,
        out_shape=jax.ShapeDtypeStruct((g, r, d), BF16),
        grid=(g, r // tm),
        in_specs=[pl.BlockSpec((None, tm, d), lambda a, i: (a, i, 0)), mspec, mspec],
        out_specs=pl.BlockSpec((None, tm, d), lambda a, i: (a, i, 0)),
        compiler_params=_cparams("arbitrary", "arbitrary"),
        name="modulate",
    )(x, sc, sh)


def _postnorm_kernel(alpha, with_h, x_ref, y_ref, gate_ref, lg_ref, lb_ref, sc_ref, sh_ref, xo_ref, *h_ref):
    v = alpha * x_ref[...] + (1.0 + gate_ref[...]) * y_ref[...]
    vc = v - jnp.mean(v, axis=-1, keepdims=True)
    var = jnp.mean(vc * vc, axis=-1, keepdims=True)
    xn = vc * lax.rsqrt(var + EPS) * lg_ref[...] + lb_ref[...]
    xo_ref[...] = xn
    if with_h:
        h_ref[0][...] = (xn * (1.0 + sc_ref[...]) + sh_ref[...]).astype(BF16)


def _postnorm(x, y, gate, ln_g, ln_b, sc, sh, alpha, tm, with_h=True):
    g, r, d = x.shape
    rm = gate.shape[1]
    row = pl.BlockSpec((None, tm, d), lambda a, i: (a, i, 0))
    mspec = pl.BlockSpec((None, rm, d), lambda a, i: (a, 0, 0))
    vec = pl.BlockSpec((1, d), lambda a, i: (0, 0))
    out_shape = [jax.ShapeDtypeStruct((g, r, d), F32)]
    out_specs = [row]
    if with_h:
        out_shape.append(jax.ShapeDtypeStruct((g, r, d), BF16))
        out_specs.append(row)
    res = pl.pallas_call(
        functools.partial(_postnorm_kernel, alpha, with_h),
        out_shape=out_shape,
        grid=(g, r // tm),
        in_specs=[row, row, mspec, vec, vec, mspec, mspec],
        out_specs=out_specs,
        compiler_params=_cparams("arbitrary", "arbitrary"),
        name="postnorm",
    )(x, y, gate, ln_g.reshape(1, d), ln_b.reshape(1, d), sc, sh)
    return (res[0], res[1]) if with_h else (res[0], None)


def _matmul_kernel(relu2, nk, a_ref, as_ref, w_ref, o_ref, os_ref, *scratch):
    i = pl.program_id(1)
    k = pl.program_id(2)

    def finish(v):
        return jnp.square(jnp.maximum(v, 0.0)) if relu2 else v

    if nk == 1:
        o_ref[...] = finish(jnp.dot(a_ref[...], w_ref[...], preferred_element_type=F32)).astype(o_ref.dtype)

        @pl.when(i == 0)
        def _():
            os_ref[...] = finish(jnp.dot(as_ref[...], w_ref[...], preferred_element_type=F32)).astype(os_ref.dtype)
    else:
        acc_ref, accs_ref = scratch

        @pl.when(k == 0)
        def _():
            acc_ref[...] = jnp.zeros_like(acc_ref)

        acc_ref[...] += jnp.dot(a_ref[...], w_ref[...], preferred_element_type=F32)

        @pl.when(k == nk - 1)
        def _():
            o_ref[...] = finish(acc_ref[...]).astype(o_ref.dtype)

        @pl.when(i == 0)
        def _():
            @pl.when(k == 0)
            def _():
                accs_ref[...] = jnp.zeros_like(accs_ref)

            accs_ref[...] += jnp.dot(as_ref[...], w_ref[...], preferred_element_type=F32)

            @pl.when(k == nk - 1)
            def _():
                os_ref[...] = finish(accs_ref[...]).astype(os_ref.dtype)


def _matmul(a, a_s, w, out_dtype=F32, relu2=False, tm=1024, tn=1024, tk=4096):
    m, kdim = a.shape
    n = w.shape[1]
    rs = a_s.shape[0]
    tn = min(tn, n)
    tk = min(tk, kdim)
    nk = kdim // tk
    scratch = [] if nk == 1 else [pltpu.VMEM((tm, tn), F32), pltpu.VMEM((rs, tn), F32)]
    return pl.pallas_call(
        functools.partial(_matmul_kernel, relu2, nk),
        out_shape=[jax.ShapeDtypeStruct((m, n), out_dtype), jax.ShapeDtypeStruct((rs, n), out_dtype)],
        grid=(n // tn, m // tm, nk),
        in_specs=[pl.BlockSpec((tm, tk), lambda j, i, k: (i, k)),
                  pl.BlockSpec((rs, tk), lambda j, i, k: (0, k)),
                  pl.BlockSpec((tk, tn), lambda j, i, k: (k, j))],
        out_specs=[pl.BlockSpec((tm, tn), lambda j, i, k: (i, j)),
                   pl.BlockSpec((rs, tn), lambda j, i, k: (0, j))],
        scratch_shapes=scratch,
        compiler_params=_cparams("arbitrary", "arbitrary", "arbitrary"),
        name="matmul_relu2" if relu2 else "matmul",
    )(a, a_s, w)


def _panel_matmul_kernel(relu2, n_panels, shift, a_ref, as_ref, *rest):
    if shift:
        wa_ref, wn_ref, o_ref, os_ref, wb_ref = rest
    else:
        wa_ref, o_ref, os_ref, wb_ref = rest
    jj = pl.program_id(0)
    i = pl.program_id(1)
    rows = wa_ref.shape[0]

    def finish(v):
        return jnp.square(jnp.maximum(v, 0.0)) if relu2 else v

    @pl.when(jj < n_panels)
    def _():
        w = wa_ref[...]
        if shift:
            w = jnp.concatenate([w[:, shift:], wn_ref[:, :shift]], axis=1)
        wb_ref[jj % 2, pl.ds(pl.multiple_of(i * rows, rows), rows), :] = w.astype(BF16)

    @pl.when(jj > 0)
    def _():
        w = wb_ref[(jj + 1) % 2]
        o_ref[...] = finish(jnp.dot(a_ref[...], w, preferred_element_type=F32)).astype(o_ref.dtype)

        @pl.when(i == 0)
        def _():
            os_ref[...] = finish(jnp.dot(as_ref[...], w, preferred_element_type=F32)).astype(os_ref.dtype)


def _panel_matmul(a, a_s, w3, layer, col0, n, out_dtype=F32, relu2=False, tm=1024, tn=1024):
    m, kdim = a.shape
    rs = a_s.shape[0]
    n_i = m // tm
    n_panels = n // tn
    assert n % tn == 0 and m % tm == 0 and kdim % n_i == 0
    rows = kdim // n_i
    cb0, shift = divmod(col0, tn)

    def w_spec(extra):
        def index(jj, i):
            last = jj == n_panels
            return (layer, jnp.where(last, n_i - 1, i), cb0 + extra + jnp.minimum(jj, n_panels - 1))
        return pl.BlockSpec((None, rows, tn), index)

    def row_block(jj, i):
        return jnp.where(jj == 0, 0, i)

    w_specs = [w_spec(0), w_spec(1)] if shift else [w_spec(0)]
    return pl.pallas_call(
        functools.partial(_panel_matmul_kernel, relu2, n_panels, shift),
        out_shape=[jax.ShapeDtypeStruct((m, n), out_dtype), jax.ShapeDtypeStruct((rs, n), out_dtype)],
        grid=(n_panels + 1, n_i),
        in_specs=[pl.BlockSpec((tm, kdim), lambda jj, i: (row_block(jj, i), 0)),
                  pl.BlockSpec((rs, kdim), lambda jj, i: (0, 0))] + w_specs,
        out_specs=[pl.BlockSpec((tm, tn), lambda jj, i: (row_block(jj, i), jnp.maximum(jj - 1, 0))),
                   pl.BlockSpec((rs, tn), lambda jj, i: (0, jnp.maximum(jj - 1, 0)))],
        scratch_shapes=[pltpu.VMEM((2, kdim, tn), BF16)],
        compiler_params=_cparams("arbitrary", "arbitrary"),
        name="panel_matmul_relu2" if relu2 else "panel_matmul",
    )(a, a_s, *([w3] * len(w_specs)))


def _gates_kernel(n_heads, zg_ref, alog_ref, dtb_ref, gc_ref, beta_ref):
    zg = zg_ref[...]
    rows = zg.shape[0]
    g = -jnp.exp(alog_ref[...]) * _softplus(zg[:, :n_heads] + dtb_ref[...])
    beta = _sigmoid(zg[:, n_heads:])
    pos = lax.broadcasted_iota(jnp.int32, (rows, HEAD_DIM), 0) % DELTA_CHUNK
    for h in range(n_heads):
        gh = jnp.broadcast_to(g[:, h:h + 1], (rows, HEAD_DIM))
        s = 1
        while s < DELTA_CHUNK:
            gh = gh + jnp.where(pos >= s, pltpu.roll(gh, s, 0), 0.0)
            s *= 2
        gc_ref[h] = gh
        beta_ref[h] = jnp.broadcast_to(beta[:, h:h + 1], (rows, HEAD_DIM))


def _gates(zg, a_log, dt_bias, rows=256):
    b, t, h2 = zg.shape
    nh = h2 // 2
    out = jax.ShapeDtypeStruct((b, nh, t, HEAD_DIM), F32)
    ospec = pl.BlockSpec((None, nh, rows, HEAD_DIM), lambda a, i: (a, 0, i, 0))
    vec = pl.BlockSpec((1, nh), lambda a, i: (0, 0))
    return pl.pallas_call(
        functools.partial(_gates_kernel, nh),
        out_shape=[out, out],
        grid=(b, t // rows),
        in_specs=[pl.BlockSpec((None, rows, h2), lambda a, i: (a, i, 0)), vec, vec],
        out_specs=[ospec, ospec],
        compiler_params=_cparams("arbitrary", "arbitrary"),
        name="delta_gates",
    )(zg, a_log.reshape(1, nh), dt_bias.reshape(1, nh))


def _delta_prompt_kernel(nh, q_ref, k_ref, v_ref, za_ref, cq_ref, ck_ref, cv_ref, gc_ref, beta_ref, norm_ref,
                         o_ref, s_ref, wq_ref, ka_ref, u_ref, eg_ref):
    t = q_ref.shape[0]
    sb = DELTA_SUPER
    c = DELTA_CHUNK
    n_sb = t // sb
    cps = sb // c
    half = sb // 2
    ii = lax.broadcasted_iota(jnp.int32, (sb, sb), 0)
    jj = lax.broadcasted_iota(jnp.int32, (sb, sb), 1)
    lower = ((ii // c) == (jj // c)) & (ii >= jj)
    strict = ii > jj
    row = lax.broadcasted_iota(jnp.int32, (sb, HEAD_DIM), 0)
    pair_chunk = lax.broadcasted_iota(jnp.int32, (HEAD_DIM, 2 * c), 1) // c
    levels = int(math.log2(c))

    def l2norm(x):
        return x * lax.rsqrt(jnp.sum(x * x, axis=-1, keepdims=True) + 1e-6)

    def prepare(n, first, hh):
        r0 = 0 if first else pl.multiple_of(n * sb, sb)
        rows = pl.ds(r0, sb)
        lanes = slice(hh * HEAD_DIM, (hh + 1) * HEAD_DIM)

        def conv_silu(x_ref, w_ref):
            x = x_ref[rows, lanes]
            w = w_ref[:, lanes]
            if first:
                prev = jnp.zeros((8, HEAD_DIM), F32)
            else:
                prev = x_ref[pl.ds(pl.multiple_of(r0 - 8, 8), 8), lanes]
            y = x * w[CONV_W - 1:CONV_W]
            for s in range(1, CONV_W):
                head = jnp.broadcast_to(pltpu.roll(prev, s, 0)[None], (sb // 8, 8, HEAD_DIM))
                xs = jnp.where(row < s, head.reshape(sb, HEAD_DIM), pltpu.roll(x, s, 0))
                y = y + xs * w[CONV_W - 1 - s:CONV_W - s]
            return _silu(y)

        q = l2norm(conv_silu(q_ref, cq_ref)) * (HEAD_DIM ** -0.5)
        k = l2norm(conv_silu(k_ref, ck_ref))
        v = conv_silu(v_ref, cv_ref)
        gc = gc_ref[hh, rows]
        beta = beta_ref[hh, rows]
        kb = k * beta
        eg = jnp.exp(gc)
        gct = jnp.concatenate([gc[:half].T, gc[half:].T], axis=1)
        gc_j = jnp.broadcast_to(gct[0:1], (sb, sb))
        gc_i = jnp.concatenate([gc] * (sb // HEAD_DIM), axis=1)
        dec = jnp.exp(jnp.where(lower, gc_i - gc_j, -jnp.inf))
        kq = _dot_nt(jnp.concatenate([kb, q], axis=0), k)
        yield
        x = -jnp.where(strict, kq[:sb] * dec, 0.0)
        attn = kq[sb:] * dec
        p = jnp.where(ii == jj, 1.0, x)
        xb = x.astype(BF16)
        x = jnp.dot(xb, xb, preferred_element_type=F32)
        yield
        for lvl in range(1, levels):
            xb = x.astype(BF16)
            if lvl < levels - 1:
                y = jnp.dot(jnp.concatenate([p.astype(BF16), xb], axis=0), xb, preferred_element_type=F32)
                yield
                p = p + y[:sb]
                x = y[sb:]
            else:
                y = jnp.dot(p.astype(BF16), xb, preferred_element_type=F32)
                yield
                p = p + y
        uw = _dot(p, jnp.concatenate([v * beta, kb * eg], axis=1))
        yield
        u_ref[hh, rows] = uw[:, :HEAD_DIM]
        w = uw[:, HEAD_DIM:]
        qd = q * eg
        gl = jnp.concatenate([jnp.broadcast_to(gc[ci * c + c - 1:ci * c + c], (c, HEAD_DIM))
                              for ci in range(cps)], axis=0)
        kd = k * jnp.exp(gl - gc)
        kdt = jnp.concatenate([kd[:half].T, kd[half:].T], axis=1)
        for ci in range(cps):
            cc = n * cps + ci
            cr = slice(ci * c, (ci + 1) * c)
            pair = slice((ci // 2) * 2 * c, (ci // 2 + 1) * 2 * c)
            wq_ref[hh, cc] = jnp.concatenate([w[cr], qd[cr]], axis=0).astype(BF16)
            ka_ref[hh, cc] = jnp.concatenate([jnp.where(pair_chunk == ci % 2, kdt[:, pair], 0.0),
                                              attn[cr, pair]], axis=0).astype(BF16)
            eg_ref[hh, cc] = jnp.exp(gl[ci * c:ci * c + 8])

    def recur(n, states, static):
        for ci in range(cps):
            cc = n * cps + ci
            crow = pl.ds(cc * c if static else pl.multiple_of(cc * c, c), c)
            r1 = [jnp.dot(wq_ref[hh, cc], states[hh].astype(BF16), preferred_element_type=F32) for hh in range(nh)]
            yield
            r2 = []
            for hh in range(nh):
                v_new = (u_ref[hh, crow] - r1[hh][:c]).astype(BF16)
                r2.append(jnp.dot(ka_ref[hh, cc], jnp.concatenate([v_new, v_new], axis=0),
                                  preferred_element_type=F32))
            yield
            for hh in range(nh):
                lanes = slice(hh * HEAD_DIM, (hh + 1) * HEAD_DIM)
                o = r1[hh][c:] + r2[hh][HEAD_DIM:]
                states[hh] = states[hh] * eg_ref[hh, cc][0:1] + r2[hh][:HEAD_DIM]
                o_ref[crow, lanes] = (_rms_rows(o) * norm_ref[...] * _silu(za_ref[crow, lanes])).astype(o_ref.dtype)

    def interleave(gens):
        live = list(gens)
        while live:
            live = [g for g in live if next(g, live) is not live]

    def body(n, states):
        states = list(states)
        interleave([prepare(n, False, hh) for hh in range(nh)] + [recur(n - 1, states, False)])
        return tuple(states)

    interleave([prepare(0, True, hh) for hh in range(nh)])
    states = lax.fori_loop(1, n_sb, body, tuple(jnp.zeros((HEAD_DIM, HEAD_DIM), F32) for _ in range(nh)))
    states = list(states)
    interleave([recur(n_sb - 1, states, True)])
    for hh in range(nh):
        s_ref[hh] = states[hh]


def _delta_prompt(z_main, conv_w, gc, beta, norm_w, n_heads, mix_width, hpb=2):
    b, t, _ = z_main.shape
    wl = hpb * HEAD_DIM
    ng = n_heads // hpb
    n_chunks = t // DELTA_CHUNK
    col = lambda off: pl.BlockSpec((None, t, wl), lambda a, h: (a, 0, off + h))
    cw = lambda off: pl.BlockSpec((CONV_W, wl), lambda a, h: (0, off + h))
    gate = pl.BlockSpec((None, hpb, t, HEAD_DIM), lambda a, h: (a, h, 0, 0))
    return pl.pallas_call(
        functools.partial(_delta_prompt_kernel, hpb),
        out_shape=[jax.ShapeDtypeStruct((b, t, mix_width), BF16),
                   jax.ShapeDtypeStruct((b, n_heads, HEAD_DIM, HEAD_DIM), F32)],
        grid=(b, ng),
        in_specs=[col(0), col(ng), col(2 * ng), col(3 * ng),
                  cw(0), cw(ng), cw(2 * ng), gate, gate,
                  pl.BlockSpec((1, HEAD_DIM), lambda a, h: (0, 0))],
        out_specs=[pl.BlockSpec((None, t, wl), lambda a, h: (a, 0, h)),
                   pl.BlockSpec((None, hpb, HEAD_DIM, HEAD_DIM), lambda a, h: (a, h, 0, 0))],
        scratch_shapes=[pltpu.VMEM((hpb, n_chunks, 2 * DELTA_CHUNK, HEAD_DIM), BF16),
                        pltpu.VMEM((hpb, n_chunks, HEAD_DIM + DELTA_CHUNK, 2 * DELTA_CHUNK), BF16),
                        pltpu.VMEM((hpb, t, HEAD_DIM), F32),
                        pltpu.VMEM((hpb, n_chunks, 8, HEAD_DIM), F32)],
        compiler_params=_cparams("arbitrary", "arbitrary"),
        name="delta_prompt",
    )(z_main, z_main, z_main, z_main, conv_w, conv_w, conv_w, gc, beta, norm_w.reshape(1, HEAD_DIM))


def _xpos(x, cos, sin_e, sin_o):
    return x * cos + pltpu.roll(x, HEAD_DIM - 1, 1) * sin_e + pltpu.roll(x, 1, 1) * sin_o


def _log_gamma(h):
    return jnp.log1p(-jnp.exp2(-5.0 - h))


def _ret_prompt_kernel(q_ref, k_ref, v_ref, g_ref, cos_ref, sine_ref, sino_ref, mix_ref, o_ref, s_ref,
                       qs_ref, ks_ref):
    del mix_ref
    t = q_ref.shape[0]
    c = RET_CHUNK
    cos, sin_e, sin_o = cos_ref[...], sine_ref[...], sino_ref[...]
    qs_ref[...] = _xpos(q_ref[...], cos, sin_e, sin_o)
    ks_ref[...] = _xpos(k_ref[...], cos, sin_e, sin_o) * (HEAD_DIM ** -0.5)

    h = jnp.full((1, 1), pl.program_id(1), jnp.int32).astype(F32)
    lg = _log_gamma(h)
    ii = lax.broadcasted_iota(jnp.int32, (c, c), 0)
    jj = lax.broadcasted_iota(jnp.int32, (c, c), 1)
    dec = jnp.exp(jnp.where(ii >= jj, (ii - jj).astype(F32) * lg, -jnp.inf))
    pos = lax.broadcasted_iota(jnp.int32, (c, HEAD_DIM), 0).astype(F32)
    e_in = jnp.exp((pos + 1.0) * lg)
    e_out = jnp.exp((c - 1.0 - pos) * lg)
    e_all = jnp.exp(c * lg)

    unroll = 2

    def body(n, s):
        rows = [pl.ds(pl.multiple_of((n * unroll + r) * c, c), c) for r in range(unroll)]
        q = [qs_ref[rw] for rw in rows]
        k = [ks_ref[rw] for rw in rows]
        v = [v_ref[rw].astype(BF16) for rw in rows]
        qk = [_dot_nt(q[r], k[r]) for r in range(unroll)]
        upd = [_dot((k[r] * e_out).T, v[r]) for r in range(unroll)]
        for r in range(unroll):
            o = _dot(q[r] * e_in, s) + _dot(qk[r] * dec, v[r])
            s = s * e_all + upd[r]
            o_ref[rows[r]] = (_rms_rows(o) * _silu(g_ref[rows[r]])).astype(o_ref.dtype)
        return s

    s_ref[...] = lax.fori_loop(0, t // (c * unroll), body, jnp.zeros((HEAD_DIM, HEAD_DIM), F32))


def _ret_prompt(z_tail, cos, sin_e, sin_o, mix, mix_col0, n_heads):
    b, t, _ = z_tail.shape
    c0 = mix_col0 // HEAD_DIM
    col = lambda off: pl.BlockSpec((None, t, HEAD_DIM), lambda a, h: (a, 0, off + h))
    tab = pl.BlockSpec((t, HEAD_DIM), lambda a, h: (0, 0))
    return pl.pallas_call(
        _ret_prompt_kernel,
        out_shape=[jax.ShapeDtypeStruct(mix.shape, mix.dtype),
                   jax.ShapeDtypeStruct((b, n_heads, HEAD_DIM, HEAD_DIM), F32)],
        grid=(b, n_heads),
        in_specs=[col(0), col(n_heads), col(2 * n_heads), col(3 * n_heads), tab, tab, tab,
                  pl.BlockSpec(memory_space=pl.ANY)],
        out_specs=[pl.BlockSpec((None, t, HEAD_DIM), lambda a, h: (a, 0, c0 + h)),
                   pl.BlockSpec((None, None, HEAD_DIM, HEAD_DIM), lambda a, h: (a, h, 0, 0))],
        scratch_shapes=[pltpu.VMEM((t, HEAD_DIM), F32)] * 2,
        input_output_aliases={7: 0},
        compiler_params=_cparams("arbitrary", "arbitrary"),
        name="ret_prompt",
    )(z_tail, z_tail, z_tail, z_tail, cos, sin_e, sin_o, mix)


def _lambda(lq, lam_init):
    l01 = jnp.sum(lq[0:1] * lq[1:2], axis=-1, keepdims=True)
    l23 = jnp.sum(lq[2:3] * lq[3:4], axis=-1, keepdims=True)
    return jnp.exp(l01) - jnp.exp(l23) + lam_init


def _attn_prompt_kernel(lam_init, nh, tq, q_ref, k_ref, v_ref, lq_ref, sub_ref, mix_ref, o_ref):
    del mix_ref
    i = pl.program_id(2)
    lane = lax.broadcasted_iota(jnp.int32, (tq, HEAD_DIM), 1)
    scale = HALF_DIM ** -0.5
    q2 = []
    for hh in range(nh):
        q = q_ref[:, hh * HEAD_DIM:(hh + 1) * HEAD_DIM]
        q2.append((jnp.concatenate([jnp.where(lane < HALF_DIM, q, 0.0), jnp.where(lane >= HALF_DIM, q, 0.0)],
                                   axis=0) * scale).astype(BF16))
    causal = (lax.broadcasted_iota(jnp.int32, (2 * tq, tq), 1)
              <= lax.broadcasted_iota(jnp.int32, (2 * tq, tq), 0) % tq)

    def step(j, carry, diagonal):
        rows = pl.ds(pl.multiple_of(j * tq, tq), tq)
        out = []
        scores = [_dot_nt(q2[hh], k_ref[rows, hh * HEAD_DIM:(hh + 1) * HEAD_DIM]) for hh in range(nh)]
        for hh in range(nh):
            lanes = slice(hh * HEAD_DIM, (hh + 1) * HEAD_DIM)
            m, l, acc = carry[hh]
            s = scores[hh]
            if diagonal:
                s = jnp.where(causal, s, -jnp.inf)
            m_new = jnp.maximum(m, jnp.max(s, axis=-1, keepdims=True))
            alpha = jnp.exp(m - m_new)
            p = jnp.exp(s - m_new)
            l = alpha * l + jnp.sum(p, axis=-1, keepdims=True)
            acc = alpha * acc + _dot(p, v_ref[rows, lanes])
            out.append((m_new, l, acc))
        return tuple(out)

    init = tuple((jnp.full((2 * tq, 1), -jnp.inf, F32), jnp.zeros((2 * tq, 1), F32),
                  jnp.zeros((2 * tq, HEAD_DIM), F32)) for _ in range(nh))
    carry = lax.fori_loop(0, i, lambda j, cr: step(j, cr, False), init)
    carry = step(i, carry, True)
    lam = _lambda(lq_ref[...], lam_init)
    for hh in range(nh):
        m, l, acc = carry[hh]
        o2 = acc / l
        o = o2[:tq] - lam * o2[tq:]
        o_ref[:, hh * HEAD_DIM:(hh + 1) * HEAD_DIM] = (_rms_rows(o) * sub_ref[...] * (1.0 - lam_init)).astype(o_ref.dtype)


def _attn_prompt(z_tail, lambda_qk, subln, lam_init, mix, mix_col0, n_heads, col0, tq=256, hpb=2):
    b, t, _ = z_tail.shape
    wl = hpb * HEAD_DIM
    ng = n_heads // hpb
    c0 = col0 // hpb
    m0 = mix_col0 // wl
    return pl.pallas_call(
        functools.partial(_attn_prompt_kernel, lam_init, hpb, tq),
        out_shape=jax.ShapeDtypeStruct(mix.shape, mix.dtype),
        grid=(b, ng, t // tq),
        in_specs=[pl.BlockSpec((None, tq, wl), lambda a, h, i: (a, i, c0 + h)),
                  pl.BlockSpec((None, t, wl), lambda a, h, i: (a, 0, c0 + ng + h)),
                  pl.BlockSpec((None, t, wl), lambda a, h, i: (a, 0, c0 + 2 * ng + h)),
                  pl.BlockSpec((4, HALF_DIM), lambda a, h, i: (0, 0)),
                  pl.BlockSpec((1, HEAD_DIM), lambda a, h, i: (0, 0)),
                  pl.BlockSpec(memory_space=pl.ANY)],
        out_specs=pl.BlockSpec((None, tq, wl), lambda a, h, i: (a, i, m0 + h)),
        input_output_aliases={5: 0},
        compiler_params=_cparams("arbitrary", "arbitrary", "arbitrary"),
        name="attn_prompt",
    )(z_tail, z_tail, z_tail, lambda_qk, subln.reshape(1, HEAD_DIM), mix)


def _lane_pick(x, h):
    lane = lax.broadcasted_iota(jnp.int32, x.shape, 1)
    return jnp.sum(jnp.where(lane == h, x, 0.0), axis=-1, keepdims=True)


def _row_to_col(x):
    ii = lax.broadcasted_iota(jnp.int32, (HEAD_DIM, HEAD_DIM), 0)
    jj = lax.broadcasted_iota(jnp.int32, (HEAD_DIM, HEAD_DIM), 1)
    return jnp.sum(jnp.where(ii == jj, jnp.broadcast_to(x, (HEAD_DIM, HEAD_DIM)), 0.0), axis=-1, keepdims=True)


def _rows8(*rows):
    pad = jnp.zeros((8 - len(rows), HEAD_DIM), F32)
    return jnp.concatenate(list(rows) + [pad], axis=0)


def _delta_sample_kernel(q_ref, k_ref, v_ref, za_ref, bq_ref, bk_ref, bv_ref, cq_ref, ck_ref, cv_ref,
                         zg_ref, alog_ref, dtb_ref, norm_ref, s_ref, o_ref, so_ref):
    h = pl.program_id(1)
    n_heads = alog_ref.shape[1]

    def conv_silu(x_ref, buf_ref, w_ref):
        w = w_ref[...]
        buf = buf_ref[...]
        y = x_ref[...] * w[CONV_W - 1:CONV_W]
        for j in range(CONV_W - 1):
            y = y + buf[j:j + 1] * w[j:j + 1]
        return _silu(y)

    def l2norm(x):
        return x * lax.rsqrt(jnp.sum(x * x, axis=-1, keepdims=True) + 1e-6)

    q = l2norm(conv_silu(q_ref, bq_ref, cq_ref)) * (HEAD_DIM ** -0.5)
    k = l2norm(conv_silu(k_ref, bk_ref, ck_ref))
    v = conv_silu(v_ref, bv_ref, cv_ref)
    zg = zg_ref[...]
    a_a = _lane_pick(zg, h)
    b_a = _lane_pick(zg, h + n_heads)
    g = -jnp.exp(_lane_pick(alog_ref[...], h)) * _softplus(a_a + _lane_pick(dtb_ref[...], h))
    beta = _sigmoid(b_a)
    eg = jnp.exp(g)
    s = s_ref[...]
    ks_qs = _dot(_rows8(k, q), s)
    v_new = beta * (v - eg * ks_qs[0:1])
    qk = jnp.sum(q * k, axis=-1, keepdims=True)
    o = eg * ks_qs[1:2] + qk * v_new
    so_ref[...] = s * eg + _row_to_col(k) * v_new
    o_ref[...] = (_rms_rows(o) * norm_ref[...] * _silu(za_ref[...])).astype(o_ref.dtype)


def _delta_sample(zs_main, zs_gate, conv_buf, conv_w, a_log, dt_bias, norm_w, state, layer, n_heads):
    nb = state.shape[0]
    col = lambda off: pl.BlockSpec((None, 1, HEAD_DIM), lambda a, h: (a, 0, off + h))
    buf = lambda off: pl.BlockSpec((None, None, CONV_W - 1, HEAD_DIM), lambda a, h: (a, layer, 0, off + h))
    cw = lambda off: pl.BlockSpec((CONV_W, HEAD_DIM), lambda a, h: (0, off + h))
    vec = pl.BlockSpec((1, n_heads), lambda a, h: (0, 0))
    return pl.pallas_call(
        _delta_sample_kernel,
        out_shape=[jax.ShapeDtypeStruct((nb, 1, n_heads * HEAD_DIM), BF16),
                   jax.ShapeDtypeStruct((nb, n_heads, HEAD_DIM, HEAD_DIM), F32)],
        grid=(nb, n_heads),
        in_specs=[col(0), col(n_heads), col(2 * n_heads), col(3 * n_heads),
                  buf(0), buf(n_heads), buf(2 * n_heads), cw(0), cw(n_heads), cw(2 * n_heads),
                  pl.BlockSpec((None, 1, 2 * n_heads), lambda a, h: (a, 0, 0)), vec, vec,
                  pl.BlockSpec((1, HEAD_DIM), lambda a, h: (0, 0)),
                  pl.BlockSpec((None, None, None, HEAD_DIM, HEAD_DIM), lambda a, h: (a, layer, h, 0, 0))],
        out_specs=[pl.BlockSpec((None, 1, HEAD_DIM), lambda a, h: (a, 0, h)),
                   pl.BlockSpec((None, None, HEAD_DIM, HEAD_DIM), lambda a, h: (a, h, 0, 0))],
        compiler_params=_cparams("arbitrary", "arbitrary"),
        name="delta_sample",
    )(zs_main, zs_main, zs_main, zs_main, conv_buf, conv_buf, conv_buf, conv_w, conv_w, conv_w,
      zs_gate, a_log.reshape(1, n_heads), dt_bias.reshape(1, n_heads), norm_w.reshape(1, HEAD_DIM), state)


def _ret_sample_kernel(q_ref, k_ref, v_ref, g_ref, cos_ref, sine_ref, sino_ref, s_ref, o_ref, so_ref):
    h = jnp.full((1, 1), pl.program_id(1), jnp.int32).astype(F32)
    gamma = jnp.exp(_log_gamma(h))
    cos, sin_e, sin_o = cos_ref[...], sine_ref[...], sino_ref[...]
    q = _xpos(q_ref[...], cos, sin_e, sin_o)
    k = _xpos(k_ref[...], cos, sin_e, sin_o) * (HEAD_DIM ** -0.5)
    v = v_ref[...]
    s = s_ref[...]
    qs = _dot(_rows8(q), s)
    qk = jnp.sum(q * k, axis=-1, keepdims=True)
    o = qk * v + gamma * qs[0:1]
    so_ref[...] = s * gamma + _row_to_col(k) * v
    o_ref[...] = (_rms_rows(o) * _silu(g_ref[...])).astype(o_ref.dtype)


def _ret_sample(zs_tail, cos, sin_e, sin_o, state, layer, n_heads):
    nb = state.shape[0]
    col = lambda off: pl.BlockSpec((None, 1, HEAD_DIM), lambda a, h: (a, 0, off + h))
    tab = pl.BlockSpec((1, HEAD_DIM), lambda a, h: (0, 0))
    return pl.pallas_call(
        _ret_sample_kernel,
        out_shape=[jax.ShapeDtypeStruct((nb, 1, n_heads * HEAD_DIM), BF16),
                   jax.ShapeDtypeStruct((nb, n_heads, HEAD_DIM, HEAD_DIM), F32)],
        grid=(nb, n_heads),
        in_specs=[col(0), col(n_heads), col(2 * n_heads), col(3 * n_heads), tab, tab, tab,
                  pl.BlockSpec((None, None, None, HEAD_DIM, HEAD_DIM), lambda a, h: (a, layer, h, 0, 0))],
        out_specs=[pl.BlockSpec((None, 1, HEAD_DIM), lambda a, h: (a, 0, h)),
                   pl.BlockSpec((None, None, HEAD_DIM, HEAD_DIM), lambda a, h: (a, h, 0, 0))],
        compiler_params=_cparams("arbitrary", "arbitrary"),
        name="ret_sample",
    )(zs_tail, zs_tail, zs_tail, zs_tail, cos, sin_e, sin_o, state)


def _attn_decode_kernel(lam_init, ppb, pt_ref, q_ref, kn_ref, vn_ref, *rest):
    kc_refs, vc_refs = rest[:ppb], rest[ppb:2 * ppb]
    lq_ref, sub_ref, o_ref, m_ref, l_ref, acc_ref = rest[2 * ppb:]
    j = pl.program_id(1)
    n_steps = pl.num_programs(1)
    nh = q_ref.shape[0]
    scale = HALF_DIM ** -0.5
    di = lax.broadcasted_iota(jnp.int32, (HEAD_DIM, 2 * HEAD_DIM), 0)
    ci = lax.broadcasted_iota(jnp.int32, (HEAD_DIM, 2 * HEAD_DIM), 1)
    expand = ((di // HALF_DIM) == (ci // HEAD_DIM)).astype(BF16)
    q = q_ref[...] * scale

    @pl.when(j == 0)
    def _():
        m_ref[...] = jnp.full(m_ref.shape, -jnp.inf, F32)
        l_ref[...] = jnp.zeros_like(l_ref)
        acc_ref[...] = jnp.zeros_like(acc_ref)

    def scores(kt):
        n_tok = kt.shape[0]
        prod = (kt * q[None]).reshape(n_tok * nh, HEAD_DIM)
        s = jnp.dot(prod.astype(BF16), expand, preferred_element_type=F32)
        return s.reshape(n_tok, nh, 2 * HEAD_DIM)

    def update(s, vt):
        m_old = m_ref[...]
        m_new = jnp.maximum(m_old, jnp.max(s, axis=0))
        alpha = jnp.exp(m_old - m_new)
        p = jnp.exp(s - m_new[None])
        l_ref[...] = alpha * l_ref[...] + jnp.sum(p, axis=0)
        pv = jnp.concatenate([jnp.sum(p[:, :, :HEAD_DIM] * vt, axis=0),
                              jnp.sum(p[:, :, HEAD_DIM:] * vt, axis=0)], axis=-1)
        acc_ref[...] = alpha * acc_ref[...] + pv
        m_ref[...] = m_new

    page_scores = [scores(kc_ref[...]) for kc_ref in kc_refs]
    for s, vc_ref in zip(page_scores, vc_refs):
        update(s, vc_ref[...])

    @pl.when(j == n_steps - 1)
    def _():
        update(scores(kn_ref[...][None]), vn_ref[...][None])
        o2 = acc_ref[...] / l_ref[...]
        lam = _lambda(lq_ref[...], lam_init)
        o = o2[:, :HEAD_DIM] - lam * o2[:, HEAD_DIM:]
        o_ref[...] = (_rms_rows(o) * sub_ref[...] * (1.0 - lam_init)).astype(o_ref.dtype)


def _attn_decode(q, k_new, v_new, cache_k, cache_v, page_table, lambda_qk, subln, lam_init, layer, ppb=4):
    nb, nh, _ = q.shape
    n_pages = page_table.shape[1]
    assert n_pages % ppb == 0
    tok = pl.BlockSpec((None, nh, HEAD_DIM), lambda b, j, pt: (b, 0, 0))

    def page(r):
        return pl.BlockSpec((None, None, PAGE_SIZE, nh, HEAD_DIM),
                            lambda b, j, pt: (pt[b, j * ppb + r], layer, 0, 0, 0))

    pages = [page(r) for r in range(ppb)]
    grid_spec = pltpu.PrefetchScalarGridSpec(
        num_scalar_prefetch=1,
        grid=(nb, n_pages // ppb),
        in_specs=[tok, tok, tok] + pages + pages +
                 [pl.BlockSpec((4, HALF_DIM), lambda b, j, pt: (0, 0)),
                  pl.BlockSpec((1, HEAD_DIM), lambda b, j, pt: (0, 0))],
        out_specs=tok,
        scratch_shapes=[pltpu.VMEM((nh, 2 * HEAD_DIM), F32)] * 3,
    )
    return pl.pallas_call(
        functools.partial(_attn_decode_kernel, lam_init, ppb),
        out_shape=jax.ShapeDtypeStruct((nb, nh, HEAD_DIM), BF16),
        grid_spec=grid_spec,
        compiler_params=_cparams("arbitrary", "arbitrary"),
        name="attn_decode",
    )(page_table, q, k_new, v_new, *([cache_k] * ppb), *([cache_v] * ppb), lambda_qk, subln.reshape(1, HEAD_DIM))


def _xpos_tables(pos):
    half = HEAD_DIM // 2
    angle = jnp.repeat(1.0 / (10000.0 ** jnp.linspace(0.0, 1.0, half, dtype=F32)), 2)
    ph = pos.astype(F32)[:, None] * angle
    sin, cos = jnp.sin(ph), jnp.cos(ph)
    even = (jnp.arange(HEAD_DIM) % 2) == 0
    return cos, jnp.where(even, -sin, 0.0), jnp.where(even, 0.0, sin)


def kernel(x_prompt, x_sample, state_conv_a, state_delta, state_ret, cache_k, cache_v, page_table, c_prompt, c_sample, w_ada, b_ada, w_in, conv_a, a_log, dt_bias, norm_a, lambda_qk, subln_c, w_out, ln1_g, ln1_b, w_up, w_down, ln2_g, ln2_b):
    bp, t, d = x_prompt.shape
    nb = x_sample.shape[0]
    depth = w_in.shape[0]
    h_a = state_delta.shape[2]
    h_b = state_ret.shape[2]
    h_c = cache_k.shape[3]
    a_w, b_w, c_w = h_a * HEAD_DIM, h_b * HEAD_DIM, h_c * HEAD_DIM
    n_main = 4 * a_w
    n_gate = 2 * h_a
    past_len = page_table.shape[1] * PAGE_SIZE
    alpha = (2 * depth) ** 0.25
    rs = SAMPLE_ROWS
    assert nb + bp <= rs and x_sample.shape[1] == 1

    c_all = jnp.concatenate([c_sample, c_prompt, jnp.zeros((rs - nb - bp, d), F32)], axis=0)
    mod = _ada(c_all, w_ada, b_ada)

    def mod_p(l, i):
        return mod[l, nb:nb + bp, None, i * d:(i + 1) * d]

    def mod_s(l, i):
        return mod[l][None, :, i * d:(i + 1) * d]

    cos_p, sine_p, sino_p = _xpos_tables(jnp.arange(t))
    cos_s, sine_s, sino_s = _xpos_tables(past_len + jnp.arange(1))

    xp = x_prompt
    xs = jnp.concatenate([x_sample.reshape(nb, d), jnp.zeros((rs - nb, d), F32)], axis=0)[None]
    hp = _modulate(xp, mod_p(0, 1), mod_p(0, 0), 256)
    hs = _modulate(xs, mod_s(0, 1), mod_s(0, 0), rs)

    st_p, st_s = [], []
    for l in range(depth):
        lam_init = 0.8 - 0.6 * math.exp(-0.3 * l)
        w_gate = w_in[l, :, n_main:n_main + n_gate].astype(BF16)
        n_tail = w_in.shape[2] - n_main - n_gate
        hp2 = hp.reshape(bp * t, d)
        hs2 = hs.reshape(rs, d)
        zp_main, zs_main = _panel_matmul(hp2, hs2, w_in, l, 0, n_main)
        zp_gate, zs_gate = _matmul(hp2, hs2, w_gate)
        zp_tail, zs_tail = _panel_matmul(hp2, hs2, w_in, l, n_main + n_gate, n_tail, tn=512)
        zp_main = zp_main.reshape(bp, t, n_main)
        zp_tail = zp_tail.reshape(bp, t, -1)

        gc, beta = _gates(zp_gate.reshape(bp, t, n_gate), a_log[l], dt_bias[l])
        mix_p, delta_p = _delta_prompt(zp_main, conv_a[l], gc, beta, norm_a[l], h_a, a_w + b_w + c_w)
        mix_p, ret_p = _ret_prompt(zp_tail, cos_p, sine_p, sino_p, mix_p, a_w, h_b)
        mix_p = _attn_prompt(zp_tail, lambda_qk[l], subln_c[l], lam_init, mix_p, a_w + b_w, h_c, 4 * h_b)
        k_off = 4 * b_w + c_w
        conv_p = zp_main[:, t - (CONV_W - 1):, :3 * a_w]
        k_p = zp_tail[:, :, k_off:k_off + c_w].reshape(bp, t, h_c, HEAD_DIM)
        v_p = zp_tail[:, :, k_off + c_w:k_off + 2 * c_w].reshape(bp, t, h_c, HEAD_DIM)
        st_p.append((conv_p, delta_p, ret_p, k_p, v_p))

        zs_m = zs_main[:nb]
        zs_t = zs_tail[:nb]
        oa_s, delta_s = _delta_sample(zs_m[:, None], zs_gate[:nb, None], state_conv_a, conv_a[l], a_log[l],
                                      dt_bias[l], norm_a[l], state_delta, l, h_a)
        ob_s, ret_s = _ret_sample(zs_t[:, None], cos_s, sine_s, sino_s, state_ret, l, h_b)
        q_s = zs_t[:, 4 * b_w:4 * b_w + c_w].reshape(nb, h_c, HEAD_DIM)
        k_s = zs_t[:, k_off:k_off + c_w].reshape(nb, h_c, HEAD_DIM)
        v_s = zs_t[:, k_off + c_w:k_off + 2 * c_w].reshape(nb, h_c, HEAD_DIM)
        oc_s = _attn_decode(q_s, k_s, v_s, cache_k, cache_v, page_table, lambda_qk[l], subln_c[l], lam_init, l)
        conv_s = jnp.concatenate([state_conv_a[:, l, 1:], zs_m[:, None, :3 * a_w]], axis=1)
        st_s.append((conv_s, delta_s, ret_s, k_s[:, None], v_s[:, None]))

        mix_p = mix_p.reshape(bp * t, -1)
        mix_s = jnp.concatenate([oa_s.reshape(nb, a_w), ob_s.reshape(nb, b_w), oc_s.reshape(nb, c_w)], axis=-1)
        mix_s = jnp.concatenate([mix_s, jnp.zeros((rs - nb, mix_s.shape[1]), BF16)], axis=0)
        yp, ys = _panel_matmul(mix_p, mix_s, w_out, l, 0, d)
        xp, hp = _postnorm(xp, yp.reshape(bp, t, d), mod_p(l, 2), ln1_g[l], ln1_b[l], mod_p(l, 4), mod_p(l, 3),
                           alpha, 256)
        xs, hs = _postnorm(xs, ys[None], mod_s(l, 2), ln1_g[l], ln1_b[l], mod_s(l, 4), mod_s(l, 3), alpha, rs)

        up, us = _panel_matmul(hp.reshape(bp * t, d), hs.reshape(rs, d), w_up, l, 0, w_up.shape[2],
                               out_dtype=BF16, relu2=True)
        fp, fs = _matmul(up, us, w_down[l].astype(BF16))
        last = l == depth - 1
        nl = min(l + 1, depth - 1)
        xp, hp = _postnorm(xp, fp.reshape(bp, t, d), mod_p(l, 5), ln2_g[l], ln2_b[l], mod_p(nl, 1), mod_p(nl, 0),
                           alpha, 256, with_h=not last)
        xs, hs = _postnorm(xs, fs[None], mod_s(l, 5), ln2_g[l], ln2_b[l], mod_s(nl, 1), mod_s(nl, 0), alpha, rs,
                           with_h=not last)

    outs_p = tuple(jnp.stack([s[i] for s in st_p], axis=1) for i in range(5))
    outs_s = tuple(jnp.stack([s[i] for s in st_s], axis=1) for i in range(5))
    y_sample = xs[0, :nb].reshape(nb, 1, d)
    return (xp, y_sample) + outs_p + outs_s
```

```python
import functools
import math

import jax
import jax.numpy as jnp
from jax import lax
from jax.experimental import pallas as pl
from jax.experimental.pallas import tpu as pltpu

F32 = jnp.float32
BF16 = jnp.bfloat16

HEAD_DIM = 128
HALF_DIM = HEAD_DIM // 2
CONV_W = 4
DELTA_CHUNK = 64
DELTA_SUPER = 256
DELTA_BASE = 4
RET_CHUNK = 128
PAGE_SIZE = 128
EPS = 1e-5
SAMPLE_ROWS = 16
VMEM_LIMIT = 58 * 1024 * 1024


def _cparams(*sem):
    return pltpu.CompilerParams(dimension_semantics=sem, vmem_limit_bytes=VMEM_LIMIT)


def _sigmoid(x):
    return 1.0 / (1.0 + jnp.exp(-x))


def _silu(x):
    return x * _sigmoid(x)


def _softplus(x):
    return jnp.maximum(x, 0.0) + jnp.log1p(jnp.exp(-jnp.abs(x)))


def _dot(a, b):
    return jnp.dot(a.astype(BF16), b.astype(BF16), preferred_element_type=F32)


def _dot_nt(a, b):
    return lax.dot_general(a.astype(BF16), b.astype(BF16), (((1,), (1,)), ((), ())),
                           preferred_element_type=F32)


def _rms_rows(x):
    return x * lax.rsqrt(jnp.mean(x * x, axis=-1, keepdims=True) + EPS)


def _ada_kernel(c_ref, w0_ref, w1_ref, b_ref, o_ref):
    c = _silu(c_ref[...])
    dh = w0_ref.shape[0]
    o_ref[...] = _dot(c[:, :dh], w0_ref[...]) + _dot(c[:, dh:], w1_ref[...]) + b_ref[...]


def _ada(c_all, w_ada, b_ada, tn=1024):
    depth, d, n = w_ada.shape
    rows = c_all.shape[0]
    return pl.pallas_call(
        _ada_kernel,
        out_shape=jax.ShapeDtypeStruct((depth, rows, n), F32),
        grid=(depth, n // tn),
        in_specs=[pl.BlockSpec((rows, d), lambda l, j: (0, 0)),
                  pl.BlockSpec((None, d // 2, tn), lambda l, j: (l, 0, j)),
                  pl.BlockSpec((None, d // 2, tn), lambda l, j: (l, 1, j)),
                  pl.BlockSpec((None, 1, tn), lambda l, j: (l, 0, j))],
        out_specs=pl.BlockSpec((None, rows, tn), lambda l, j: (l, 0, j)),
        compiler_params=_cparams("arbitrary", "arbitrary"),
        name="ada_mod",
    )(c_all, w_ada, w_ada, b_ada.reshape(depth, 1, n))


def _modulate_kernel(x_ref, sc_ref, sh_ref, o_ref):
    o_ref[...] = (x_ref[...] * (1.0 + sc_ref[...]) + sh_ref[...]).astype(o_ref.dtype)


def _modulate(x, sc, sh, tm):
    g, r, d = x.shape
    rm = sc.shape[1]
    mspec = pl.BlockSpec((None, rm, d), lambda a, i: (a, 0, 0))
    return pl.pallas_call(
        _modulate_kernel,
        out_shape=jax.ShapeDtypeStruct((g, r, d), BF16),
        grid=(g, r // tm),
        in_specs=[pl.BlockSpec((None, tm, d), lambda a, i: (a, i, 0)), mspec, mspec],
        out_specs=pl.BlockSpec((None, tm, d), lambda a, i: (a, i, 0)),
        compiler_params=_cparams("arbitrary", "arbitrary"),
        name="modulate",
    )(x, sc, sh)


def _postnorm_kernel(alpha, with_h, x_ref, y_ref, gate_ref, lg_ref, lb_ref, sc_ref, sh_ref, xo_ref, *h_ref):
    v = alpha * x_ref[...] + (1.0 + gate_ref[...]) * y_ref[...]
    vc = v - jnp.mean(v, axis=-1, keepdims=True)
    var = jnp.mean(vc * vc, axis=-1, keepdims=True)
    xn = vc * lax.rsqrt(var + EPS) * lg_ref[...] + lb_ref[...]
    xo_ref[...] = xn
    if with_h:
        h_ref[0][...] = (xn * (1.0 + sc_ref[...]) + sh_ref[...]).astype(BF16)


def _postnorm(x, y, gate, ln_g, ln_b, sc, sh, alpha, tm, with_h=True):
    g, r, d = x.shape
    rm = gate.shape[1]
    row = pl.BlockSpec((None, tm, d), lambda a, i: (a, i, 0))
    mspec = pl.BlockSpec((None, rm, d), lambda a, i: (a, 0, 0))
    vec = pl.BlockSpec((1, d), lambda a, i: (0, 0))
    out_shape = [jax.ShapeDtypeStruct((g, r, d), F32)]
    out_specs = [row]
    if with_h:
        out_shape.append(jax.ShapeDtypeStruct((g, r, d), BF16))
        out_specs.append(row)
    res = pl.pallas_call(
        functools.partial(_postnorm_kernel, alpha, with_h),
        out_shape=out_shape,
        grid=(g, r // tm),
        in_specs=[row, row, mspec, vec, vec, mspec, mspec],
        out_specs=out_specs,
        compiler_params=_cparams("arbitrary", "arbitrary"),
        name="postnorm",
    )(x, y, gate, ln_g.reshape(1, d), ln_b.reshape(1, d), sc, sh)
    return (res[0], res[1]) if with_h else (res[0], None)


def _matmul_kernel(relu2, nk, a_ref, as_ref, w_ref, o_ref, os_ref, *scratch):
    i = pl.program_id(1)
    k = pl.program_id(2)

    def finish(v):
        return jnp.square(jnp.maximum(v, 0.0)) if relu2 else v

    if nk == 1:
        o_ref[...] = finish(jnp.dot(a_ref[...], w_ref[...], preferred_element_type=F32)).astype(o_ref.dtype)

        @pl.when(i == 0)
        def _():
            os_ref[...] = finish(jnp.dot(as_ref[...], w_ref[...], preferred_element_type=F32)).astype(os_ref.dtype)
    else:
        acc_ref, accs_ref = scratch

        @pl.when(k == 0)
        def _():
            acc_ref[...] = jnp.zeros_like(acc_ref)

        acc_ref[...] += jnp.dot(a_ref[...], w_ref[...], preferred_element_type=F32)

        @pl.when(k == nk - 1)
        def _():
            o_ref[...] = finish(acc_ref[...]).astype(o_ref.dtype)

        @pl.when(i == 0)
        def _():
            @pl.when(k == 0)
            def _():
                accs_ref[...] = jnp.zeros_like(accs_ref)

            accs_ref[...] += jnp.dot(as_ref[...], w_ref[...], preferred_element_type=F32)

            @pl.when(k == nk - 1)
            def _():
                os_ref[...] = finish(accs_ref[...]).astype(os_ref.dtype)


def _matmul(a, a_s, w3, layer, col0=0, n=None, out_dtype=F32, relu2=False, tm=1024, tn=1024, tk=4096):
    m, kdim = a.shape
    n = w3.shape[2] if n is None else n
    rs = a_s.shape[0]
    tn = min(tn, n)
    tk = min(tk, kdim)
    nk = kdim // tk
    assert n % tn == 0 and col0 % tn == 0 and m % tm == 0 and kdim % tk == 0
    cb0 = col0 // tn
    scratch = [] if nk == 1 else [pltpu.VMEM((tm, tn), F32), pltpu.VMEM((rs, tn), F32)]
    return pl.pallas_call(
        functools.partial(_matmul_kernel, relu2, nk),
        out_shape=[jax.ShapeDtypeStruct((m, n), out_dtype), jax.ShapeDtypeStruct((rs, n), out_dtype)],
        grid=(n // tn, m // tm, nk),
        in_specs=[pl.BlockSpec((tm, tk), lambda j, i, k: (i, k)),
                  pl.BlockSpec((rs, tk), lambda j, i, k: (0, k)),
                  pl.BlockSpec((None, tk, tn), lambda j, i, k: (layer, k, cb0 + j))],
        out_specs=[pl.BlockSpec((tm, tn), lambda j, i, k: (i, j)),
                   pl.BlockSpec((rs, tn), lambda j, i, k: (0, j))],
        scratch_shapes=scratch,
        compiler_params=_cparams("arbitrary", "arbitrary", "arbitrary"),
        name="matmul_relu2" if relu2 else "matmul",
    )(a, a_s, w3)


def _panel_matmul_kernel(relu2, n_panels, shift, a_ref, as_ref, *rest):
    if shift:
        wa_ref, wn_ref, o_ref, os_ref, wb_ref = rest
    else:
        wa_ref, o_ref, os_ref, wb_ref = rest
    jj = pl.program_id(0)
    i = pl.program_id(1)
    rows = wa_ref.shape[0]

    def finish(v):
        return jnp.square(jnp.maximum(v, 0.0)) if relu2 else v

    @pl.when(jj < n_panels)
    def _():
        w = wa_ref[...]
        if shift:
            w = jnp.concatenate([w[:, shift:], wn_ref[:, :shift]], axis=1)
        wb_ref[jj % 2, pl.ds(pl.multiple_of(i * rows, rows), rows), :] = w.astype(BF16)

    @pl.when(jj > 0)
    def _():
        w = wb_ref[(jj + 1) % 2]
        o_ref[...] = finish(jnp.dot(a_ref[...], w, preferred_element_type=F32)).astype(o_ref.dtype)

        @pl.when(i == 0)
        def _():
            os_ref[...] = finish(jnp.dot(as_ref[...], w, preferred_element_type=F32)).astype(os_ref.dtype)


def _panel_matmul(a, a_s, w3, layer, col0, n, out_dtype=F32, relu2=False, tm=1024, tn=1024):
    m, kdim = a.shape
    rs = a_s.shape[0]
    n_i = m // tm
    n_panels = n // tn
    assert n % tn == 0 and m % tm == 0 and kdim % n_i == 0
    rows = kdim // n_i
    cb0, shift = divmod(col0, tn)

    def w_spec(extra):
        def index(jj, i):
            last = jj == n_panels
            return (layer, jnp.where(last, n_i - 1, i), cb0 + extra + jnp.minimum(jj, n_panels - 1))
        return pl.BlockSpec((None, rows, tn), index)

    def row_block(jj, i):
        return jnp.where(jj == 0, 0, i)

    w_specs = [w_spec(0), w_spec(1)] if shift else [w_spec(0)]
    return pl.pallas_call(
        functools.partial(_panel_matmul_kernel, relu2, n_panels, shift),
        out_shape=[jax.ShapeDtypeStruct((m, n), out_dtype), jax.ShapeDtypeStruct((rs, n), out_dtype)],
        grid=(n_panels + 1, n_i),
        in_specs=[pl.BlockSpec((tm, kdim), lambda jj, i: (row_block(jj, i), 0)),
                  pl.BlockSpec((rs, kdim), lambda jj, i: (0, 0))] + w_specs,
        out_specs=[pl.BlockSpec((tm, tn), lambda jj, i: (row_block(jj, i), jnp.maximum(jj - 1, 0))),
                   pl.BlockSpec((rs, tn), lambda jj, i: (0, jnp.maximum(jj - 1, 0)))],
        scratch_shapes=[pltpu.VMEM((2, kdim, tn), BF16)],
        compiler_params=_cparams("arbitrary", "arbitrary"),
        name="panel_matmul_relu2" if relu2 else "panel_matmul",
    )(a, a_s, *([w3] * len(w_specs)))


def _gates_kernel(n_heads, zg_ref, alog_ref, dtb_ref, gc_ref, beta_ref):
    zg = zg_ref[...]
    rows = zg.shape[0]
    g = -jnp.exp(alog_ref[...]) * _softplus(zg[:, :n_heads] + dtb_ref[...])
    beta = _sigmoid(zg[:, n_heads:])
    pos = lax.broadcasted_iota(jnp.int32, (rows, HEAD_DIM), 0) % DELTA_CHUNK
    for h in range(n_heads):
        gh = jnp.broadcast_to(g[:, h:h + 1], (rows, HEAD_DIM))
        s = 1
        while s < DELTA_CHUNK:
            gh = gh + jnp.where(pos >= s, pltpu.roll(gh, s, 0), 0.0)
            s *= 2
        gc_ref[h] = gh
        beta_ref[h] = jnp.broadcast_to(beta[:, h:h + 1], (rows, HEAD_DIM))


def _gates(zg, a_log, dt_bias, rows=256):
    b, t, h2 = zg.shape
    nh = h2 // 2
    out = jax.ShapeDtypeStruct((b, nh, t, HEAD_DIM), F32)
    ospec = pl.BlockSpec((None, nh, rows, HEAD_DIM), lambda a, i: (a, 0, i, 0))
    vec = pl.BlockSpec((1, nh), lambda a, i: (0, 0))
    return pl.pallas_call(
        functools.partial(_gates_kernel, nh),
        out_shape=[out, out],
        grid=(b, t // rows),
        in_specs=[pl.BlockSpec((None, rows, h2), lambda a, i: (a, i, 0)), vec, vec],
        out_specs=[ospec, ospec],
        compiler_params=_cparams("arbitrary", "arbitrary"),
        name="delta_gates",
    )(zg, a_log.reshape(1, nh), dt_bias.reshape(1, nh))


def _delta_prompt_kernel(nh, q_ref, k_ref, v_ref, za_ref, cq_ref, ck_ref, cv_ref, gc_ref, beta_ref, norm_ref,
                         o_ref, s_ref, wq_ref, ka_ref, u_ref, eg_ref):
    t = q_ref.shape[0]
    sb = DELTA_SUPER
    c = DELTA_CHUNK
    n_sb = t // sb
    cps = sb // c
    half = sb // 2
    ii = lax.broadcasted_iota(jnp.int32, (sb, sb), 0)
    jj = lax.broadcasted_iota(jnp.int32, (sb, sb), 1)
    lower = ((ii // c) == (jj // c)) & (ii >= jj)
    strict = ii > jj
    row = lax.broadcasted_iota(jnp.int32, (sb, HEAD_DIM), 0)
    pair_chunk = lax.broadcasted_iota(jnp.int32, (HEAD_DIM, 2 * c), 1) // c
    blk_xor = ii ^ jj
    base = DELTA_BASE
    base_levels = int(math.log2(base))

    def l2norm(x):
        return x * lax.rsqrt(jnp.sum(x * x, axis=-1, keepdims=True) + 1e-6)

    def prepare(n, hh):
        static = isinstance(n, int)
        r0 = n * sb if static else pl.multiple_of(n * sb, sb)
        rows = pl.ds(r0, sb)
        lanes = slice(hh * HEAD_DIM, (hh + 1) * HEAD_DIM)

        def conv_silu(x_ref, w_ref):
            x = x_ref[rows, lanes]
            w = w_ref[:, lanes]
            if static and n == 0:
                prev = jnp.zeros((8, HEAD_DIM), F32)
            else:
                prev = x_ref[pl.ds(r0 - 8 if static else pl.multiple_of(r0 - 8, 8), 8), lanes]
            y = x * w[CONV_W - 1:CONV_W]
            for s in range(1, CONV_W):
                head = jnp.broadcast_to(pltpu.roll(prev, s, 0)[None], (sb // 8, 8, HEAD_DIM))
                xs = jnp.where(row < s, head.reshape(sb, HEAD_DIM), pltpu.roll(x, s, 0))
                y = y + xs * w[CONV_W - 1 - s:CONV_W - s]
            return _silu(y)

        q = l2norm(conv_silu(q_ref, cq_ref)) * (HEAD_DIM ** -0.5)
        k = l2norm(conv_silu(k_ref, ck_ref))
        v = conv_silu(v_ref, cv_ref)
        gc = gc_ref[hh, rows]
        beta = beta_ref[hh, rows]
        kb = k * beta
        eg = jnp.exp(gc)
        gct = jnp.concatenate([gc[:half].T, gc[half:].T], axis=1)
        gc_j = jnp.broadcast_to(gct[0:1], (sb, sb))
        gc_i = jnp.concatenate([gc] * (sb // HEAD_DIM), axis=1)
        dec = jnp.exp(jnp.where(lower, gc_i - gc_j, -jnp.inf))
        kq = _dot_nt(jnp.concatenate([kb, q], axis=0), k)
        yield
        mm = jnp.where(strict, kq[:sb] * dec, 0.0)
        attn = kq[sb:] * dec
        x = jnp.where(blk_xor < base, -mm, 0.0)
        p = jnp.where(ii == jj, 1.0, x)
        xb = x.astype(BF16)
        x = jnp.dot(xb, xb, preferred_element_type=F32)
        yield
        for lvl in range(1, base_levels):
            xb = x.astype(BF16)
            if lvl < base_levels - 1:
                y = jnp.dot(jnp.concatenate([p.astype(BF16), xb], axis=0), xb, preferred_element_type=F32)
                yield
                p = p + y[:sb]
                x = y[sb:]
            else:
                y = jnp.dot(p.astype(BF16), xb, preferred_element_type=F32)
                yield
                p = p + y
        size = base
        while size < c:
            pb = p.astype(BF16)
            off = jnp.where((blk_xor >= size) & (blk_xor < 2 * size), mm, 0.0).astype(BF16)
            e = jnp.dot(off, pb, preferred_element_type=F32)
            yield
            y = jnp.dot(pb, e.astype(BF16), preferred_element_type=F32)
            yield
            p = p - y
            size *= 2
        rhs = jnp.concatenate([v * beta, kb * eg], axis=1)
        pb = p.astype(BF16)
        uw = jnp.dot(pb, rhs.astype(BF16), preferred_element_type=F32)
        yield
        mm_hi = mm.astype(BF16)
        mm_lo = (mm - mm_hi.astype(F32)).astype(BF16)
        uw_hi = uw.astype(BF16)
        uw_lo = (uw - uw_hi.astype(F32)).astype(BF16)
        resid = rhs - uw - (jnp.dot(mm_hi, uw_hi, preferred_element_type=F32)
                            + jnp.dot(mm_hi, uw_lo, preferred_element_type=F32)
                            + jnp.dot(mm_lo, uw_hi, preferred_element_type=F32))
        yield
        uw = uw + jnp.dot(pb, resid.astype(BF16), preferred_element_type=F32)
        yield
        u_ref[hh, rows] = uw[:, :HEAD_DIM]
        w = uw[:, HEAD_DIM:]
        qd = q * eg
        gl = jnp.concatenate([jnp.broadcast_to(gc[ci * c + c - 1:ci * c + c], (c, HEAD_DIM))
                              for ci in range(cps)], axis=0)
        kd = k * jnp.exp(gl - gc)
        kdt = jnp.concatenate([kd[:half].T, kd[half:].T], axis=1)
        for ci in range(cps):
            cc = n * cps + ci
            cr = slice(ci * c, (ci + 1) * c)
            pair = slice((ci // 2) * 2 * c, (ci // 2 + 1) * 2 * c)
            wq_ref[hh, cc] = jnp.concatenate([w[cr], qd[cr]], axis=0).astype(BF16)
            ka_ref[hh, cc] = jnp.concatenate([jnp.where(pair_chunk == ci % 2, kdt[:, pair], 0.0),
                                              attn[cr, pair]], axis=0).astype(BF16)
            eg_ref[hh, cc] = jnp.exp(gl[ci * c:ci * c + 8])

    def recur(n, states):
        static = isinstance(n, int)
        for ci in range(cps):
            cc = n * cps + ci
            crow = pl.ds(cc * c if static else pl.multiple_of(cc * c, c), c)
            r1 = [jnp.dot(wq_ref[hh, cc], states[hh].astype(BF16), preferred_element_type=F32) for hh in range(nh)]
            yield
            r2 = []
            for hh in range(nh):
                v_new = (u_ref[hh, crow] - r1[hh][:c]).astype(BF16)
                r2.append(jnp.dot(ka_ref[hh, cc], jnp.concatenate([v_new, v_new], axis=0),
                                  preferred_element_type=F32))
            yield
            for hh in range(nh):
                lanes = slice(hh * HEAD_DIM, (hh + 1) * HEAD_DIM)
                o = r1[hh][c:] + r2[hh][HEAD_DIM:]
                states[hh] = states[hh] * eg_ref[hh, cc][0:1] + r2[hh][:HEAD_DIM]
                o_ref[crow, lanes] = (_rms_rows(o) * norm_ref[...] * _silu(za_ref[crow, lanes])).astype(o_ref.dtype)

    def interleave(gens):
        live = list(gens)
        while live:
            live = [g for g in live if next(g, live) is not live]

    def chain(*gens):
        for g in gens:
            yield from g

    bpi = 2
    assert n_sb % bpi == 0

    def body(it, states):
        states = list(states)
        first = it * bpi
        interleave([prepare(first + r, hh) for r in range(bpi) for hh in range(nh)]
                   + [chain(*[recur(first - bpi + r, states) for r in range(bpi)])])
        return tuple(states)

    interleave([prepare(r, hh) for r in range(bpi) for hh in range(nh)])
    states = lax.fori_loop(1, n_sb // bpi, body, tuple(jnp.zeros((HEAD_DIM, HEAD_DIM), F32) for _ in range(nh)))
    states = list(states)
    interleave([chain(*[recur(n_sb - bpi + r, states) for r in range(bpi)])])
    for hh in range(nh):
        s_ref[hh] = states[hh]


def _delta_prompt(z_main, conv_w, gc, beta, norm_w, n_heads, mix_width, hpb=2):
    b, t, _ = z_main.shape
    wl = hpb * HEAD_DIM
    ng = n_heads // hpb
    n_chunks = t // DELTA_CHUNK
    col = lambda off: pl.BlockSpec((None, t, wl), lambda a, h: (a, 0, off + h))
    cw = lambda off: pl.BlockSpec((CONV_W, wl), lambda a, h: (0, off + h))
    gate = pl.BlockSpec((None, hpb, t, HEAD_DIM), lambda a, h: (a, h, 0, 0))
    return pl.pallas_call(
        functools.partial(_delta_prompt_kernel, hpb),
        out_shape=[jax.ShapeDtypeStruct((b, t, mix_width), BF16),
                   jax.ShapeDtypeStruct((b, n_heads, HEAD_DIM, HEAD_DIM), F32)],
        grid=(b, ng),
        in_specs=[col(0), col(ng), col(2 * ng), col(3 * ng),
                  cw(0), cw(ng), cw(2 * ng), gate, gate,
                  pl.BlockSpec((1, HEAD_DIM), lambda a, h: (0, 0))],
        out_specs=[pl.BlockSpec((None, t, wl), lambda a, h: (a, 0, h)),
                   pl.BlockSpec((None, hpb, HEAD_DIM, HEAD_DIM), lambda a, h: (a, h, 0, 0))],
        scratch_shapes=[pltpu.VMEM((hpb, n_chunks, 2 * DELTA_CHUNK, HEAD_DIM), BF16),
                        pltpu.VMEM((hpb, n_chunks, HEAD_DIM + DELTA_CHUNK, 2 * DELTA_CHUNK), BF16),
                        pltpu.VMEM((hpb, t, HEAD_DIM), F32),
                        pltpu.VMEM((hpb, n_chunks, 8, HEAD_DIM), F32)],
        compiler_params=_cparams("arbitrary", "arbitrary"),
        name="delta_prompt",
    )(z_main, z_main, z_main, z_main, conv_w, conv_w, conv_w, gc, beta, norm_w.reshape(1, HEAD_DIM))


def _xpos(x, cos, sin_e, sin_o):
    return x * cos + pltpu.roll(x, HEAD_DIM - 1, 1) * sin_e + pltpu.roll(x, 1, 1) * sin_o


def _log_gamma(h):
    return jnp.log1p(-jnp.exp2(-5.0 - h))


def _ret_prompt_kernel(q_ref, k_ref, v_ref, g_ref, cos_ref, sine_ref, sino_ref, mix_ref, o_ref, s_ref,
                       qs_ref, ks_ref):
    del mix_ref
    t = q_ref.shape[0]
    c = RET_CHUNK
    cos, sin_e, sin_o = cos_ref[...], sine_ref[...], sino_ref[...]
    qs_ref[...] = _xpos(q_ref[...], cos, sin_e, sin_o)
    ks_ref[...] = _xpos(k_ref[...], cos, sin_e, sin_o) * (HEAD_DIM ** -0.5)

    h = jnp.full((1, 1), pl.program_id(1), jnp.int32).astype(F32)
    lg = _log_gamma(h)
    ii = lax.broadcasted_iota(jnp.int32, (c, c), 0)
    jj = lax.broadcasted_iota(jnp.int32, (c, c), 1)
    dec = jnp.exp(jnp.where(ii >= jj, (ii - jj).astype(F32) * lg, -jnp.inf))
    pos = lax.broadcasted_iota(jnp.int32, (c, HEAD_DIM), 0).astype(F32)
    e_in = jnp.exp((pos + 1.0) * lg)
    e_out = jnp.exp((c - 1.0 - pos) * lg)
    e_all = jnp.exp(c * lg)

    unroll = 2

    def body(n, s):
        rows = [pl.ds(pl.multiple_of((n * unroll + r) * c, c), c) for r in range(unroll)]
        q = [qs_ref[rw] for rw in rows]
        k = [ks_ref[rw] for rw in rows]
        v = [v_ref[rw].astype(BF16) for rw in rows]
        qk = [_dot_nt(q[r], k[r]) for r in range(unroll)]
        upd = [_dot((k[r] * e_out).T, v[r]) for r in range(unroll)]
        for r in range(unroll):
            o = _dot(q[r] * e_in, s) + _dot(qk[r] * dec, v[r])
            s = s * e_all + upd[r]
            o_ref[rows[r]] = (_rms_rows(o) * _silu(g_ref[rows[r]])).astype(o_ref.dtype)
        return s

    s_ref[...] = lax.fori_loop(0, t // (c * unroll), body, jnp.zeros((HEAD_DIM, HEAD_DIM), F32))


def _ret_prompt(z_tail, cos, sin_e, sin_o, mix, mix_col0, n_heads):
    b, t, _ = z_tail.shape
    c0 = mix_col0 // HEAD_DIM
    col = lambda off: pl.BlockSpec((None, t, HEAD_DIM), lambda a, h: (a, 0, off + h))
    tab = pl.BlockSpec((t, HEAD_DIM), lambda a, h: (0, 0))
    return pl.pallas_call(
        _ret_prompt_kernel,
        out_shape=[jax.ShapeDtypeStruct(mix.shape, mix.dtype),
                   jax.ShapeDtypeStruct((b, n_heads, HEAD_DIM, HEAD_DIM), F32)],
        grid=(b, n_heads),
        in_specs=[col(0), col(n_heads), col(2 * n_heads), col(3 * n_heads), tab, tab, tab,
                  pl.BlockSpec(memory_space=pl.ANY)],
        out_specs=[pl.BlockSpec((None, t, HEAD_DIM), lambda a, h: (a, 0, c0 + h)),
                   pl.BlockSpec((None, None, HEAD_DIM, HEAD_DIM), lambda a, h: (a, h, 0, 0))],
        scratch_shapes=[pltpu.VMEM((t, HEAD_DIM), F32)] * 2,
        input_output_aliases={7: 0},
        compiler_params=_cparams("arbitrary", "arbitrary"),
        name="ret_prompt",
    )(z_tail, z_tail, z_tail, z_tail, cos, sin_e, sin_o, mix)


def _lambda(lq, lam_init):
    l01 = jnp.sum(lq[0:1] * lq[1:2], axis=-1, keepdims=True)
    l23 = jnp.sum(lq[2:3] * lq[3:4], axis=-1, keepdims=True)
    return jnp.exp(l01) - jnp.exp(l23) + lam_init


def _attn_prompt_kernel(lam_init, nh, tq, q_ref, k_ref, v_ref, lq_ref, sub_ref, mix_ref, o_ref,
                        q2_ref, s_ref, p_ref, m_ref, l_ref, acc_ref):
    del mix_ref
    i = pl.program_id(2)
    strip = 64
    lane = lax.broadcasted_iota(jnp.int32, (tq, HEAD_DIM), 1)
    scale = HALF_DIM ** -0.5
    for hh in range(nh):
        q = q_ref[:, hh * HEAD_DIM:(hh + 1) * HEAD_DIM]
        q2_ref[hh] = (jnp.concatenate([jnp.where(lane < HALF_DIM, q, 0.0), jnp.where(lane >= HALF_DIM, q, 0.0)],
                                      axis=0) * scale).astype(BF16)
    m_ref[...] = jnp.full(m_ref.shape, -jnp.inf, F32)
    l_ref[...] = jnp.zeros_like(l_ref)
    acc_ref[...] = jnp.zeros_like(acc_ref)
    key_idx = lax.broadcasted_iota(jnp.int32, (strip, tq), 1)
    qry_idx = lax.broadcasted_iota(jnp.int32, (strip, tq), 0)

    def step(j, diagonal):
        rows = pl.ds(pl.multiple_of(j * tq, tq), tq)
        for hh in range(nh):
            s_ref[hh] = _dot_nt(q2_ref[hh], k_ref[rows, hh * HEAD_DIM:(hh + 1) * HEAD_DIM])
        for hh in range(nh):
            for c in range(2 * tq // strip):
                sr = slice(c * strip, (c + 1) * strip)
                s = s_ref[hh, sr]
                if diagonal:
                    q0 = (c * strip) % tq
                    s = jnp.where(key_idx <= qry_idx + q0, s, -jnp.inf)
                m_old = m_ref[hh, sr]
                m_new = jnp.maximum(m_old, jnp.max(s, axis=-1, keepdims=True))
                alpha = jnp.exp(m_old - m_new)
                p = jnp.exp(s - jnp.concatenate([m_new] * (tq // HEAD_DIM), axis=1))
                l_ref[hh, sr] = alpha * l_ref[hh, sr] + jnp.sum(p, axis=-1, keepdims=True)
                m_ref[hh, sr] = m_new
                p_ref[hh, sr] = p.astype(BF16)
                acc_ref[hh, sr] = alpha * acc_ref[hh, sr]
            acc_ref[hh] += jnp.dot(p_ref[hh], v_ref[rows, hh * HEAD_DIM:(hh + 1) * HEAD_DIM].astype(BF16),
                                   preferred_element_type=F32)

    def body(j, carry):
        step(j, False)
        return carry

    lax.fori_loop(0, i, body, 0)
    step(i, True)
    lam = _lambda(lq_ref[...], lam_init)
    for hh in range(nh):
        o2 = acc_ref[hh] / l_ref[hh]
        o = o2[:tq] - lam * o2[tq:]
        o_ref[:, hh * HEAD_DIM:(hh + 1) * HEAD_DIM] = (_rms_rows(o) * sub_ref[...] * (1.0 - lam_init)).astype(o_ref.dtype)


def _attn_prompt(z_tail, lambda_qk, subln, lam_init, mix, mix_col0, n_heads, col0, tq=256, hpb=2):
    b, t, _ = z_tail.shape
    wl = hpb * HEAD_DIM
    ng = n_heads // hpb
    c0 = col0 // hpb
    m0 = mix_col0 // wl
    return pl.pallas_call(
        functools.partial(_attn_prompt_kernel, lam_init, hpb, tq),
        out_shape=jax.ShapeDtypeStruct(mix.shape, mix.dtype),
        grid=(b, ng, t // tq),
        in_specs=[pl.BlockSpec((None, tq, wl), lambda a, h, i: (a, i, c0 + h)),
                  pl.BlockSpec((None, t, wl), lambda a, h, i: (a, 0, c0 + ng + h)),
                  pl.BlockSpec((None, t, wl), lambda a, h, i: (a, 0, c0 + 2 * ng + h)),
                  pl.BlockSpec((4, HALF_DIM), lambda a, h, i: (0, 0)),
                  pl.BlockSpec((1, HEAD_DIM), lambda a, h, i: (0, 0)),
                  pl.BlockSpec(memory_space=pl.ANY)],
        out_specs=pl.BlockSpec((None, tq, wl), lambda a, h, i: (a, i, m0 + h)),
        scratch_shapes=[pltpu.VMEM((hpb, 2 * tq, HEAD_DIM), BF16),
                        pltpu.VMEM((hpb, 2 * tq, tq), F32),
                        pltpu.VMEM((hpb, 2 * tq, tq), BF16),
                        pltpu.VMEM((hpb, 2 * tq, HEAD_DIM), F32),
                        pltpu.VMEM((hpb, 2 * tq, HEAD_DIM), F32),
                        pltpu.VMEM((hpb, 2 * tq, HEAD_DIM), F32)],
        input_output_aliases={5: 0},
        compiler_params=_cparams("arbitrary", "arbitrary", "arbitrary"),
        name="attn_prompt",
    )(z_tail, z_tail, z_tail, lambda_qk, subln.reshape(1, HEAD_DIM), mix)


def _lane_pick(x, h):
    lane = lax.broadcasted_iota(jnp.int32, x.shape, 1)
    return jnp.sum(jnp.where(lane == h, x, 0.0), axis=-1, keepdims=True)


def _row_to_col(x):
    ii = lax.broadcasted_iota(jnp.int32, (HEAD_DIM, HEAD_DIM), 0)
    jj = lax.broadcasted_iota(jnp.int32, (HEAD_DIM, HEAD_DIM), 1)
    return jnp.sum(jnp.where(ii == jj, jnp.broadcast_to(x, (HEAD_DIM, HEAD_DIM)), 0.0), axis=-1, keepdims=True)


def _rows8(*rows):
    pad = jnp.zeros((8 - len(rows), HEAD_DIM), F32)
    return jnp.concatenate(list(rows) + [pad], axis=0)


def _delta_sample_kernel(q_ref, k_ref, v_ref, za_ref, bq_ref, bk_ref, bv_ref, cq_ref, ck_ref, cv_ref,
                         zg_ref, alog_ref, dtb_ref, norm_ref, s_ref, o_ref, so_ref):
    h = pl.program_id(1)
    n_heads = alog_ref.shape[1]

    def conv_silu(x_ref, buf_ref, w_ref):
        w = w_ref[...]
        buf = buf_ref[...]
        y = x_ref[...] * w[CONV_W - 1:CONV_W]
        for j in range(CONV_W - 1):
            y = y + buf[j:j + 1] * w[j:j + 1]
        return _silu(y)

    def l2norm(x):
        return x * lax.rsqrt(jnp.sum(x * x, axis=-1, keepdims=True) + 1e-6)

    q = l2norm(conv_silu(q_ref, bq_ref, cq_ref)) * (HEAD_DIM ** -0.5)
    k = l2norm(conv_silu(k_ref, bk_ref, ck_ref))
    v = conv_silu(v_ref, bv_ref, cv_ref)
    zg = zg_ref[...]
    a_a = _lane_pick(zg, h)
    b_a = _lane_pick(zg, h + n_heads)
    g = -jnp.exp(_lane_pick(alog_ref[...], h)) * _softplus(a_a + _lane_pick(dtb_ref[...], h))
    beta = _sigmoid(b_a)
    eg = jnp.exp(g)
    s = s_ref[...]
    ks_qs = _dot(_rows8(k, q), s)
    v_new = beta * (v - eg * ks_qs[0:1])
    qk = jnp.sum(q * k, axis=-1, keepdims=True)
    o = eg * ks_qs[1:2] + qk * v_new
    so_ref[...] = s * eg + _row_to_col(k) * v_new
    o_ref[...] = (_rms_rows(o) * norm_ref[...] * _silu(za_ref[...])).astype(o_ref.dtype)


def _delta_sample(zs_main, zs_gate, conv_buf, conv_w, a_log, dt_bias, norm_w, state, layer, n_heads):
    nb = state.shape[0]
    col = lambda off: pl.BlockSpec((None, 1, HEAD_DIM), lambda a, h: (a, 0, off + h))
    buf = lambda off: pl.BlockSpec((None, None, CONV_W - 1, HEAD_DIM), lambda a, h: (a, layer, 0, off + h))
    cw = lambda off: pl.BlockSpec((CONV_W, HEAD_DIM), lambda a, h: (0, off + h))
    vec = pl.BlockSpec((1, n_heads), lambda a, h: (0, 0))
    return pl.pallas_call(
        _delta_sample_kernel,
        out_shape=[jax.ShapeDtypeStruct((nb, 1, n_heads * HEAD_DIM), BF16),
                   jax.ShapeDtypeStruct((nb, n_heads, HEAD_DIM, HEAD_DIM), F32)],
        grid=(nb, n_heads),
        in_specs=[col(0), col(n_heads), col(2 * n_heads), col(3 * n_heads),
                  buf(0), buf(n_heads), buf(2 * n_heads), cw(0), cw(n_heads), cw(2 * n_heads),
                  pl.BlockSpec((None, 1, 2 * n_heads), lambda a, h: (a, 0, 0)), vec, vec,
                  pl.BlockSpec((1, HEAD_DIM), lambda a, h: (0, 0)),
                  pl.BlockSpec((None, None, None, HEAD_DIM, HEAD_DIM), lambda a, h: (a, layer, h, 0, 0))],
        out_specs=[pl.BlockSpec((None, 1, HEAD_DIM), lambda a, h: (a, 0, h)),
                   pl.BlockSpec((None, None, HEAD_DIM, HEAD_DIM), lambda a, h: (a, h, 0, 0))],
        compiler_params=_cparams("arbitrary", "arbitrary"),
        name="delta_sample",
    )(zs_main, zs_main, zs_main, zs_main, conv_buf, conv_buf, conv_buf, conv_w, conv_w, conv_w,
      zs_gate, a_log.reshape(1, n_heads), dt_bias.reshape(1, n_heads), norm_w.reshape(1, HEAD_DIM), state)


def _ret_sample_kernel(q_ref, k_ref, v_ref, g_ref, cos_ref, sine_ref, sino_ref, s_ref, o_ref, so_ref):
    h = jnp.full((1, 1), pl.program_id(1), jnp.int32).astype(F32)
    gamma = jnp.exp(_log_gamma(h))
    cos, sin_e, sin_o = cos_ref[...], sine_ref[...], sino_ref[...]
    q = _xpos(q_ref[...], cos, sin_e, sin_o)
    k = _xpos(k_ref[...], cos, sin_e, sin_o) * (HEAD_DIM ** -0.5)
    v = v_ref[...]
    s = s_ref[...]
    qs = _dot(_rows8(q), s)
    qk = jnp.sum(q * k, axis=-1, keepdims=True)
    o = qk * v + gamma * qs[0:1]
    so_ref[...] = s * gamma + _row_to_col(k) * v
    o_ref[...] = (_rms_rows(o) * _silu(g_ref[...])).astype(o_ref.dtype)


def _ret_sample(zs_tail, cos, sin_e, sin_o, state, layer, n_heads):
    nb = state.shape[0]
    col = lambda off: pl.BlockSpec((None, 1, HEAD_DIM), lambda a, h: (a, 0, off + h))
    tab = pl.BlockSpec((1, HEAD_DIM), lambda a, h: (0, 0))
    return pl.pallas_call(
        _ret_sample_kernel,
        out_shape=[jax.ShapeDtypeStruct((nb, 1, n_heads * HEAD_DIM), BF16),
                   jax.ShapeDtypeStruct((nb, n_heads, HEAD_DIM, HEAD_DIM), F32)],
        grid=(nb, n_heads),
        in_specs=[col(0), col(n_heads), col(2 * n_heads), col(3 * n_heads), tab, tab, tab,
                  pl.BlockSpec((None, None, None, HEAD_DIM, HEAD_DIM), lambda a, h: (a, layer, h, 0, 0))],
        out_specs=[pl.BlockSpec((None, 1, HEAD_DIM), lambda a, h: (a, 0, h)),
                   pl.BlockSpec((None, None, HEAD_DIM, HEAD_DIM), lambda a, h: (a, h, 0, 0))],
        compiler_params=_cparams("arbitrary", "arbitrary"),
        name="ret_sample",
    )(zs_tail, zs_tail, zs_tail, zs_tail, cos, sin_e, sin_o, state)


def _attn_decode_kernel(lam_init, ppb, pt_ref, q_ref, kn_ref, vn_ref, *rest):
    kc_refs, vc_refs = rest[:ppb], rest[ppb:2 * ppb]
    lq_ref, sub_ref, o_ref, m_ref, l_ref, acc_ref, s_ref = rest[2 * ppb:]
    j = pl.program_id(1)
    n_steps = pl.num_programs(1)
    nh = q_ref.shape[0]
    scale = HALF_DIM ** -0.5
    di = lax.broadcasted_iota(jnp.int32, (HEAD_DIM, 2 * HEAD_DIM), 0)
    ci = lax.broadcasted_iota(jnp.int32, (HEAD_DIM, 2 * HEAD_DIM), 1)
    expand = ((di // HALF_DIM) == (ci // HEAD_DIM)).astype(BF16)
    q = q_ref[...] * (scale * math.log2(math.e))
    strip = 16
    n_strips = PAGE_SIZE // strip

    @pl.when(j == 0)
    def _():
        m_ref[...] = jnp.full(m_ref.shape, -jnp.inf, F32)
        l_ref[...] = jnp.zeros_like(l_ref)
        acc_ref[...] = jnp.zeros_like(acc_ref)

    def scores(kt):
        n_tok = kt.shape[0]
        prod = (kt * q[None]).reshape(n_tok * nh, HEAD_DIM)
        s = jnp.dot(prod.astype(BF16), expand, preferred_element_type=F32)
        return s.reshape(n_tok, nh, 2 * HEAD_DIM)

    def weighted(p, vt):
        return jnp.concatenate([jnp.sum(p[:, :, :HEAD_DIM] * vt, axis=0),
                                jnp.sum(p[:, :, HEAD_DIM:] * vt, axis=0)], axis=-1)

    m, l, acc = m_ref[...], l_ref[...], acc_ref[...]
    s_ref[0] = scores(kc_refs[0][...])
    for pg in range(ppb):
        if pg + 1 < ppb:
            s_ref[pg + 1] = scores(kc_refs[pg + 1][...])
        m_new = m
        for i in range(n_strips):
            m_new = jnp.maximum(m_new, jnp.max(s_ref[pg, i * strip:(i + 1) * strip], axis=0))
        alpha = jnp.exp2(m - m_new)
        l, acc, m = alpha * l, alpha * acc, m_new
        for i in range(n_strips):
            p = jnp.exp2(s_ref[pg, i * strip:(i + 1) * strip] - m[None])
            l = l + jnp.sum(p, axis=0)
            acc = acc + weighted(p, vc_refs[pg][i * strip:(i + 1) * strip])
    m_ref[...] = m
    l_ref[...] = l
    acc_ref[...] = acc

    @pl.when(j == n_steps - 1)
    def _():
        s = scores(kn_ref[...][None])[0]
        m_old = m_ref[...]
        m_new = jnp.maximum(m_old, s)
        alpha = jnp.exp2(m_old - m_new)
        p = jnp.exp2(s - m_new)
        l_ref[...] = alpha * l_ref[...] + p
        acc_ref[...] = alpha * acc_ref[...] + weighted(p[None], vn_ref[...][None])
        o2 = acc_ref[...] / l_ref[...]
        lam = _lambda(lq_ref[...], lam_init)
        o = o2[:, :HEAD_DIM] - lam * o2[:, HEAD_DIM:]
        o_ref[...] = (_rms_rows(o) * sub_ref[...] * (1.0 - lam_init)).astype(o_ref.dtype)


def _attn_decode(q, k_new, v_new, cache_k, cache_v, page_table, lambda_qk, subln, lam_init, layer, ppb=4):
    nb, nh, _ = q.shape
    n_pages = page_table.shape[1]
    assert n_pages % ppb == 0
    tok = pl.BlockSpec((None, nh, HEAD_DIM), lambda b, j, pt: (b, 0, 0))

    def page(r):
        return pl.BlockSpec((None, None, PAGE_SIZE, nh, HEAD_DIM),
                            lambda b, j, pt: (pt[b, j * ppb + r], layer, 0, 0, 0))

    pages = [page(r) for r in range(ppb)]
    grid_spec = pltpu.PrefetchScalarGridSpec(
        num_scalar_prefetch=1,
        grid=(nb, n_pages // ppb),
        in_specs=[tok, tok, tok] + pages + pages +
                 [pl.BlockSpec((4, HALF_DIM), lambda b, j, pt: (0, 0)),
                  pl.BlockSpec((1, HEAD_DIM), lambda b, j, pt: (0, 0))],
        out_specs=tok,
        scratch_shapes=[pltpu.VMEM((nh, 2 * HEAD_DIM), F32)] * 3 +
                       [pltpu.VMEM((ppb, PAGE_SIZE, nh, 2 * HEAD_DIM), F32)],
    )
    return pl.pallas_call(
        functools.partial(_attn_decode_kernel, lam_init, ppb),
        out_shape=jax.ShapeDtypeStruct((nb, nh, HEAD_DIM), BF16),
        grid_spec=grid_spec,
        compiler_params=_cparams("arbitrary", "arbitrary"),
        name="attn_decode",
    )(page_table, q, k_new, v_new, *([cache_k] * ppb), *([cache_v] * ppb), lambda_qk, subln.reshape(1, HEAD_DIM))


def _xpos_tables(pos):
    half = HEAD_DIM // 2
    angle = jnp.repeat(1.0 / (10000.0 ** jnp.linspace(0.0, 1.0, half, dtype=F32)), 2)
    ph = pos.astype(F32)[:, None] * angle
    sin, cos = jnp.sin(ph), jnp.cos(ph)
    even = (jnp.arange(HEAD_DIM) % 2) == 0
    return cos, jnp.where(even, -sin, 0.0), jnp.where(even, 0.0, sin)


def kernel(x_prompt, x_sample, state_conv_a, state_delta, state_ret, cache_k, cache_v, page_table, c_prompt, c_sample, w_ada, b_ada, w_in, conv_a, a_log, dt_bias, norm_a, lambda_qk, subln_c, w_out, ln1_g, ln1_b, w_up, w_down, ln2_g, ln2_b):
    bp, t, d = x_prompt.shape
    nb = x_sample.shape[0]
    depth = w_in.shape[0]
    h_a = state_delta.shape[2]
    h_b = state_ret.shape[2]
    h_c = cache_k.shape[3]
    a_w, b_w, c_w = h_a * HEAD_DIM, h_b * HEAD_DIM, h_c * HEAD_DIM
    n_main = 4 * a_w
    n_gate = 2 * h_a
    past_len = page_table.shape[1] * PAGE_SIZE
    alpha = (2 * depth) ** 0.25
    rs = SAMPLE_ROWS
    assert nb + bp <= rs and x_sample.shape[1] == 1

    c_all = jnp.concatenate([c_sample, c_prompt, jnp.zeros((rs - nb - bp, d), F32)], axis=0)
    mod = _ada(c_all, w_ada, b_ada)

    def mod_p(l, i):
        return mod[l, nb:nb + bp, None, i * d:(i + 1) * d]

    def mod_s(l, i):
        return mod[l][None, :, i * d:(i + 1) * d]

    cos_p, sine_p, sino_p = _xpos_tables(jnp.arange(t))
    cos_s, sine_s, sino_s = _xpos_tables(past_len + jnp.arange(1))

    xp = x_prompt
    xs = jnp.concatenate([x_sample.reshape(nb, d), jnp.zeros((rs - nb, d), F32)], axis=0)[None]
    hp = _modulate(xp, mod_p(0, 1), mod_p(0, 0), 256)
    hs = _modulate(xs, mod_s(0, 1), mod_s(0, 0), rs)

    w_in_b = w_in.astype(BF16)
    w_gate_b = w_in_b[:, :, n_main:n_main + n_gate]
    w_tail_b = w_in_b[:, :, n_main + n_gate:]
    w_down_b = w_down.astype(BF16)

    st_p, st_s = [], []
    for l in range(depth):
        lam_init = 0.8 - 0.6 * math.exp(-0.3 * l)
        hp2 = hp.reshape(bp * t, d)
        hs2 = hs.reshape(rs, d)
        zp_main, zs_main = _matmul(hp2, hs2, w_in_b, l, 0, n_main)
        zp_gate, zs_gate = _matmul(hp2, hs2, w_gate_b, l)
        zp_tail, zs_tail = _matmul(hp2, hs2, w_tail_b, l)
        zp_main = zp_main.reshape(bp, t, n_main)
        zp_tail = zp_tail.reshape(bp, t, -1)

        gc, beta = _gates(zp_gate.reshape(bp, t, n_gate), a_log[l], dt_bias[l])
        mix_p, delta_p = _delta_prompt(zp_main, conv_a[l], gc, beta, norm_a[l], h_a, a_w + b_w + c_w)
        mix_p, ret_p = _ret_prompt(zp_tail, cos_p, sine_p, sino_p, mix_p, a_w, h_b)
        mix_p = _attn_prompt(zp_tail, lambda_qk[l], subln_c[l], lam_init, mix_p, a_w + b_w, h_c, 4 * h_b)
        k_off = 4 * b_w + c_w
        conv_p = zp_main[:, t - (CONV_W - 1):, :3 * a_w]
        k_p = zp_tail[:, :, k_off:k_off + c_w].reshape(bp, t, h_c, HEAD_DIM)
        v_p = zp_tail[:, :, k_off + c_w:k_off + 2 * c_w].reshape(bp, t, h_c, HEAD_DIM)
        st_p.append((conv_p, delta_p, ret_p, k_p, v_p))

        zs_m = zs_main[:nb]
        zs_t = zs_tail[:nb]
        oa_s, delta_s = _delta_sample(zs_m[:, None], zs_gate[:nb, None], state_conv_a, conv_a[l], a_log[l],
                                      dt_bias[l], norm_a[l], state_delta, l, h_a)
        ob_s, ret_s = _ret_sample(zs_t[:, None], cos_s, sine_s, sino_s, state_ret, l, h_b)
        q_s = zs_t[:, 4 * b_w:4 * b_w + c_w].reshape(nb, h_c, HEAD_DIM)
        k_s = zs_t[:, k_off:k_off + c_w].reshape(nb, h_c, HEAD_DIM)
        v_s = zs_t[:, k_off + c_w:k_off + 2 * c_w].reshape(nb, h_c, HEAD_DIM)
        oc_s = _attn_decode(q_s, k_s, v_s, cache_k, cache_v, page_table, lambda_qk[l], subln_c[l], lam_init, l)
        conv_s = jnp.concatenate([state_conv_a[:, l, 1:], zs_m[:, None, :3 * a_w]], axis=1)
        st_s.append((conv_s, delta_s, ret_s, k_s[:, None], v_s[:, None]))

        mix_p = mix_p.reshape(bp * t, -1)
        mix_s = jnp.concatenate([oa_s.reshape(nb, a_w), ob_s.reshape(nb, b_w), oc_s.reshape(nb, c_w)], axis=-1)
        mix_s = jnp.concatenate([mix_s, jnp.zeros((rs - nb, mix_s.shape[1]), BF16)], axis=0)
        yp, ys = _panel_matmul(mix_p, mix_s, w_out, l, 0, d)
        xp, hp = _postnorm(xp, yp.reshape(bp, t, d), mod_p(l, 2), ln1_g[l], ln1_b[l], mod_p(l, 4), mod_p(l, 3),
                           alpha, 256)
        xs, hs = _postnorm(xs, ys[None], mod_s(l, 2), ln1_g[l], ln1_b[l], mod_s(l, 4), mod_s(l, 3), alpha, rs)

        up, us = _panel_matmul(hp.reshape(bp * t, d), hs.reshape(rs, d), w_up, l, 0, w_up.shape[2],
                               out_dtype=BF16, relu2=True)
        fp, fs = _matmul(up, us, w_down_b, l)
        last = l == depth - 1
        nl = min(l + 1, depth - 1)
        xp, hp = _postnorm(xp, fp.reshape(bp, t, d), mod_p(l, 5), ln2_g[l], ln2_b[l], mod_p(nl, 1), mod_p(nl, 0),
                           alpha, 256, with_h=not last)
        xs, hs = _postnorm(xs, fs[None], mod_s(l, 5), ln2_g[l], ln2_b[l], mod_s(nl, 1), mod_s(nl, 0), alpha, rs,
                           with_h=not last)

    outs_p = tuple(jnp.stack([s[i] for s in st_p], axis=1) for i in range(5))
    outs_s = tuple(jnp.stack([s[i] for s in st_s], axis=1) for i in range(5))
    y_sample = xs[0, :nb].reshape(nb, 1, d)
    return (xp, y_sample) + outs_p + outs_s
```

```python
import functools
import math

import jax
import jax.numpy as jnp
from jax import lax
from jax.experimental import pallas as pl
from jax.experimental.pallas import tpu as pltpu

F32 = jnp.float32
BF16 = jnp.bfloat16

HEAD_DIM = 128
HALF_DIM = HEAD_DIM // 2
CONV_W = 4
DELTA_CHUNK = 64
DELTA_SUPER = 256
DELTA_BASE = 4
RET_CHUNK = 128
PAGE_SIZE = 128
EPS = 1e-5
SAMPLE_ROWS = 16
VMEM_LIMIT = 58 * 1024 * 1024


def _cparams(*sem):
    return pltpu.CompilerParams(dimension_semantics=sem, vmem_limit_bytes=VMEM_LIMIT)


def _sigmoid(x):
    return 1.0 / (1.0 + jnp.exp(-x))


def _silu(x):
    return x * _sigmoid(x)


def _softplus(x):
    return jnp.maximum(x, 0.0) + jnp.log1p(jnp.exp(-jnp.abs(x)))


def _dot(a, b):
    return jnp.dot(a.astype(BF16), b.astype(BF16), preferred_element_type=F32)


def _dot_nt(a, b):
    return lax.dot_general(a.astype(BF16), b.astype(BF16), (((1,), (1,)), ((), ())),
                           preferred_element_type=F32)


def _rms_rows(x):
    return x * lax.rsqrt(jnp.mean(x * x, axis=-1, keepdims=True) + EPS)


def _ada_kernel(c_ref, w0_ref, w1_ref, b_ref, o_ref):
    c = _silu(c_ref[...])
    dh = w0_ref.shape[0]
    o_ref[...] = _dot(c[:, :dh], w0_ref[...]) + _dot(c[:, dh:], w1_ref[...]) + b_ref[...]


def _ada(c_all, w_ada, b_ada, tn=1024):
    depth, d, n = w_ada.shape
    rows = c_all.shape[0]
    return pl.pallas_call(
        _ada_kernel,
        out_shape=jax.ShapeDtypeStruct((depth, rows, n), F32),
        grid=(depth, n // tn),
        in_specs=[pl.BlockSpec((rows, d), lambda l, j: (0, 0)),
                  pl.BlockSpec((None, d // 2, tn), lambda l, j: (l, 0, j)),
                  pl.BlockSpec((None, d // 2, tn), lambda l, j: (l, 1, j)),
                  pl.BlockSpec((None, 1, tn), lambda l, j: (l, 0, j))],
        out_specs=pl.BlockSpec((None, rows, tn), lambda l, j: (l, 0, j)),
        compiler_params=_cparams("arbitrary", "arbitrary"),
        name="ada_mod",
    )(c_all, w_ada, w_ada, b_ada.reshape(depth, 1, n))


def _modulate_kernel(x_ref, sc_ref, sh_ref, o_ref):
    o_ref[...] = (x_ref[...] * (1.0 + sc_ref[...]) + sh_ref[...]).astype(o_ref.dtype)


def _modulate(x, sc, sh, tm):
    g, r, d = x.shape
    rm = sc.shape[1]
    mspec = pl.BlockSpec((None, rm, d), lambda a, i: (a, 0, 0))
    return pl.pallas_call(
        _modulate_kernel,
        out_shape=jax.ShapeDtypeStruct((g, r, d), BF16),
        grid=(g, r // tm),
        in_specs=[pl.BlockSpec((None, tm, d), lambda a, i: (a, i, 0)), mspec, mspec],
        out_specs=pl.BlockSpec((None, tm, d), lambda a, i: (a, i, 0)),
        compiler_params=_cparams("arbitrary", "arbitrary"),
        name="modulate",
    )(x, sc, sh)


def _postnorm_kernel(alpha, with_h, x_ref, y_ref, gate_ref, lg_ref, lb_ref, sc_ref, sh_ref, xo_ref, *h_ref):
    v = alpha * x_ref[...] + (1.0 + gate_ref[...]) * y_ref[...]
    vc = v - jnp.mean(v, axis=-1, keepdims=True)
    var = jnp.mean(vc * vc, axis=-1, keepdims=True)
    xn = vc * lax.rsqrt(var + EPS) * lg_ref[...] + lb_ref[...]
    xo_ref[...] = xn
    if with_h:
        h_ref[0][...] = (xn * (1.0 + sc_ref[...]) + sh_ref[...]).astype(BF16)


def _postnorm(x, y, gate, ln_g, ln_b, sc, sh, alpha, tm, with_h=True):
    g, r, d = x.shape
    rm = gate.shape[1]
    row = pl.BlockSpec((None, tm, d), lambda a, i: (a, i, 0))
    mspec = pl.BlockSpec((None, rm, d), lambda a, i: (a, 0, 0))
    vec = pl.BlockSpec((1, d), lambda a, i: (0, 0))
    out_shape = [jax.ShapeDtypeStruct((g, r, d), F32)]
    out_specs = [row]
    if with_h:
        out_shape.append(jax.ShapeDtypeStruct((g, r, d), BF16))
        out_specs.append(row)
    res = pl.pallas_call(
        functools.partial(_postnorm_kernel, alpha, with_h),
        out_shape=out_shape,
        grid=(g, r // tm),
        in_specs=[row, row, mspec, vec, vec, mspec, mspec],
        out_specs=out_specs,
        compiler_params=_cparams("arbitrary", "arbitrary"),
        name="postnorm",
    )(x, y, gate, ln_g.reshape(1, d), ln_b.reshape(1, d), sc, sh)
    return (res[0], res[1]) if with_h else (res[0], None)


def _matmul_kernel(relu2, nk, a_ref, as_ref, w_ref, o_ref, os_ref, *scratch):
    i = pl.program_id(1)
    k = pl.program_id(2)

    def finish(v):
        return jnp.square(jnp.maximum(v, 0.0)) if relu2 else v

    if nk == 1:
        o_ref[...] = finish(jnp.dot(a_ref[...], w_ref[...], preferred_element_type=F32)).astype(o_ref.dtype)

        @pl.when(i == 0)
        def _():
            os_ref[...] = finish(jnp.dot(as_ref[...], w_ref[...], preferred_element_type=F32)).astype(os_ref.dtype)
    else:
        acc_ref, accs_ref = scratch

        @pl.when(k == 0)
        def _():
            acc_ref[...] = jnp.zeros_like(acc_ref)

        acc_ref[...] += jnp.dot(a_ref[...], w_ref[...], preferred_element_type=F32)

        @pl.when(k == nk - 1)
        def _():
            o_ref[...] = finish(acc_ref[...]).astype(o_ref.dtype)

        @pl.when(i == 0)
        def _():
            @pl.when(k == 0)
            def _():
                accs_ref[...] = jnp.zeros_like(accs_ref)

            accs_ref[...] += jnp.dot(as_ref[...], w_ref[...], preferred_element_type=F32)

            @pl.when(k == nk - 1)
            def _():
                os_ref[...] = finish(accs_ref[...]).astype(os_ref.dtype)


def _matmul(a, a_s, w3, layer, col0=0, n=None, out_dtype=F32, relu2=False, tm=1024, tn=1024, tk=4096):
    m, kdim = a.shape
    n = w3.shape[2] if n is None else n
    rs = a_s.shape[0]
    tn = min(tn, n)
    tk = min(tk, kdim)
    nk = kdim // tk
    assert n % tn == 0 and col0 % tn == 0 and m % tm == 0 and kdim % tk == 0
    cb0 = col0 // tn
    scratch = [] if nk == 1 else [pltpu.VMEM((tm, tn), F32), pltpu.VMEM((rs, tn), F32)]
    return pl.pallas_call(
        functools.partial(_matmul_kernel, relu2, nk),
        out_shape=[jax.ShapeDtypeStruct((m, n), out_dtype), jax.ShapeDtypeStruct((rs, n), out_dtype)],
        grid=(n // tn, m // tm, nk),
        in_specs=[pl.BlockSpec((tm, tk), lambda j, i, k: (i, k)),
                  pl.BlockSpec((rs, tk), lambda j, i, k: (0, k)),
                  pl.BlockSpec((None, tk, tn), lambda j, i, k: (layer, k, cb0 + j))],
        out_specs=[pl.BlockSpec((tm, tn), lambda j, i, k: (i, j)),
                   pl.BlockSpec((rs, tn), lambda j, i, k: (0, j))],
        scratch_shapes=scratch,
        compiler_params=_cparams("arbitrary", "arbitrary", "arbitrary"),
        name="matmul_relu2" if relu2 else "matmul",
    )(a, a_s, w3)


def _panel_matmul_kernel(relu2, n_panels, shift, a_ref, as_ref, *rest):
    if shift:
        wa_ref, wn_ref, o_ref, os_ref, wb_ref = rest
    else:
        wa_ref, o_ref, os_ref, wb_ref = rest
    jj = pl.program_id(0)
    i = pl.program_id(1)
    rows = wa_ref.shape[0]

    def finish(v):
        return jnp.square(jnp.maximum(v, 0.0)) if relu2 else v

    @pl.when(jj < n_panels)
    def _():
        w = wa_ref[...]
        if shift:
            w = jnp.concatenate([w[:, shift:], wn_ref[:, :shift]], axis=1)
        wb_ref[jj % 2, pl.ds(pl.multiple_of(i * rows, rows), rows), :] = w.astype(BF16)

    @pl.when(jj > 0)
    def _():
        w = wb_ref[(jj + 1) % 2]
        o_ref[...] = finish(jnp.dot(a_ref[...], w, preferred_element_type=F32)).astype(o_ref.dtype)

        @pl.when(i == 0)
        def _():
            os_ref[...] = finish(jnp.dot(as_ref[...], w, preferred_element_type=F32)).astype(os_ref.dtype)


def _panel_matmul(a, a_s, w3, layer, col0, n, out_dtype=F32, relu2=False, tm=1024, tn=1024):
    m, kdim = a.shape
    rs = a_s.shape[0]
    n_i = m // tm
    n_panels = n // tn
    assert n % tn == 0 and m % tm == 0 and kdim % n_i == 0
    rows = kdim // n_i
    cb0, shift = divmod(col0, tn)

    def w_spec(extra):
        def index(jj, i):
            last = jj == n_panels
            return (layer, jnp.where(last, n_i - 1, i), cb0 + extra + jnp.minimum(jj, n_panels - 1))
        return pl.BlockSpec((None, rows, tn), index)

    def row_block(jj, i):
        return jnp.where(jj == 0, 0, i)

    w_specs = [w_spec(0), w_spec(1)] if shift else [w_spec(0)]
    return pl.pallas_call(
        functools.partial(_panel_matmul_kernel, relu2, n_panels, shift),
        out_shape=[jax.ShapeDtypeStruct((m, n), out_dtype), jax.ShapeDtypeStruct((rs, n), out_dtype)],
        grid=(n_panels + 1, n_i),
        in_specs=[pl.BlockSpec((tm, kdim), lambda jj, i: (row_block(jj, i), 0)),
                  pl.BlockSpec((rs, kdim), lambda jj, i: (0, 0))] + w_specs,
        out_specs=[pl.BlockSpec((tm, tn), lambda jj, i: (row_block(jj, i), jnp.maximum(jj - 1, 0))),
                   pl.BlockSpec((rs, tn), lambda jj, i: (0, jnp.maximum(jj - 1, 0)))],
        scratch_shapes=[pltpu.VMEM((2, kdim, tn), BF16)],
        compiler_params=_cparams("arbitrary", "arbitrary"),
        name="panel_matmul_relu2" if relu2 else "panel_matmul",
    )(a, a_s, *([w3] * len(w_specs)))


def _gates_kernel(n_heads, zg_ref, alog_ref, dtb_ref, gc_ref, beta_ref):
    zg = zg_ref[...]
    rows = zg.shape[0]
    g = -jnp.exp(alog_ref[...]) * _softplus(zg[:, :n_heads] + dtb_ref[...])
    beta = _sigmoid(zg[:, n_heads:])
    pos = lax.broadcasted_iota(jnp.int32, (rows, HEAD_DIM), 0) % DELTA_CHUNK
    for h in range(n_heads):
        gh = jnp.broadcast_to(g[:, h:h + 1], (rows, HEAD_DIM))
        s = 1
        while s < DELTA_CHUNK:
            gh = gh + jnp.where(pos >= s, pltpu.roll(gh, s, 0), 0.0)
            s *= 2
        gc_ref[h] = gh
        beta_ref[h] = jnp.broadcast_to(beta[:, h:h + 1], (rows, HEAD_DIM))


def _gates(zg, a_log, dt_bias, rows=256):
    b, t, h2 = zg.shape
    nh = h2 // 2
    out = jax.ShapeDtypeStruct((b, nh, t, HEAD_DIM), F32)
    ospec = pl.BlockSpec((None, nh, rows, HEAD_DIM), lambda a, i: (a, 0, i, 0))
    vec = pl.BlockSpec((1, nh), lambda a, i: (0, 0))
    return pl.pallas_call(
        functools.partial(_gates_kernel, nh),
        out_shape=[out, out],
        grid=(b, t // rows),
        in_specs=[pl.BlockSpec((None, rows, h2), lambda a, i: (a, i, 0)), vec, vec],
        out_specs=[ospec, ospec],
        compiler_params=_cparams("arbitrary", "arbitrary"),
        name="delta_gates",
    )(zg, a_log.reshape(1, nh), dt_bias.reshape(1, nh))


def _delta_prompt_kernel(nh, q_ref, k_ref, v_ref, za_ref, cq_ref, ck_ref, cv_ref, gc_ref, beta_ref, norm_ref,
                         o_ref, s_ref, wq_ref, ka_ref, u_ref, eg_ref):
    t = q_ref.shape[0]
    sb = DELTA_SUPER
    c = DELTA_CHUNK
    n_sb = t // sb
    cps = sb // c
    half = sb // 2
    ii = lax.broadcasted_iota(jnp.int32, (sb, sb), 0)
    jj = lax.broadcasted_iota(jnp.int32, (sb, sb), 1)
    lower = ((ii // c) == (jj // c)) & (ii >= jj)
    strict = ii > jj
    row = lax.broadcasted_iota(jnp.int32, (sb, HEAD_DIM), 0)
    pair_chunk = lax.broadcasted_iota(jnp.int32, (HEAD_DIM, 2 * c), 1) // c
    blk_xor = ii ^ jj
    base = DELTA_BASE
    base_levels = int(math.log2(base))

    def l2norm(x):
        return x * lax.rsqrt(jnp.sum(x * x, axis=-1, keepdims=True) + 1e-6)

    def prepare(n, hh):
        static = isinstance(n, int)
        r0 = n * sb if static else pl.multiple_of(n * sb, sb)
        rows = pl.ds(r0, sb)
        lanes = slice(hh * HEAD_DIM, (hh + 1) * HEAD_DIM)

        def conv_silu(x_ref, w_ref):
            x = x_ref[rows, lanes]
            w = w_ref[:, lanes]
            if static and n == 0:
                prev = jnp.zeros((8, HEAD_DIM), F32)
            else:
                prev = x_ref[pl.ds(r0 - 8 if static else pl.multiple_of(r0 - 8, 8), 8), lanes]
            y = x * w[CONV_W - 1:CONV_W]
            for s in range(1, CONV_W):
                head = jnp.broadcast_to(pltpu.roll(prev, s, 0)[None], (sb // 8, 8, HEAD_DIM))
                xs = jnp.where(row < s, head.reshape(sb, HEAD_DIM), pltpu.roll(x, s, 0))
                y = y + xs * w[CONV_W - 1 - s:CONV_W - s]
            return _silu(y)

        q = l2norm(conv_silu(q_ref, cq_ref)) * (HEAD_DIM ** -0.5)
        k = l2norm(conv_silu(k_ref, ck_ref))
        v = conv_silu(v_ref, cv_ref)
        gc = gc_ref[hh, rows]
        beta = beta_ref[hh, rows]
        kb = k * beta
        eg = jnp.exp(gc)
        gct = jnp.concatenate([gc[:half].T, gc[half:].T], axis=1)
        gc_j = jnp.broadcast_to(gct[0:1], (sb, sb))
        gc_i = jnp.concatenate([gc] * (sb // HEAD_DIM), axis=1)
        dec = jnp.exp(jnp.where(lower, gc_i - gc_j, -jnp.inf))
        kq = _dot_nt(jnp.concatenate([kb, q], axis=0), k)
        yield
        mm = jnp.where(strict, kq[:sb] * dec, 0.0)
        attn = kq[sb:] * dec
        x = jnp.where(blk_xor < base, -mm, 0.0)
        p = jnp.where(ii == jj, 1.0, x)
        xb = x.astype(BF16)
        x = jnp.dot(xb, xb, preferred_element_type=F32)
        yield
        for lvl in range(1, base_levels):
            xb = x.astype(BF16)
            if lvl < base_levels - 1:
                y = jnp.dot(jnp.concatenate([p.astype(BF16), xb], axis=0), xb, preferred_element_type=F32)
                yield
                p = p + y[:sb]
                x = y[sb:]
            else:
                y = jnp.dot(p.astype(BF16), xb, preferred_element_type=F32)
                yield
                p = p + y
        size = base
        while size < c:
            pb = p.astype(BF16)
            off = jnp.where((blk_xor >= size) & (blk_xor < 2 * size), mm, 0.0).astype(BF16)
            e = jnp.dot(off, pb, preferred_element_type=F32)
            yield
            y = jnp.dot(pb, e.astype(BF16), preferred_element_type=F32)
            yield
            p = p - y
            size *= 2
        rhs = jnp.concatenate([v * beta, kb * eg], axis=1)
        pb = p.astype(BF16)
        uw = jnp.dot(pb, rhs.astype(BF16), preferred_element_type=F32)
        yield
        mm_hi = mm.astype(BF16)
        mm_lo = (mm - mm_hi.astype(F32)).astype(BF16)
        uw_hi = uw.astype(BF16)
        uw_lo = (uw - uw_hi.astype(F32)).astype(BF16)
        resid = rhs - uw - (jnp.dot(mm_hi, uw_hi, preferred_element_type=F32)
                            + jnp.dot(mm_hi, uw_lo, preferred_element_type=F32)
                            + jnp.dot(mm_lo, uw_hi, preferred_element_type=F32))
        yield
        uw = uw + jnp.dot(pb, resid.astype(BF16), preferred_element_type=F32)
        yield
        u_ref[hh, rows] = uw[:, :HEAD_DIM]
        w = uw[:, HEAD_DIM:]
        qd = q * eg
        gl = jnp.concatenate([jnp.broadcast_to(gc[ci * c + c - 1:ci * c + c], (c, HEAD_DIM))
                              for ci in range(cps)], axis=0)
        kd = k * jnp.exp(gl - gc)
        kdt = jnp.concatenate([kd[:half].T, kd[half:].T], axis=1)
        for ci in range(cps):
            cc = n * cps + ci
            cr = slice(ci * c, (ci + 1) * c)
            pair = slice((ci // 2) * 2 * c, (ci // 2 + 1) * 2 * c)
            wq_ref[hh, cc] = jnp.concatenate([w[cr], qd[cr]], axis=0).astype(BF16)
            ka_ref[hh, cc] = jnp.concatenate([jnp.where(pair_chunk == ci % 2, kdt[:, pair], 0.0),
                                              attn[cr, pair]], axis=0).astype(BF16)
            eg_ref[hh, cc] = jnp.exp(gl[ci * c:ci * c + 8])

    def recur(n, states):
        static = isinstance(n, int)
        for ci in range(cps):
            cc = n * cps + ci
            crow = pl.ds(cc * c if static else pl.multiple_of(cc * c, c), c)
            r1 = [jnp.dot(wq_ref[hh, cc], states[hh].astype(BF16), preferred_element_type=F32) for hh in range(nh)]
            yield
            r2 = []
            for hh in range(nh):
                v_new = (u_ref[hh, crow] - r1[hh][:c]).astype(BF16)
                r2.append(jnp.dot(ka_ref[hh, cc], jnp.concatenate([v_new, v_new], axis=0),
                                  preferred_element_type=F32))
            yield
            for hh in range(nh):
                lanes = slice(hh * HEAD_DIM, (hh + 1) * HEAD_DIM)
                o = r1[hh][c:] + r2[hh][HEAD_DIM:]
                states[hh] = states[hh] * eg_ref[hh, cc][0:1] + r2[hh][:HEAD_DIM]
                o_ref[crow, lanes] = (_rms_rows(o) * norm_ref[...] * _silu(za_ref[crow, lanes])).astype(o_ref.dtype)

    def interleave(gens):
        live = list(gens)
        while live:
            live = [g for g in live if next(g, live) is not live]

    def chain(*gens):
        for g in gens:
            yield from g

    bpi = 2
    assert n_sb % bpi == 0

    def body(it, states):
        states = list(states)
        first = it * bpi
        interleave([prepare(first + r, hh) for r in range(bpi) for hh in range(nh)]
                   + [chain(*[recur(first - bpi + r, states) for r in range(bpi)])])
        return tuple(states)

    interleave([prepare(r, hh) for r in range(bpi) for hh in range(nh)])
    states = lax.fori_loop(1, n_sb // bpi, body, tuple(jnp.zeros((HEAD_DIM, HEAD_DIM), F32) for _ in range(nh)))
    states = list(states)
    interleave([chain(*[recur(n_sb - bpi + r, states) for r in range(bpi)])])
    for hh in range(nh):
        s_ref[hh] = states[hh]


def _delta_prompt(z_main, conv_w, gc, beta, norm_w, n_heads, mix_width, hpb=2):
    b, t, _ = z_main.shape
    wl = hpb * HEAD_DIM
    ng = n_heads // hpb
    n_chunks = t // DELTA_CHUNK
    col = lambda off: pl.BlockSpec((None, t, wl), lambda a, h: (a, 0, off + h))
    cw = lambda off: pl.BlockSpec((CONV_W, wl), lambda a, h: (0, off + h))
    gate = pl.BlockSpec((None, hpb, t, HEAD_DIM), lambda a, h: (a, h, 0, 0))
    return pl.pallas_call(
        functools.partial(_delta_prompt_kernel, hpb),
        out_shape=[jax.ShapeDtypeStruct((b, t, mix_width), BF16),
                   jax.ShapeDtypeStruct((b, n_heads, HEAD_DIM, HEAD_DIM), F32)],
        grid=(b, ng),
        in_specs=[col(0), col(ng), col(2 * ng), col(3 * ng),
                  cw(0), cw(ng), cw(2 * ng), gate, gate,
                  pl.BlockSpec((1, HEAD_DIM), lambda a, h: (0, 0))],
        out_specs=[pl.BlockSpec((None, t, wl), lambda a, h: (a, 0, h)),
                   pl.BlockSpec((None, hpb, HEAD_DIM, HEAD_DIM), lambda a, h: (a, h, 0, 0))],
        scratch_shapes=[pltpu.VMEM((hpb, n_chunks, 2 * DELTA_CHUNK, HEAD_DIM), BF16),
                        pltpu.VMEM((hpb, n_chunks, HEAD_DIM + DELTA_CHUNK, 2 * DELTA_CHUNK), BF16),
                        pltpu.VMEM((hpb, t, HEAD_DIM), F32),
                        pltpu.VMEM((hpb, n_chunks, 8, HEAD_DIM), F32)],
        compiler_params=_cparams("arbitrary", "arbitrary"),
        name="delta_prompt",
    )(z_main, z_main, z_main, z_main, conv_w, conv_w, conv_w, gc, beta, norm_w.reshape(1, HEAD_DIM))


def _xpos(x, cos, sin_e, sin_o):
    return x * cos + pltpu.roll(x, HEAD_DIM - 1, 1) * sin_e + pltpu.roll(x, 1, 1) * sin_o


def _log_gamma(h):
    return jnp.log1p(-jnp.exp2(-5.0 - h))


def _ret_prompt_kernel(q_ref, k_ref, v_ref, g_ref, cos_ref, sine_ref, sino_ref, mix_ref, o_ref, s_ref,
                       qs_ref, ks_ref):
    del mix_ref
    t = q_ref.shape[0]
    c = RET_CHUNK
    cos, sin_e, sin_o = cos_ref[...], sine_ref[...], sino_ref[...]
    qs_ref[...] = _xpos(q_ref[...], cos, sin_e, sin_o)
    ks_ref[...] = _xpos(k_ref[...], cos, sin_e, sin_o) * (HEAD_DIM ** -0.5)

    h = jnp.full((1, 1), pl.program_id(1), jnp.int32).astype(F32)
    lg = _log_gamma(h)
    ii = lax.broadcasted_iota(jnp.int32, (c, c), 0)
    jj = lax.broadcasted_iota(jnp.int32, (c, c), 1)
    dec = jnp.exp(jnp.where(ii >= jj, (ii - jj).astype(F32) * lg, -jnp.inf))
    pos = lax.broadcasted_iota(jnp.int32, (c, HEAD_DIM), 0).astype(F32)
    e_in = jnp.exp((pos + 1.0) * lg)
    e_out = jnp.exp((c - 1.0 - pos) * lg)
    e_all = jnp.exp(c * lg)

    unroll = 2

    def body(n, s):
        rows = [pl.ds(pl.multiple_of((n * unroll + r) * c, c), c) for r in range(unroll)]
        q = [qs_ref[rw] for rw in rows]
        k = [ks_ref[rw] for rw in rows]
        v = [v_ref[rw].astype(BF16) for rw in rows]
        qk = [_dot_nt(q[r], k[r]) for r in range(unroll)]
        upd = [_dot((k[r] * e_out).T, v[r]) for r in range(unroll)]
        for r in range(unroll):
            o = _dot(q[r] * e_in, s) + _dot(qk[r] * dec, v[r])
            s = s * e_all + upd[r]
            o_ref[rows[r]] = (_rms_rows(o) * _silu(g_ref[rows[r]])).astype(o_ref.dtype)
        return s

    s_ref[...] = lax.fori_loop(0, t // (c * unroll), body, jnp.zeros((HEAD_DIM, HEAD_DIM), F32))


def _ret_prompt(z_tail, cos, sin_e, sin_o, mix, mix_col0, n_heads):
    b, t, _ = z_tail.shape
    c0 = mix_col0 // HEAD_DIM
    col = lambda off: pl.BlockSpec((None, t, HEAD_DIM), lambda a, h: (a, 0, off + h))
    tab = pl.BlockSpec((t, HEAD_DIM), lambda a, h: (0, 0))
    return pl.pallas_call(
        _ret_prompt_kernel,
        out_shape=[jax.ShapeDtypeStruct(mix.shape, mix.dtype),
                   jax.ShapeDtypeStruct((b, n_heads, HEAD_DIM, HEAD_DIM), F32)],
        grid=(b, n_heads),
        in_specs=[col(0), col(n_heads), col(2 * n_heads), col(3 * n_heads), tab, tab, tab,
                  pl.BlockSpec(memory_space=pl.ANY)],
        out_specs=[pl.BlockSpec((None, t, HEAD_DIM), lambda a, h: (a, 0, c0 + h)),
                   pl.BlockSpec((None, None, HEAD_DIM, HEAD_DIM), lambda a, h: (a, h, 0, 0))],
        scratch_shapes=[pltpu.VMEM((t, HEAD_DIM), F32)] * 2,
        input_output_aliases={7: 0},
        compiler_params=_cparams("arbitrary", "arbitrary"),
        name="ret_prompt",
    )(z_tail, z_tail, z_tail, z_tail, cos, sin_e, sin_o, mix)


def _lambda(lq, lam_init):
    l01 = jnp.sum(lq[0:1] * lq[1:2], axis=-1, keepdims=True)
    l23 = jnp.sum(lq[2:3] * lq[3:4], axis=-1, keepdims=True)
    return jnp.exp(l01) - jnp.exp(l23) + lam_init


def _attn_prompt_kernel(lam_init, nh, tq, q_ref, k_ref, v_ref, lq_ref, sub_ref, mix_ref, o_ref,
                        q2_ref, s_ref, p_ref, m_ref, l_ref, acc_ref):
    del mix_ref
    i = pl.program_id(2)
    strip = 64
    lane = lax.broadcasted_iota(jnp.int32, (tq, HEAD_DIM), 1)
    scale = HALF_DIM ** -0.5
    for hh in range(nh):
        q = q_ref[:, hh * HEAD_DIM:(hh + 1) * HEAD_DIM]
        q2_ref[hh] = (jnp.concatenate([jnp.where(lane < HALF_DIM, q, 0.0), jnp.where(lane >= HALF_DIM, q, 0.0)],
                                      axis=0) * scale).astype(BF16)
    m_ref[...] = jnp.full(m_ref.shape, -jnp.inf, F32)
    l_ref[...] = jnp.zeros_like(l_ref)
    acc_ref[...] = jnp.zeros_like(acc_ref)
    key_idx = lax.broadcasted_iota(jnp.int32, (strip, tq), 1)
    qry_idx = lax.broadcasted_iota(jnp.int32, (strip, tq), 0)

    def step(j, diagonal):
        rows = pl.ds(pl.multiple_of(j * tq, tq), tq)
        for hh in range(nh):
            s_ref[hh] = _dot_nt(q2_ref[hh], k_ref[rows, hh * HEAD_DIM:(hh + 1) * HEAD_DIM])
        for hh in range(nh):
            for c in range(2 * tq // strip):
                sr = slice(c * strip, (c + 1) * strip)
                s = s_ref[hh, sr]
                if diagonal:
                    q0 = (c * strip) % tq
                    s = jnp.where(key_idx <= qry_idx + q0, s, -jnp.inf)
                m_old = m_ref[hh, sr]
                m_new = jnp.maximum(m_old, jnp.max(s, axis=-1, keepdims=True))
                alpha = jnp.exp(m_old - m_new)
                p = jnp.exp(s - jnp.concatenate([m_new] * (tq // HEAD_DIM), axis=1))
                l_ref[hh, sr] = alpha * l_ref[hh, sr] + jnp.sum(p, axis=-1, keepdims=True)
                m_ref[hh, sr] = m_new
                p_ref[hh, sr] = p.astype(BF16)
                acc_ref[hh, sr] = alpha * acc_ref[hh, sr]
            acc_ref[hh] += jnp.dot(p_ref[hh], v_ref[rows, hh * HEAD_DIM:(hh + 1) * HEAD_DIM].astype(BF16),
                                   preferred_element_type=F32)

    def body(j, carry):
        step(j, False)
        return carry

    lax.fori_loop(0, i, body, 0)
    step(i, True)
    lam = _lambda(lq_ref[...], lam_init)
    for hh in range(nh):
        o2 = acc_ref[hh] / l_ref[hh]
        o = o2[:tq] - lam * o2[tq:]
        o_ref[:, hh * HEAD_DIM:(hh + 1) * HEAD_DIM] = (_rms_rows(o) * sub_ref[...] * (1.0 - lam_init)).astype(o_ref.dtype)


def _attn_prompt(z_tail, lambda_qk, subln, lam_init, mix, mix_col0, n_heads, col0, tq=256, hpb=2):
    b, t, _ = z_tail.shape
    wl = hpb * HEAD_DIM
    ng = n_heads // hpb
    c0 = col0 // hpb
    m0 = mix_col0 // wl
    return pl.pallas_call(
        functools.partial(_attn_prompt_kernel, lam_init, hpb, tq),
        out_shape=jax.ShapeDtypeStruct(mix.shape, mix.dtype),
        grid=(b, ng, t // tq),
        in_specs=[pl.BlockSpec((None, tq, wl), lambda a, h, i: (a, i, c0 + h)),
                  pl.BlockSpec((None, t, wl), lambda a, h, i: (a, 0, c0 + ng + h)),
                  pl.BlockSpec((None, t, wl), lambda a, h, i: (a, 0, c0 + 2 * ng + h)),
                  pl.BlockSpec((4, HALF_DIM), lambda a, h, i: (0, 0)),
                  pl.BlockSpec((1, HEAD_DIM), lambda a, h, i: (0, 0)),
                  pl.BlockSpec(memory_space=pl.ANY)],
        out_specs=pl.BlockSpec((None, tq, wl), lambda a, h, i: (a, i, m0 + h)),
        scratch_shapes=[pltpu.VMEM((hpb, 2 * tq, HEAD_DIM), BF16),
                        pltpu.VMEM((hpb, 2 * tq, tq), F32),
                        pltpu.VMEM((hpb, 2 * tq, tq), BF16),
                        pltpu.VMEM((hpb, 2 * tq, HEAD_DIM), F32),
                        pltpu.VMEM((hpb, 2 * tq, HEAD_DIM), F32),
                        pltpu.VMEM((hpb, 2 * tq, HEAD_DIM), F32)],
        input_output_aliases={5: 0},
        compiler_params=_cparams("arbitrary", "arbitrary", "arbitrary"),
        name="attn_prompt",
    )(z_tail, z_tail, z_tail, lambda_qk, subln.reshape(1, HEAD_DIM), mix)


def _lane_pick(x, h):
    lane = lax.broadcasted_iota(jnp.int32, x.shape, 1)
    return jnp.sum(jnp.where(lane == h, x, 0.0), axis=-1, keepdims=True)


def _row_to_col(x):
    ii = lax.broadcasted_iota(jnp.int32, (HEAD_DIM, HEAD_DIM), 0)
    jj = lax.broadcasted_iota(jnp.int32, (HEAD_DIM, HEAD_DIM), 1)
    return jnp.sum(jnp.where(ii == jj, jnp.broadcast_to(x, (HEAD_DIM, HEAD_DIM)), 0.0), axis=-1, keepdims=True)


def _rows8(*rows):
    pad = jnp.zeros((8 - len(rows), HEAD_DIM), F32)
    return jnp.concatenate(list(rows) + [pad], axis=0)


def _delta_sample_kernel(q_ref, k_ref, v_ref, za_ref, bq_ref, bk_ref, bv_ref, cq_ref, ck_ref, cv_ref,
                         zg_ref, alog_ref, dtb_ref, norm_ref, s_ref, o_ref, so_ref):
    h = pl.program_id(1)
    n_heads = alog_ref.shape[1]

    def conv_silu(x_ref, buf_ref, w_ref):
        w = w_ref[...]
        buf = buf_ref[...]
        y = x_ref[...] * w[CONV_W - 1:CONV_W]
        for j in range(CONV_W - 1):
            y = y + buf[j:j + 1] * w[j:j + 1]
        return _silu(y)

    def l2norm(x):
        return x * lax.rsqrt(jnp.sum(x * x, axis=-1, keepdims=True) + 1e-6)

    q = l2norm(conv_silu(q_ref, bq_ref, cq_ref)) * (HEAD_DIM ** -0.5)
    k = l2norm(conv_silu(k_ref, bk_ref, ck_ref))
    v = conv_silu(v_ref, bv_ref, cv_ref)
    zg = zg_ref[...]
    a_a = _lane_pick(zg, h)
    b_a = _lane_pick(zg, h + n_heads)
    g = -jnp.exp(_lane_pick(alog_ref[...], h)) * _softplus(a_a + _lane_pick(dtb_ref[...], h))
    beta = _sigmoid(b_a)
    eg = jnp.exp(g)
    s = s_ref[...]
    ks_qs = _dot(_rows8(k, q), s)
    v_new = beta * (v - eg * ks_qs[0:1])
    qk = jnp.sum(q * k, axis=-1, keepdims=True)
    o = eg * ks_qs[1:2] + qk * v_new
    so_ref[...] = s * eg + _row_to_col(k) * v_new
    o_ref[...] = (_rms_rows(o) * norm_ref[...] * _silu(za_ref[...])).astype(o_ref.dtype)


def _delta_sample(zs_main, zs_gate, conv_buf, conv_w, a_log, dt_bias, norm_w, state, layer, n_heads):
    nb = state.shape[0]
    col = lambda off: pl.BlockSpec((None, 1, HEAD_DIM), lambda a, h: (a, 0, off + h))
    buf = lambda off: pl.BlockSpec((None, None, CONV_W - 1, HEAD_DIM), lambda a, h: (a, layer, 0, off + h))
    cw = lambda off: pl.BlockSpec((CONV_W, HEAD_DIM), lambda a, h: (0, off + h))
    vec = pl.BlockSpec((1, n_heads), lambda a, h: (0, 0))
    return pl.pallas_call(
        _delta_sample_kernel,
        out_shape=[jax.ShapeDtypeStruct((nb, 1, n_heads * HEAD_DIM), BF16),
                   jax.ShapeDtypeStruct((nb, n_heads, HEAD_DIM, HEAD_DIM), F32)],
        grid=(nb, n_heads),
        in_specs=[col(0), col(n_heads), col(2 * n_heads), col(3 * n_heads),
                  buf(0), buf(n_heads), buf(2 * n_heads), cw(0), cw(n_heads), cw(2 * n_heads),
                  pl.BlockSpec((None, 1, 2 * n_heads), lambda a, h: (a, 0, 0)), vec, vec,
                  pl.BlockSpec((1, HEAD_DIM), lambda a, h: (0, 0)),
                  pl.BlockSpec((None, None, None, HEAD_DIM, HEAD_DIM), lambda a, h: (a, layer, h, 0, 0))],
        out_specs=[pl.BlockSpec((None, 1, HEAD_DIM), lambda a, h: (a, 0, h)),
                   pl.BlockSpec((None, None, HEAD_DIM, HEAD_DIM), lambda a, h: (a, h, 0, 0))],
        compiler_params=_cparams("arbitrary", "arbitrary"),
        name="delta_sample",
    )(zs_main, zs_main, zs_main, zs_main, conv_buf, conv_buf, conv_buf, conv_w, conv_w, conv_w,
      zs_gate, a_log.reshape(1, n_heads), dt_bias.reshape(1, n_heads), norm_w.reshape(1, HEAD_DIM), state)


def _ret_sample_kernel(q_ref, k_ref, v_ref, g_ref, cos_ref, sine_ref, sino_ref, s_ref, o_ref, so_ref):
    h = jnp.full((1, 1), pl.program_id(1), jnp.int32).astype(F32)
    gamma = jnp.exp(_log_gamma(h))
    cos, sin_e, sin_o = cos_ref[...], sine_ref[...], sino_ref[...]
    q = _xpos(q_ref[...], cos, sin_e, sin_o)
    k = _xpos(k_ref[...], cos, sin_e, sin_o) * (HEAD_DIM ** -0.5)
    v = v_ref[...]
    s = s_ref[...]
    qs = _dot(_rows8(q), s)
    qk = jnp.sum(q * k, axis=-1, keepdims=True)
    o = qk * v + gamma * qs[0:1]
    so_ref[...] = s * gamma + _row_to_col(k) * v
    o_ref[...] = (_rms_rows(o) * _silu(g_ref[...])).astype(o_ref.dtype)


def _ret_sample(zs_tail, cos, sin_e, sin_o, state, layer, n_heads):
    nb = state.shape[0]
    col = lambda off: pl.BlockSpec((None, 1, HEAD_DIM), lambda a, h: (a, 0, off + h))
    tab = pl.BlockSpec((1, HEAD_DIM), lambda a, h: (0, 0))
    return pl.pallas_call(
        _ret_sample_kernel,
        out_shape=[jax.ShapeDtypeStruct((nb, 1, n_heads * HEAD_DIM), BF16),
                   jax.ShapeDtypeStruct((nb, n_heads, HEAD_DIM, HEAD_DIM), F32)],
        grid=(nb, n_heads),
        in_specs=[col(0), col(n_heads), col(2 * n_heads), col(3 * n_heads), tab, tab, tab,
                  pl.BlockSpec((None, None, None, HEAD_DIM, HEAD_DIM), lambda a, h: (a, layer, h, 0, 0))],
        out_specs=[pl.BlockSpec((None, 1, HEAD_DIM), lambda a, h: (a, 0, h)),
                   pl.BlockSpec((None, None, HEAD_DIM, HEAD_DIM), lambda a, h: (a, h, 0, 0))],
        compiler_params=_cparams("arbitrary", "arbitrary"),
        name="ret_sample",
    )(zs_tail, zs_tail, zs_tail, zs_tail, cos, sin_e, sin_o, state)


def _attn_decode_kernel(lam_init, ppb, pt_ref, q_ref, kn_ref, vn_ref, *rest):
    kc_refs, vc_refs = rest[:ppb], rest[ppb:2 * ppb]
    lq_ref, sub_ref, o_ref, m_ref, l_ref, acc_ref, s_ref = rest[2 * ppb:]
    j = pl.program_id(1)
    n_steps = pl.num_programs(1)
    nh = q_ref.shape[0]
    scale = HALF_DIM ** -0.5
    di = lax.broadcasted_iota(jnp.int32, (HEAD_DIM, 2 * HEAD_DIM), 0)
    ci = lax.broadcasted_iota(jnp.int32, (HEAD_DIM, 2 * HEAD_DIM), 1)
    expand = ((di // HALF_DIM) == (ci // HEAD_DIM)).astype(BF16)
    q = q_ref[...] * (scale * math.log2(math.e))
    strip = 8
    n_strips = PAGE_SIZE // strip

    @pl.when(j == 0)
    def _():
        m_ref[...] = jnp.full(m_ref.shape, -jnp.inf, F32)
        l_ref[...] = jnp.zeros_like(l_ref)
        acc_ref[...] = jnp.zeros_like(acc_ref)

    low = lax.broadcasted_iota(jnp.int32, (nh, HEAD_DIM), 1) < HALF_DIM

    def swap_halves(x):
        return pltpu.roll(x.reshape(-1, HEAD_DIM), HALF_DIM, 1).reshape(x.shape)

    def scores(kt):
        n_tok = kt.shape[0]
        prod = (kt * q[None]).reshape(n_tok * nh, HEAD_DIM)
        s = jnp.dot(prod.astype(BF16), expand, preferred_element_type=F32).reshape(n_tok, nh, 2 * HEAD_DIM)
        return jnp.where(low[None], s[:, :, :HEAD_DIM], s[:, :, HEAD_DIM:])

    def weighted(p, vt):
        return jnp.concatenate([jnp.sum(p * vt, axis=0), jnp.sum(swap_halves(p) * vt, axis=0)], axis=-1)

    m, l, acc = m_ref[...], l_ref[...], acc_ref[...]
    s_ref[0] = scores(kc_refs[0][...])
    for pg in range(ppb):
        if pg + 1 < ppb:
            s_ref[pg + 1] = scores(kc_refs[pg + 1][...])
        m_new = m
        for i in range(n_strips):
            m_new = jnp.maximum(m_new, jnp.max(s_ref[pg, i * strip:(i + 1) * strip], axis=0))
        alpha = jnp.exp2(m - m_new)
        l, acc, m = alpha * l, jnp.concatenate([alpha, swap_halves(alpha)], axis=-1) * acc, m_new
        for i in range(n_strips):
            p = jnp.exp2(s_ref[pg, i * strip:(i + 1) * strip] - m[None])
            l = l + jnp.sum(p, axis=0)
            acc = acc + weighted(p, vc_refs[pg][i * strip:(i + 1) * strip])
    m_ref[...] = m
    l_ref[...] = l
    acc_ref[...] = acc

    @pl.when(j == n_steps - 1)
    def _():
        s = scores(kn_ref[...][None])[0]
        m_old = m_ref[...]
        m_new = jnp.maximum(m_old, s)
        alpha = jnp.exp2(m_old - m_new)
        p = jnp.exp2(s - m_new)
        l = alpha * l_ref[...] + p
        acc = (jnp.concatenate([alpha, swap_halves(alpha)], axis=-1) * acc_ref[...]
               + weighted(p[None], vn_ref[...][None]))
        direct, cross = acc[:, :HEAD_DIM], acc[:, HEAD_DIM:]
        l_sw = swap_halves(l)
        o1 = jnp.where(low, direct, cross) / jnp.where(low, l, l_sw)
        o2 = jnp.where(low, cross, direct) / jnp.where(low, l_sw, l)
        lam = _lambda(lq_ref[...], lam_init)
        o = o1 - lam * o2
        o_ref[...] = (_rms_rows(o) * sub_ref[...] * (1.0 - lam_init)).astype(o_ref.dtype)


def _attn_decode(q, k_new, v_new, cache_k, cache_v, page_table, lambda_qk, subln, lam_init, layer, ppb=8):
    nb, nh, _ = q.shape
    n_pages = page_table.shape[1]
    assert n_pages % ppb == 0
    tok = pl.BlockSpec((None, nh, HEAD_DIM), lambda b, j, pt: (b, 0, 0))

    def page(r):
        return pl.BlockSpec((None, None, PAGE_SIZE, nh, HEAD_DIM),
                            lambda b, j, pt: (pt[b, j * ppb + r], layer, 0, 0, 0))

    pages = [page(r) for r in range(ppb)]
    grid_spec = pltpu.PrefetchScalarGridSpec(
        num_scalar_prefetch=1,
        grid=(nb, n_pages // ppb),
        in_specs=[tok, tok, tok] + pages + pages +
                 [pl.BlockSpec((4, HALF_DIM), lambda b, j, pt: (0, 0)),
                  pl.BlockSpec((1, HEAD_DIM), lambda b, j, pt: (0, 0))],
        out_specs=tok,
        scratch_shapes=[pltpu.VMEM((nh, HEAD_DIM), F32),
                        pltpu.VMEM((nh, HEAD_DIM), F32),
                        pltpu.VMEM((nh, 2 * HEAD_DIM), F32),
                        pltpu.VMEM((ppb, PAGE_SIZE, nh, HEAD_DIM), F32)],
    )
    return pl.pallas_call(
        functools.partial(_attn_decode_kernel, lam_init, ppb),
        out_shape=jax.ShapeDtypeStruct((nb, nh, HEAD_DIM), BF16),
        grid_spec=grid_spec,
        compiler_params=_cparams("arbitrary", "arbitrary"),
        name="attn_decode",
    )(page_table, q, k_new, v_new, *([cache_k] * ppb), *([cache_v] * ppb), lambda_qk, subln.reshape(1, HEAD_DIM))


def _xpos_tables(pos):
    half = HEAD_DIM // 2
    angle = jnp.repeat(1.0 / (10000.0 ** jnp.linspace(0.0, 1.0, half, dtype=F32)), 2)
    ph = pos.astype(F32)[:, None] * angle
    sin, cos = jnp.sin(ph), jnp.cos(ph)
    even = (jnp.arange(HEAD_DIM) % 2) == 0
    return cos, jnp.where(even, -sin, 0.0), jnp.where(even, 0.0, sin)


def kernel(x_prompt, x_sample, state_conv_a, state_delta, state_ret, cache_k, cache_v, page_table, c_prompt, c_sample, w_ada, b_ada, w_in, conv_a, a_log, dt_bias, norm_a, lambda_qk, subln_c, w_out, ln1_g, ln1_b, w_up, w_down, ln2_g, ln2_b):
    bp, t, d = x_prompt.shape
    nb = x_sample.shape[0]
    depth = w_in.shape[0]
    h_a = state_delta.shape[2]
    h_b = state_ret.shape[2]
    h_c = cache_k.shape[3]
    a_w, b_w, c_w = h_a * HEAD_DIM, h_b * HEAD_DIM, h_c * HEAD_DIM
    n_main = 4 * a_w
    n_gate = 2 * h_a
    past_len = page_table.shape[1] * PAGE_SIZE
    alpha = (2 * depth) ** 0.25
    rs = SAMPLE_ROWS
    assert nb + bp <= rs and x_sample.shape[1] == 1

    c_all = jnp.concatenate([c_sample, c_prompt, jnp.zeros((rs - nb - bp, d), F32)], axis=0)
    mod = _ada(c_all, w_ada, b_ada)

    def mod_p(l, i):
        return mod[l, nb:nb + bp, None, i * d:(i + 1) * d]

    def mod_s(l, i):
        return mod[l][None, :, i * d:(i + 1) * d]

    cos_p, sine_p, sino_p = _xpos_tables(jnp.arange(t))
    cos_s, sine_s, sino_s = _xpos_tables(past_len + jnp.arange(1))

    xp = x_prompt
    xs = jnp.concatenate([x_sample.reshape(nb, d), jnp.zeros((rs - nb, d), F32)], axis=0)[None]
    hp = _modulate(xp, mod_p(0, 1), mod_p(0, 0), 256)
    hs = _modulate(xs, mod_s(0, 1), mod_s(0, 0), rs)

    w_in_b = w_in.astype(BF16)
    w_gate_b = w_in_b[:, :, n_main:n_main + n_gate]
    w_tail_b = w_in_b[:, :, n_main + n_gate:]
    w_down_b = w_down.astype(BF16)

    st_p, st_s = [], []
    for l in range(depth):
        lam_init = 0.8 - 0.6 * math.exp(-0.3 * l)
        hp2 = hp.reshape(bp * t, d)
        hs2 = hs.reshape(rs, d)
        zp_main, zs_main = _matmul(hp2, hs2, w_in_b, l, 0, n_main)
        zp_gate, zs_gate = _matmul(hp2, hs2, w_gate_b, l)
        zp_tail, zs_tail = _matmul(hp2, hs2, w_tail_b, l)
        zp_main = zp_main.reshape(bp, t, n_main)
        zp_tail = zp_tail.reshape(bp, t, -1)

        gc, beta = _gates(zp_gate.reshape(bp, t, n_gate), a_log[l], dt_bias[l])
        mix_p, delta_p = _delta_prompt(zp_main, conv_a[l], gc, beta, norm_a[l], h_a, a_w + b_w + c_w)
        mix_p, ret_p = _ret_prompt(zp_tail, cos_p, sine_p, sino_p, mix_p, a_w, h_b)
        mix_p = _attn_prompt(zp_tail, lambda_qk[l], subln_c[l], lam_init, mix_p, a_w + b_w, h_c, 4 * h_b)
        k_off = 4 * b_w + c_w
        conv_p = zp_main[:, t - (CONV_W - 1):, :3 * a_w]
        k_p = zp_tail[:, :, k_off:k_off + c_w]
        v_p = zp_tail[:, :, k_off + c_w:k_off + 2 * c_w]
        st_p.append((conv_p, delta_p, ret_p, k_p, v_p))

        zs_m = zs_main[:nb]
        zs_t = zs_tail[:nb]
        oa_s, delta_s = _delta_sample(zs_m[:, None], zs_gate[:nb, None], state_conv_a, conv_a[l], a_log[l],
                                      dt_bias[l], norm_a[l], state_delta, l, h_a)
        ob_s, ret_s = _ret_sample(zs_t[:, None], cos_s, sine_s, sino_s, state_ret, l, h_b)
        q_s = zs_t[:, 4 * b_w:4 * b_w + c_w].reshape(nb, h_c, HEAD_DIM)
        k_s = zs_t[:, k_off:k_off + c_w].reshape(nb, h_c, HEAD_DIM)
        v_s = zs_t[:, k_off + c_w:k_off + 2 * c_w].reshape(nb, h_c, HEAD_DIM)
        oc_s = _attn_decode(q_s, k_s, v_s, cache_k, cache_v, page_table, lambda_qk[l], subln_c[l], lam_init, l)
        conv_s = jnp.concatenate([state_conv_a[:, l, 1:], zs_m[:, None, :3 * a_w]], axis=1)
        st_s.append((conv_s, delta_s, ret_s, k_s[:, None], v_s[:, None]))

        mix_p = mix_p.reshape(bp * t, -1)
        mix_s = jnp.concatenate([oa_s.reshape(nb, a_w), ob_s.reshape(nb, b_w), oc_s.reshape(nb, c_w)], axis=-1)
        mix_s = jnp.concatenate([mix_s, jnp.zeros((rs - nb, mix_s.shape[1]), BF16)], axis=0)
        yp, ys = _panel_matmul(mix_p, mix_s, w_out, l, 0, d)
        xp, hp = _postnorm(xp, yp.reshape(bp, t, d), mod_p(l, 2), ln1_g[l], ln1_b[l], mod_p(l, 4), mod_p(l, 3),
                           alpha, 256)
        xs, hs = _postnorm(xs, ys[None], mod_s(l, 2), ln1_g[l], ln1_b[l], mod_s(l, 4), mod_s(l, 3), alpha, rs)

        up, us = _panel_matmul(hp.reshape(bp * t, d), hs.reshape(rs, d), w_up, l, 0, w_up.shape[2],
                               out_dtype=BF16, relu2=True)
        fp, fs = _matmul(up, us, w_down_b, l)
        last = l == depth - 1
        nl = min(l + 1, depth - 1)
        xp, hp = _postnorm(xp, fp.reshape(bp, t, d), mod_p(l, 5), ln2_g[l], ln2_b[l], mod_p(nl, 1), mod_p(nl, 0),
                           alpha, 256, with_h=not last)
        xs, hs = _postnorm(xs, fs[None], mod_s(l, 5), ln2_g[l], ln2_b[l], mod_s(nl, 1), mod_s(nl, 0), alpha, rs,
                           with_h=not last)

    outs_p = [jnp.stack([s[i] for s in st_p], axis=1) for i in range(5)]
    outs_p[3:] = [o.reshape(bp, depth, t, h_c, HEAD_DIM) for o in outs_p[3:]]
    outs_p = tuple(outs_p)
    outs_s = tuple(jnp.stack([s[i] for s in st_s], axis=1) for i in range(5))
    y_sample = xs[0, :nb].reshape(nb, 1, d)
    return (xp, y_sample) + outs_p + outs_s
```

```python
import functools
import math

import jax
import jax.numpy as jnp
from jax import lax
from jax.experimental import pallas as pl
from jax.experimental.pallas import tpu as pltpu

F32 = jnp.float32
BF16 = jnp.bfloat16

HEAD_DIM = 128
HALF_DIM = HEAD_DIM // 2
CONV_W = 4
DELTA_CHUNK = 64
DELTA_SUPER = 256
DELTA_BASE = 4
RET_CHUNK = 128
PAGE_SIZE = 128
EPS = 1e-5
SAMPLE_ROWS = 16
VMEM_LIMIT = 58 * 1024 * 1024


def _cparams(*sem):
    return pltpu.CompilerParams(dimension_semantics=sem, vmem_limit_bytes=VMEM_LIMIT)


def _sigmoid(x):
    return 1.0 / (1.0 + jnp.exp(-x))


def _silu(x):
    return x * _sigmoid(x)


def _softplus(x):
    return jnp.maximum(x, 0.0) + jnp.log1p(jnp.exp(-jnp.abs(x)))


def _dot(a, b):
    return jnp.dot(a.astype(BF16), b.astype(BF16), preferred_element_type=F32)


def _dot_nt(a, b):
    return lax.dot_general(a.astype(BF16), b.astype(BF16), (((1,), (1,)), ((), ())),
                           preferred_element_type=F32)


def _rms_rows(x):
    return x * lax.rsqrt(jnp.mean(x * x, axis=-1, keepdims=True) + EPS)


def _ada_kernel(c_ref, w0_ref, w1_ref, b_ref, o_ref):
    c = _silu(c_ref[...])
    dh = w0_ref.shape[0]
    o_ref[...] = _dot(c[:, :dh], w0_ref[...]) + _dot(c[:, dh:], w1_ref[...]) + b_ref[...]


def _ada(c_all, w_ada, b_ada, tn=1024):
    depth, d, n = w_ada.shape
    rows = c_all.shape[0]
    return pl.pallas_call(
        _ada_kernel,
        out_shape=jax.ShapeDtypeStruct((depth, rows, n), F32),
        grid=(depth, n // tn),
        in_specs=[pl.BlockSpec((rows, d), lambda l, j: (0, 0)),
                  pl.BlockSpec((None, d // 2, tn), lambda l, j: (l, 0, j)),
                  pl.BlockSpec((None, d // 2, tn), lambda l, j: (l, 1, j)),
                  pl.BlockSpec((None, 1, tn), lambda l, j: (l, 0, j))],
        out_specs=pl.BlockSpec((None, rows, tn), lambda l, j: (l, 0, j)),
        compiler_params=_cparams("arbitrary", "arbitrary"),
        name="ada_mod",
    )(c_all, w_ada, w_ada, b_ada.reshape(depth, 1, n))


def _modulate_kernel(x_ref, sc_ref, sh_ref, o_ref):
    o_ref[...] = (x_ref[...] * (1.0 + sc_ref[...]) + sh_ref[...]).astype(o_ref.dtype)


def _modulate(x, sc, sh, tm):
    g, r, d = x.shape
    rm = sc.shape[1]
    mspec = pl.BlockSpec((None, rm, d), lambda a, i: (a, 0, 0))
    return pl.pallas_call(
        _modulate_kernel,
        out_shape=jax.ShapeDtypeStruct((g, r, d), BF16),
        grid=(g, r // tm),
        in_specs=[pl.BlockSpec((None, tm, d), lambda a, i: (a, i, 0)), mspec, mspec],
        out_specs=pl.BlockSpec((None, tm, d), lambda a, i: (a, i, 0)),
        compiler_params=_cparams("arbitrary", "arbitrary"),
        name="modulate",
    )(x, sc, sh)


def _postnorm_kernel(alpha, with_h, x_ref, y_ref, gate_ref, lg_ref, lb_ref, sc_ref, sh_ref, xo_ref, *h_ref):
    v = alpha * x_ref[...] + (1.0 + gate_ref[...]) * y_ref[...]
    vc = v - jnp.mean(v, axis=-1, keepdims=True)
    var = jnp.mean(vc * vc, axis=-1, keepdims=True)
    xn = vc * lax.rsqrt(var + EPS) * lg_ref[...] + lb_ref[...]
    xo_ref[...] = xn
    if with_h:
        h_ref[0][...] = (xn * (1.0 + sc_ref[...]) + sh_ref[...]).astype(BF16)


def _postnorm(x, y, gate, ln_g, ln_b, sc, sh, alpha, tm, with_h=True):
    g, r, d = x.shape
    rm = gate.shape[1]
    row = pl.BlockSpec((None, tm, d), lambda a, i: (a, i, 0))
    mspec = pl.BlockSpec((None, rm, d), lambda a, i: (a, 0, 0))
    vec = pl.BlockSpec((1, d), lambda a, i: (0, 0))
    out_shape = [jax.ShapeDtypeStruct((g, r, d), F32)]
    out_specs = [row]
    if with_h:
        out_shape.append(jax.ShapeDtypeStruct((g, r, d), BF16))
        out_specs.append(row)
    res = pl.pallas_call(
        functools.partial(_postnorm_kernel, alpha, with_h),
        out_shape=out_shape,
        grid=(g, r // tm),
        in_specs=[row, row, mspec, vec, vec, mspec, mspec],
        out_specs=out_specs,
        compiler_params=_cparams("arbitrary", "arbitrary"),
        name="postnorm",
    )(x, y, gate, ln_g.reshape(1, d), ln_b.reshape(1, d), sc, sh)
    return (res[0], res[1]) if with_h else (res[0], None)


def _matmul_kernel(relu2, nk, a_ref, as_ref, w_ref, o_ref, os_ref, *scratch):
    i = pl.program_id(1)
    k = pl.program_id(2)

    def finish(v):
        return jnp.square(jnp.maximum(v, 0.0)) if relu2 else v

    if nk == 1:
        o_ref[...] = finish(jnp.dot(a_ref[...], w_ref[...], preferred_element_type=F32)).astype(o_ref.dtype)

        @pl.when(i == 0)
        def _():
            os_ref[...] = finish(jnp.dot(as_ref[...], w_ref[...], preferred_element_type=F32)).astype(os_ref.dtype)
    else:
        acc_ref, accs_ref = scratch

        @pl.when(k == 0)
        def _():
            acc_ref[...] = jnp.zeros_like(acc_ref)

        acc_ref[...] += jnp.dot(a_ref[...], w_ref[...], preferred_element_type=F32)

        @pl.when(k == nk - 1)
        def _():
            o_ref[...] = finish(acc_ref[...]).astype(o_ref.dtype)

        @pl.when(i == 0)
        def _():
            @pl.when(k == 0)
            def _():
                accs_ref[...] = jnp.zeros_like(accs_ref)

            accs_ref[...] += jnp.dot(as_ref[...], w_ref[...], preferred_element_type=F32)

            @pl.when(k == nk - 1)
            def _():
                os_ref[...] = finish(accs_ref[...]).astype(os_ref.dtype)


def _matmul(a, a_s, w3, layer, col0=0, n=None, out_dtype=F32, relu2=False, tm=1024, tn=1024, tk=4096):
    m, kdim = a.shape
    n = w3.shape[2] if n is None else n
    rs = a_s.shape[0]
    tn = min(tn, n)
    tk = min(tk, kdim)
    nk = kdim // tk
    assert n % tn == 0 and col0 % tn == 0 and m % tm == 0 and kdim % tk == 0
    cb0 = col0 // tn
    scratch = [] if nk == 1 else [pltpu.VMEM((tm, tn), F32), pltpu.VMEM((rs, tn), F32)]
    return pl.pallas_call(
        functools.partial(_matmul_kernel, relu2, nk),
        out_shape=[jax.ShapeDtypeStruct((m, n), out_dtype), jax.ShapeDtypeStruct((rs, n), out_dtype)],
        grid=(n // tn, m // tm, nk),
        in_specs=[pl.BlockSpec((tm, tk), lambda j, i, k: (i, k)),
                  pl.BlockSpec((rs, tk), lambda j, i, k: (0, k)),
                  pl.BlockSpec((None, tk, tn), lambda j, i, k: (layer, k, cb0 + j))],
        out_specs=[pl.BlockSpec((tm, tn), lambda j, i, k: (i, j)),
                   pl.BlockSpec((rs, tn), lambda j, i, k: (0, j))],
        scratch_shapes=scratch,
        compiler_params=_cparams("arbitrary", "arbitrary", "arbitrary"),
        name="matmul_relu2" if relu2 else "matmul",
    )(a, a_s, w3)


def _panel_matmul_kernel(relu2, n_panels, transposed, a_ref, as_ref, w_ref, o_ref, os_ref, wb_ref):
    jj = pl.program_id(0)
    i = pl.program_id(1)
    rows = w_ref.shape[0]

    def product(x, w):
        if transposed:
            v = lax.dot_general(x, w, (((1,), (1,)), ((), ())), preferred_element_type=F32)
        else:
            v = jnp.dot(x, w, preferred_element_type=F32)
        return jnp.square(jnp.maximum(v, 0.0)) if relu2 else v

    @pl.when(jj < n_panels)
    def _():
        wb_ref[jj % 2, pl.ds(pl.multiple_of(i * rows, rows), rows), :] = w_ref[...].astype(BF16)

    @pl.when(jj > 0)
    def _():
        w = wb_ref[(jj + 1) % 2]
        o_ref[...] = product(a_ref[...], w).astype(o_ref.dtype)

        @pl.when(i == 0)
        def _():
            os_ref[...] = product(as_ref[...], w).astype(os_ref.dtype)


def _panel_matmul(a, a_s, w3, layer, col0, n, transposed=False, out_dtype=F32, relu2=False, tm=1024, tn=1024):
    m, kdim = a.shape
    rs = a_s.shape[0]
    n_i = m // tm
    n_panels = n // tn
    assert n % tn == 0 and m % tm == 0

    def clamp(jj, i):
        last = jj == n_panels
        return jnp.minimum(jj, n_panels - 1), jnp.where(last, n_i - 1, i)

    if transposed:
        assert tn % (8 * n_i) == 0 and col0 % 8 == 0
        rows = tn // n_i

        def w_index(jj, i):
            panel, chunk = clamp(jj, i)
            return (layer, pl.multiple_of(col0 + panel * tn + chunk * rows, 8), 0)

        w_spec = pl.BlockSpec((None, pl.Element(rows), pl.Element(kdim)), w_index)
        wb_shape = (2, tn, kdim)
    else:
        assert kdim % (8 * n_i) == 0 and col0 % tn == 0
        rows = kdim // n_i

        def w_index(jj, i):
            panel, chunk = clamp(jj, i)
            return (layer, chunk, col0 // tn + panel)

        w_spec = pl.BlockSpec((None, rows, tn), w_index)
        wb_shape = (2, kdim, tn)

    def row_block(jj, i):
        return jnp.where(jj == 0, 0, i)

    return pl.pallas_call(
        functools.partial(_panel_matmul_kernel, relu2, n_panels, transposed),
        out_shape=[jax.ShapeDtypeStruct((m, n), out_dtype), jax.ShapeDtypeStruct((rs, n), out_dtype)],
        grid=(n_panels + 1, n_i),
        in_specs=[pl.BlockSpec((tm, kdim), lambda jj, i: (row_block(jj, i), 0)),
                  pl.BlockSpec((rs, kdim), lambda jj, i: (0, 0)), w_spec],
        out_specs=[pl.BlockSpec((tm, tn), lambda jj, i: (row_block(jj, i), jnp.maximum(jj - 1, 0))),
                   pl.BlockSpec((rs, tn), lambda jj, i: (0, jnp.maximum(jj - 1, 0)))],
        scratch_shapes=[pltpu.VMEM(wb_shape, BF16)],
        compiler_params=_cparams("arbitrary", "arbitrary"),
        name="panel_matmul_relu2" if relu2 else "panel_matmul",
    )(a, a_s, w3)


def _gates_kernel(n_heads, zg_ref, alog_ref, dtb_ref, gc_ref, beta_ref):
    zg = zg_ref[...]
    rows = zg.shape[0]
    g = -jnp.exp(alog_ref[...]) * _softplus(zg[:, :n_heads] + dtb_ref[...])
    beta = _sigmoid(zg[:, n_heads:])
    pos = lax.broadcasted_iota(jnp.int32, (rows, HEAD_DIM), 0) % DELTA_CHUNK
    for h in range(n_heads):
        gh = jnp.broadcast_to(g[:, h:h + 1], (rows, HEAD_DIM))
        s = 1
        while s < DELTA_CHUNK:
            gh = gh + jnp.where(pos >= s, pltpu.roll(gh, s, 0), 0.0)
            s *= 2
        gc_ref[h] = gh
        beta_ref[h] = jnp.broadcast_to(beta[:, h:h + 1], (rows, HEAD_DIM))


def _gates(zg, a_log, dt_bias, rows=256):
    b, t, h2 = zg.shape
    nh = h2 // 2
    out = jax.ShapeDtypeStruct((b, nh, t, HEAD_DIM), F32)
    ospec = pl.BlockSpec((None, nh, rows, HEAD_DIM), lambda a, i: (a, 0, i, 0))
    vec = pl.BlockSpec((1, nh), lambda a, i: (0, 0))
    return pl.pallas_call(
        functools.partial(_gates_kernel, nh),
        out_shape=[out, out],
        grid=(b, t // rows),
        in_specs=[pl.BlockSpec((None, rows, h2), lambda a, i: (a, i, 0)), vec, vec],
        out_specs=[ospec, ospec],
        compiler_params=_cparams("arbitrary", "arbitrary"),
        name="delta_gates",
    )(zg, a_log.reshape(1, nh), dt_bias.reshape(1, nh))


def _delta_prompt_kernel(nh, q_ref, k_ref, v_ref, za_ref, cq_ref, ck_ref, cv_ref, gc_ref, beta_ref, norm_ref,
                         o_ref, s_ref, wq_ref, ka_ref, u_ref, eg_ref):
    t = q_ref.shape[0]
    sb = DELTA_SUPER
    c = DELTA_CHUNK
    n_sb = t // sb
    cps = sb // c
    half = sb // 2
    ii = lax.broadcasted_iota(jnp.int32, (sb, sb), 0)
    jj = lax.broadcasted_iota(jnp.int32, (sb, sb), 1)
    lower = ((ii // c) == (jj // c)) & (ii >= jj)
    strict = ii > jj
    row = lax.broadcasted_iota(jnp.int32, (sb, HEAD_DIM), 0)
    pair_chunk = lax.broadcasted_iota(jnp.int32, (HEAD_DIM, 2 * c), 1) // c
    blk_xor = ii ^ jj
    base = DELTA_BASE
    base_levels = int(math.log2(base))

    def l2norm(x):
        return x * lax.rsqrt(jnp.sum(x * x, axis=-1, keepdims=True) + 1e-6)

    def prepare(n, hh):
        static = isinstance(n, int)
        r0 = n * sb if static else pl.multiple_of(n * sb, sb)
        rows = pl.ds(r0, sb)
        lanes = slice(hh * HEAD_DIM, (hh + 1) * HEAD_DIM)

        def conv_silu(x_ref, w_ref):
            x = x_ref[rows, lanes]
            w = w_ref[:, lanes]
            if static and n == 0:
                prev = jnp.zeros((8, HEAD_DIM), F32)
            else:
                prev = x_ref[pl.ds(r0 - 8 if static else pl.multiple_of(r0 - 8, 8), 8), lanes]
            y = x * w[CONV_W - 1:CONV_W]
            for s in range(1, CONV_W):
                head = jnp.broadcast_to(pltpu.roll(prev, s, 0)[None], (sb // 8, 8, HEAD_DIM))
                xs = jnp.where(row < s, head.reshape(sb, HEAD_DIM), pltpu.roll(x, s, 0))
                y = y + xs * w[CONV_W - 1 - s:CONV_W - s]
            return _silu(y)

        q = l2norm(conv_silu(q_ref, cq_ref)) * (HEAD_DIM ** -0.5)
        k = l2norm(conv_silu(k_ref, ck_ref))
        v = conv_silu(v_ref, cv_ref)
        gc = gc_ref[hh, rows]
        beta = beta_ref[hh, rows]
        kb = k * beta
        eg = jnp.exp(gc)
        gct = jnp.concatenate([gc[:half].T, gc[half:].T], axis=1)
        gc_j = jnp.broadcast_to(gct[0:1], (sb, sb))
        gc_i = jnp.concatenate([gc] * (sb // HEAD_DIM), axis=1)
        dec = jnp.exp(jnp.where(lower, gc_i - gc_j, -jnp.inf))
        kq = _dot_nt(jnp.concatenate([kb, q], axis=0), k)
        yield
        mm = jnp.where(strict, kq[:sb] * dec, 0.0)
        attn = kq[sb:] * dec
        x = jnp.where(blk_xor < base, -mm, 0.0)
        p = jnp.where(ii == jj, 1.0, x)
        xb = x.astype(BF16)
        x = jnp.dot(xb, xb, preferred_element_type=F32)
        yield
        for lvl in range(1, base_levels):
            xb = x.astype(BF16)
            if lvl < base_levels - 1:
                y = jnp.dot(jnp.concatenate([p.astype(BF16), xb], axis=0), xb, preferred_element_type=F32)
                yield
                p = p + y[:sb]
                x = y[sb:]
            else:
                y = jnp.dot(p.astype(BF16), xb, preferred_element_type=F32)
                yield
                p = p + y
        size = base
        while size < c:
            pb = p.astype(BF16)
            off = jnp.where((blk_xor >= size) & (blk_xor < 2 * size), mm, 0.0).astype(BF16)
            e = jnp.dot(off, pb, preferred_element_type=F32)
            yield
            y = jnp.dot(pb, e.astype(BF16), preferred_element_type=F32)
            yield
            p = p - y
            size *= 2
        rhs = jnp.concatenate([v * beta, kb * eg], axis=1)
        pb = p.astype(BF16)
        uw = jnp.dot(pb, rhs.astype(BF16), preferred_element_type=F32)
        yield
        mm_hi = mm.astype(BF16)
        mm_lo = (mm - mm_hi.astype(F32)).astype(BF16)
        uw_hi = uw.astype(BF16)
        uw_lo = (uw - uw_hi.astype(F32)).astype(BF16)
        resid = rhs - uw - (jnp.dot(mm_hi, uw_hi, preferred_element_type=F32)
                            + jnp.dot(mm_hi, uw_lo, preferred_element_type=F32)
                            + jnp.dot(mm_lo, uw_hi, preferred_element_type=F32))
        yield
        uw = uw + jnp.dot(pb, resid.astype(BF16), preferred_element_type=F32)
        yield
        u_ref[hh, rows] = uw[:, :HEAD_DIM]
        w = uw[:, HEAD_DIM:]
        qd = q * eg
        gl = jnp.concatenate([jnp.broadcast_to(gc[ci * c + c - 1:ci * c + c], (c, HEAD_DIM))
                              for ci in range(cps)], axis=0)
        kd = k * jnp.exp(gl - gc)
        kdt = jnp.concatenate([kd[:half].T, kd[half:].T], axis=1)
        for ci in range(cps):
            cc = n * cps + ci
            cr = slice(ci * c, (ci + 1) * c)
            pair = slice((ci // 2) * 2 * c, (ci // 2 + 1) * 2 * c)
            wq_ref[hh, cc] = jnp.concatenate([w[cr], qd[cr]], axis=0).astype(BF16)
            ka_ref[hh, cc] = jnp.concatenate([jnp.where(pair_chunk == ci % 2, kdt[:, pair], 0.0),
                                              attn[cr, pair]], axis=0).astype(BF16)
            eg_ref[hh, cc] = jnp.exp(gl[ci * c:ci * c + 8])

    def recur(n, states):
        static = isinstance(n, int)
        for ci in range(cps):
            cc = n * cps + ci
            crow = pl.ds(cc * c if static else pl.multiple_of(cc * c, c), c)
            r1 = [jnp.dot(wq_ref[hh, cc], states[hh].astype(BF16), preferred_element_type=F32) for hh in range(nh)]
            yield
            r2 = []
            for hh in range(nh):
                v_new = (u_ref[hh, crow] - r1[hh][:c]).astype(BF16)
                r2.append(jnp.dot(ka_ref[hh, cc], jnp.concatenate([v_new, v_new], axis=0),
                                  preferred_element_type=F32))
            yield
            for hh in range(nh):
                lanes = slice(hh * HEAD_DIM, (hh + 1) * HEAD_DIM)
                o = r1[hh][c:] + r2[hh][HEAD_DIM:]
                states[hh] = states[hh] * eg_ref[hh, cc][0:1] + r2[hh][:HEAD_DIM]
                o_ref[crow, lanes] = (_rms_rows(o) * norm_ref[...] * _silu(za_ref[crow, lanes])).astype(o_ref.dtype)

    def interleave(gens):
        live = list(gens)
        while live:
            live = [g for g in live if next(g, live) is not live]

    def chain(*gens):
        for g in gens:
            yield from g

    bpi = 2
    assert n_sb % bpi == 0

    def body(it, states):
        states = list(states)
        first = it * bpi
        interleave([prepare(first + r, hh) for r in range(bpi) for hh in range(nh)]
                   + [chain(*[recur(first - bpi + r, states) for r in range(bpi)])])
        return tuple(states)

    interleave([prepare(r, hh) for r in range(bpi) for hh in range(nh)])
    states = lax.fori_loop(1, n_sb // bpi, body, tuple(jnp.zeros((HEAD_DIM, HEAD_DIM), F32) for _ in range(nh)))
    states = list(states)
    interleave([chain(*[recur(n_sb - bpi + r, states) for r in range(bpi)])])
    for hh in range(nh):
        s_ref[hh] = states[hh]


def _delta_prompt(z_main, conv_w, gc, beta, norm_w, n_heads, mix_width, hpb=2):
    b, t, _ = z_main.shape
    wl = hpb * HEAD_DIM
    ng = n_heads // hpb
    n_chunks = t // DELTA_CHUNK
    col = lambda off: pl.BlockSpec((None, t, wl), lambda a, h: (a, 0, off + h))
    cw = lambda off: pl.BlockSpec((CONV_W, wl), lambda a, h: (0, off + h))
    gate = pl.BlockSpec((None, hpb, t, HEAD_DIM), lambda a, h: (a, h, 0, 0))
    return pl.pallas_call(
        functools.partial(_delta_prompt_kernel, hpb),
        out_shape=[jax.ShapeDtypeStruct((b, t, mix_width), BF16),
                   jax.ShapeDtypeStruct((b, n_heads, HEAD_DIM, HEAD_DIM), F32)],
        grid=(b, ng),
        in_specs=[col(0), col(ng), col(2 * ng), col(3 * ng),
                  cw(0), cw(ng), cw(2 * ng), gate, gate,
                  pl.BlockSpec((1, HEAD_DIM), lambda a, h: (0, 0))],
        out_specs=[pl.BlockSpec((None, t, wl), lambda a, h: (a, 0, h)),
                   pl.BlockSpec((None, hpb, HEAD_DIM, HEAD_DIM), lambda a, h: (a, h, 0, 0))],
        scratch_shapes=[pltpu.VMEM((hpb, n_chunks, 2 * DELTA_CHUNK, HEAD_DIM), BF16),
                        pltpu.VMEM((hpb, n_chunks, HEAD_DIM + DELTA_CHUNK, 2 * DELTA_CHUNK), BF16),
                        pltpu.VMEM((hpb, t, HEAD_DIM), F32),
                        pltpu.VMEM((hpb, n_chunks, 8, HEAD_DIM), F32)],
        compiler_params=_cparams("arbitrary", "arbitrary"),
        name="delta_prompt",
    )(z_main, z_main, z_main, z_main, conv_w, conv_w, conv_w, gc, beta, norm_w.reshape(1, HEAD_DIM))


def _xpos(x, cos, sin_e, sin_o):
    return x * cos + pltpu.roll(x, HEAD_DIM - 1, 1) * sin_e + pltpu.roll(x, 1, 1) * sin_o


def _log_gamma(h):
    return jnp.log1p(-jnp.exp2(-5.0 - h))


def _ret_prompt_kernel(q_ref, k_ref, v_ref, g_ref, cos_ref, sine_ref, sino_ref, mix_ref, o_ref, s_ref,
                       qs_ref, ks_ref):
    del mix_ref
    t = q_ref.shape[0]
    c = RET_CHUNK
    cos, sin_e, sin_o = cos_ref[...], sine_ref[...], sino_ref[...]
    qs_ref[...] = _xpos(q_ref[...], cos, sin_e, sin_o)
    ks_ref[...] = _xpos(k_ref[...], cos, sin_e, sin_o) * (HEAD_DIM ** -0.5)

    h = jnp.full((1, 1), pl.program_id(1), jnp.int32).astype(F32)
    lg = _log_gamma(h)
    ii = lax.broadcasted_iota(jnp.int32, (c, c), 0)
    jj = lax.broadcasted_iota(jnp.int32, (c, c), 1)
    dec = jnp.exp(jnp.where(ii >= jj, (ii - jj).astype(F32) * lg, -jnp.inf))
    pos = lax.broadcasted_iota(jnp.int32, (c, HEAD_DIM), 0).astype(F32)
    e_in = jnp.exp((pos + 1.0) * lg)
    e_out = jnp.exp((c - 1.0 - pos) * lg)
    e_all = jnp.exp(c * lg)

    unroll = 2

    def body(n, s):
        rows = [pl.ds(pl.multiple_of((n * unroll + r) * c, c), c) for r in range(unroll)]
        q = [qs_ref[rw] for rw in rows]
        k = [ks_ref[rw] for rw in rows]
        v = [v_ref[rw].astype(BF16) for rw in rows]
        qk = [_dot_nt(q[r], k[r]) for r in range(unroll)]
        upd = [_dot((k[r] * e_out).T, v[r]) for r in range(unroll)]
        for r in range(unroll):
            o = _dot(q[r] * e_in, s) + _dot(qk[r] * dec, v[r])
            s = s * e_all + upd[r]
            o_ref[rows[r]] = (_rms_rows(o) * _silu(g_ref[rows[r]])).astype(o_ref.dtype)
        return s

    s_ref[...] = lax.fori_loop(0, t // (c * unroll), body, jnp.zeros((HEAD_DIM, HEAD_DIM), F32))


def _ret_prompt(z_tail, cos, sin_e, sin_o, mix, mix_col0, n_heads):
    b, t, _ = z_tail.shape
    c0 = mix_col0 // HEAD_DIM
    col = lambda off: pl.BlockSpec((None, t, HEAD_DIM), lambda a, h: (a, 0, off + h))
    tab = pl.BlockSpec((t, HEAD_DIM), lambda a, h: (0, 0))
    return pl.pallas_call(
        _ret_prompt_kernel,
        out_shape=[jax.ShapeDtypeStruct(mix.shape, mix.dtype),
                   jax.ShapeDtypeStruct((b, n_heads, HEAD_DIM, HEAD_DIM), F32)],
        grid=(b, n_heads),
        in_specs=[col(0), col(n_heads), col(2 * n_heads), col(3 * n_heads), tab, tab, tab,
                  pl.BlockSpec(memory_space=pl.ANY)],
        out_specs=[pl.BlockSpec((None, t, HEAD_DIM), lambda a, h: (a, 0, c0 + h)),
                   pl.BlockSpec((None, None, HEAD_DIM, HEAD_DIM), lambda a, h: (a, h, 0, 0))],
        scratch_shapes=[pltpu.VMEM((t, HEAD_DIM), F32)] * 2,
        input_output_aliases={7: 0},
        compiler_params=_cparams("arbitrary", "arbitrary"),
        name="ret_prompt",
    )(z_tail, z_tail, z_tail, z_tail, cos, sin_e, sin_o, mix)


def _lambda(lq, lam_init):
    l01 = jnp.sum(lq[0:1] * lq[1:2], axis=-1, keepdims=True)
    l23 = jnp.sum(lq[2:3] * lq[3:4], axis=-1, keepdims=True)
    return jnp.exp(l01) - jnp.exp(l23) + lam_init


def _attn_prompt_kernel(lam_init, nh, tq, q_ref, k_ref, v_ref, lq_ref, sub_ref, mix_ref, o_ref,
                        q2_ref, s_ref, p_ref, m_ref, l_ref, acc_ref):
    del mix_ref
    i = pl.program_id(2)
    strip = 64
    lane = lax.broadcasted_iota(jnp.int32, (tq, HEAD_DIM), 1)
    scale = HALF_DIM ** -0.5
    for hh in range(nh):
        q = q_ref[:, hh * HEAD_DIM:(hh + 1) * HEAD_DIM]
        q2_ref[hh] = (jnp.concatenate([jnp.where(lane < HALF_DIM, q, 0.0), jnp.where(lane >= HALF_DIM, q, 0.0)],
                                      axis=0) * scale).astype(BF16)
    m_ref[...] = jnp.full(m_ref.shape, -jnp.inf, F32)
    l_ref[...] = jnp.zeros_like(l_ref)
    acc_ref[...] = jnp.zeros_like(acc_ref)
    key_idx = lax.broadcasted_iota(jnp.int32, (strip, tq), 1)
    qry_idx = lax.broadcasted_iota(jnp.int32, (strip, tq), 0)

    def step(j, diagonal):
        rows = pl.ds(pl.multiple_of(j * tq, tq), tq)
        for hh in range(nh):
            s_ref[hh] = _dot_nt(q2_ref[hh], k_ref[rows, hh * HEAD_DIM:(hh + 1) * HEAD_DIM])
        for hh in range(nh):
            for c in range(2 * tq // strip):
                sr = slice(c * strip, (c + 1) * strip)
                s = s_ref[hh, sr]
                if diagonal:
                    q0 = (c * strip) % tq
                    s = jnp.where(key_idx <= qry_idx + q0, s, -jnp.inf)
                m_old = m_ref[hh, sr]
                m_new = jnp.maximum(m_old, jnp.max(s, axis=-1, keepdims=True))
                alpha = jnp.exp(m_old - m_new)
                p = jnp.exp(s - jnp.concatenate([m_new] * (tq // HEAD_DIM), axis=1))
                l_ref[hh, sr] = alpha * l_ref[hh, sr] + jnp.sum(p, axis=-1, keepdims=True)
                m_ref[hh, sr] = m_new
                p_ref[hh, sr] = p.astype(BF16)
                acc_ref[hh, sr] = alpha * acc_ref[hh, sr]
            acc_ref[hh] += jnp.dot(p_ref[hh], v_ref[rows, hh * HEAD_DIM:(hh + 1) * HEAD_DIM].astype(BF16),
                                   preferred_element_type=F32)

    def body(j, carry):
        step(j, False)
        return carry

    lax.fori_loop(0, i, body, 0)
    step(i, True)
    lam = _lambda(lq_ref[...], lam_init)
    for hh in range(nh):
        o2 = acc_ref[hh] / l_ref[hh]
        o = o2[:tq] - lam * o2[tq:]
        o_ref[:, hh * HEAD_DIM:(hh + 1) * HEAD_DIM] = (_rms_rows(o) * sub_ref[...] * (1.0 - lam_init)).astype(o_ref.dtype)


def _attn_prompt(z_tail, lambda_qk, subln, lam_init, mix, mix_col0, n_heads, col0, tq=256, hpb=2):
    b, t, _ = z_tail.shape
    wl = hpb * HEAD_DIM
    ng = n_heads // hpb
    c0 = col0 // hpb
    m0 = mix_col0 // wl
    return pl.pallas_call(
        functools.partial(_attn_prompt_kernel, lam_init, hpb, tq),
        out_shape=jax.ShapeDtypeStruct(mix.shape, mix.dtype),
        grid=(b, ng, t // tq),
        in_specs=[pl.BlockSpec((None, tq, wl), lambda a, h, i: (a, i, c0 + h)),
                  pl.BlockSpec((None, t, wl), lambda a, h, i: (a, 0, c0 + ng + h)),
                  pl.BlockSpec((None, t, wl), lambda a, h, i: (a, 0, c0 + 2 * ng + h)),
                  pl.BlockSpec((4, HALF_DIM), lambda a, h, i: (0, 0)),
                  pl.BlockSpec((1, HEAD_DIM), lambda a, h, i: (0, 0)),
                  pl.BlockSpec(memory_space=pl.ANY)],
        out_specs=pl.BlockSpec((None, tq, wl), lambda a, h, i: (a, i, m0 + h)),
        scratch_shapes=[pltpu.VMEM((hpb, 2 * tq, HEAD_DIM), BF16),
                        pltpu.VMEM((hpb, 2 * tq, tq), F32),
                        pltpu.VMEM((hpb, 2 * tq, tq), BF16),
                        pltpu.VMEM((hpb, 2 * tq, HEAD_DIM), F32),
                        pltpu.VMEM((hpb, 2 * tq, HEAD_DIM), F32),
                        pltpu.VMEM((hpb, 2 * tq, HEAD_DIM), F32)],
        input_output_aliases={5: 0},
        compiler_params=_cparams("arbitrary", "arbitrary", "arbitrary"),
        name="attn_prompt",
    )(z_tail, z_tail, z_tail, lambda_qk, subln.reshape(1, HEAD_DIM), mix)


def _lane_pick(x, h):
    lane = lax.broadcasted_iota(jnp.int32, x.shape, 1)
    return jnp.sum(jnp.where(lane == h, x, 0.0), axis=-1, keepdims=True)


def _row_to_col(x):
    ii = lax.broadcasted_iota(jnp.int32, (HEAD_DIM, HEAD_DIM), 0)
    jj = lax.broadcasted_iota(jnp.int32, (HEAD_DIM, HEAD_DIM), 1)
    return jnp.sum(jnp.where(ii == jj, jnp.broadcast_to(x, (HEAD_DIM, HEAD_DIM)), 0.0), axis=-1, keepdims=True)


def _rows8(*rows):
    pad = jnp.zeros((8 - len(rows), HEAD_DIM), F32)
    return jnp.concatenate(list(rows) + [pad], axis=0)


def _delta_sample_kernel(q_ref, k_ref, v_ref, za_ref, bq_ref, bk_ref, bv_ref, cq_ref, ck_ref, cv_ref,
                         zg_ref, alog_ref, dtb_ref, norm_ref, s_ref, o_ref, so_ref):
    h = pl.program_id(1)
    n_heads = alog_ref.shape[1]

    def conv_silu(x_ref, buf_ref, w_ref):
        w = w_ref[...]
        buf = buf_ref[...]
        y = x_ref[...] * w[CONV_W - 1:CONV_W]
        for j in range(CONV_W - 1):
            y = y + buf[j:j + 1] * w[j:j + 1]
        return _silu(y)

    def l2norm(x):
        return x * lax.rsqrt(jnp.sum(x * x, axis=-1, keepdims=True) + 1e-6)

    q = l2norm(conv_silu(q_ref, bq_ref, cq_ref)) * (HEAD_DIM ** -0.5)
    k = l2norm(conv_silu(k_ref, bk_ref, ck_ref))
    v = conv_silu(v_ref, bv_ref, cv_ref)
    zg = zg_ref[...]
    a_a = _lane_pick(zg, h)
    b_a = _lane_pick(zg, h + n_heads)
    g = -jnp.exp(_lane_pick(alog_ref[...], h)) * _softplus(a_a + _lane_pick(dtb_ref[...], h))
    beta = _sigmoid(b_a)
    eg = jnp.exp(g)
    s = s_ref[...]
    ks_qs = _dot(_rows8(k, q), s)
    v_new = beta * (v - eg * ks_qs[0:1])
    qk = jnp.sum(q * k, axis=-1, keepdims=True)
    o = eg * ks_qs[1:2] + qk * v_new
    so_ref[...] = s * eg + _row_to_col(k) * v_new
    o_ref[...] = (_rms_rows(o) * norm_ref[...] * _silu(za_ref[...])).astype(o_ref.dtype)


def _delta_sample(zs_main, zs_gate, conv_buf, conv_w, a_log, dt_bias, norm_w, state, layer, n_heads):
    nb = state.shape[0]
    col = lambda off: pl.BlockSpec((None, 1, HEAD_DIM), lambda a, h: (a, 0, off + h))
    buf = lambda off: pl.BlockSpec((None, None, CONV_W - 1, HEAD_DIM), lambda a, h: (a, layer, 0, off + h))
    cw = lambda off: pl.BlockSpec((CONV_W, HEAD_DIM), lambda a, h: (0, off + h))
    vec = pl.BlockSpec((1, n_heads), lambda a, h: (0, 0))
    return pl.pallas_call(
        _delta_sample_kernel,
        out_shape=[jax.ShapeDtypeStruct((nb, 1, n_heads * HEAD_DIM), BF16),
                   jax.ShapeDtypeStruct((nb, n_heads, HEAD_DIM, HEAD_DIM), F32)],
        grid=(nb, n_heads),
        in_specs=[col(0), col(n_heads), col(2 * n_heads), col(3 * n_heads),
                  buf(0), buf(n_heads), buf(2 * n_heads), cw(0), cw(n_heads), cw(2 * n_heads),
                  pl.BlockSpec((None, 1, 2 * n_heads), lambda a, h: (a, 0, 0)), vec, vec,
                  pl.BlockSpec((1, HEAD_DIM), lambda a, h: (0, 0)),
                  pl.BlockSpec((None, None, None, HEAD_DIM, HEAD_DIM), lambda a, h: (a, layer, h, 0, 0))],
        out_specs=[pl.BlockSpec((None, 1, HEAD_DIM), lambda a, h: (a, 0, h)),
                   pl.BlockSpec((None, None, HEAD_DIM, HEAD_DIM), lambda a, h: (a, h, 0, 0))],
        compiler_params=_cparams("arbitrary", "arbitrary"),
        name="delta_sample",
    )(zs_main, zs_main, zs_main, zs_main, conv_buf, conv_buf, conv_buf, conv_w, conv_w, conv_w,
      zs_gate, a_log.reshape(1, n_heads), dt_bias.reshape(1, n_heads), norm_w.reshape(1, HEAD_DIM), state)


def _ret_sample_kernel(q_ref, k_ref, v_ref, g_ref, cos_ref, sine_ref, sino_ref, s_ref, o_ref, so_ref):
    h = jnp.full((1, 1), pl.program_id(1), jnp.int32).astype(F32)
    gamma = jnp.exp(_log_gamma(h))
    cos, sin_e, sin_o = cos_ref[...], sine_ref[...], sino_ref[...]
    q = _xpos(q_ref[...], cos, sin_e, sin_o)
    k = _xpos(k_ref[...], cos, sin_e, sin_o) * (HEAD_DIM ** -0.5)
    v = v_ref[...]
    s = s_ref[...]
    qs = _dot(_rows8(q), s)
    qk = jnp.sum(q * k, axis=-1, keepdims=True)
    o = qk * v + gamma * qs[0:1]
    so_ref[...] = s * gamma + _row_to_col(k) * v
    o_ref[...] = (_rms_rows(o) * _silu(g_ref[...])).astype(o_ref.dtype)


def _ret_sample(zs_tail, cos, sin_e, sin_o, state, layer, n_heads):
    nb = state.shape[0]
    col = lambda off: pl.BlockSpec((None, 1, HEAD_DIM), lambda a, h: (a, 0, off + h))
    tab = pl.BlockSpec((1, HEAD_DIM), lambda a, h: (0, 0))
    return pl.pallas_call(
        _ret_sample_kernel,
        out_shape=[jax.ShapeDtypeStruct((nb, 1, n_heads * HEAD_DIM), BF16),
                   jax.ShapeDtypeStruct((nb, n_heads, HEAD_DIM, HEAD_DIM), F32)],
        grid=(nb, n_heads),
        in_specs=[col(0), col(n_heads), col(2 * n_heads), col(3 * n_heads), tab, tab, tab,
                  pl.BlockSpec((None, None, None, HEAD_DIM, HEAD_DIM), lambda a, h: (a, layer, h, 0, 0))],
        out_specs=[pl.BlockSpec((None, 1, HEAD_DIM), lambda a, h: (a, 0, h)),
                   pl.BlockSpec((None, None, HEAD_DIM, HEAD_DIM), lambda a, h: (a, h, 0, 0))],
        compiler_params=_cparams("arbitrary", "arbitrary"),
        name="ret_sample",
    )(zs_tail, zs_tail, zs_tail, zs_tail, cos, sin_e, sin_o, state)


def _attn_decode_kernel(lam_init, ppb, pt_ref, q_ref, kn_ref, vn_ref, *rest):
    kc_refs, vc_refs = rest[:ppb], rest[ppb:2 * ppb]
    lq_ref, sub_ref, o_ref, m_ref, l_ref, acc_ref, s_ref = rest[2 * ppb:]
    j = pl.program_id(1)
    n_steps = pl.num_programs(1)
    nh = q_ref.shape[0]
    scale = HALF_DIM ** -0.5
    di = lax.broadcasted_iota(jnp.int32, (HEAD_DIM, 2 * HEAD_DIM), 0)
    ci = lax.broadcasted_iota(jnp.int32, (HEAD_DIM, 2 * HEAD_DIM), 1)
    expand = ((di // HALF_DIM) == (ci // HEAD_DIM)).astype(BF16)
    q = q_ref[...] * (scale * math.log2(math.e))
    strip = 8
    n_strips = PAGE_SIZE // strip

    @pl.when(j == 0)
    def _():
        m_ref[...] = jnp.full(m_ref.shape, -jnp.inf, F32)
        l_ref[...] = jnp.zeros_like(l_ref)
        acc_ref[...] = jnp.zeros_like(acc_ref)

    low = lax.broadcasted_iota(jnp.int32, (nh, HEAD_DIM), 1) < HALF_DIM

    def swap_halves(x):
        return pltpu.roll(x.reshape(-1, HEAD_DIM), HALF_DIM, 1).reshape(x.shape)

    def scores(kt):
        n_tok = kt.shape[0]
        prod = (kt * q[None]).reshape(n_tok * nh, HEAD_DIM)
        s = jnp.dot(prod.astype(BF16), expand, preferred_element_type=F32).reshape(n_tok, nh, 2 * HEAD_DIM)
        return jnp.where(low[None], s[:, :, :HEAD_DIM], s[:, :, HEAD_DIM:])

    def weighted(p, vt):
        return jnp.concatenate([jnp.sum(p * vt, axis=0), jnp.sum(swap_halves(p) * vt, axis=0)], axis=-1)

    m, l, acc = m_ref[...], l_ref[...], acc_ref[...]
    s_ref[0] = scores(kc_refs[0][...])
    for pg in range(ppb):
        if pg + 1 < ppb:
            s_ref[pg + 1] = scores(kc_refs[pg + 1][...])
        m_new = m
        for i in range(n_strips):
            m_new = jnp.maximum(m_new, jnp.max(s_ref[pg, i * strip:(i + 1) * strip], axis=0))
        alpha = jnp.exp2(m - m_new)
        l, acc, m = alpha * l, jnp.concatenate([alpha, swap_halves(alpha)], axis=-1) * acc, m_new
        for i in range(n_strips):
            p = jnp.exp2(s_ref[pg, i * strip:(i + 1) * strip] - m[None])
            l = l + jnp.sum(p, axis=0)
            acc = acc + weighted(p, vc_refs[pg][i * strip:(i + 1) * strip])
    m_ref[...] = m
    l_ref[...] = l
    acc_ref[...] = acc

    @pl.when(j == n_steps - 1)
    def _():
        s = scores(kn_ref[...][None])[0]
        m_old = m_ref[...]
        m_new = jnp.maximum(m_old, s)
        alpha = jnp.exp2(m_old - m_new)
        p = jnp.exp2(s - m_new)
        l = alpha * l_ref[...] + p
        acc = (jnp.concatenate([alpha, swap_halves(alpha)], axis=-1) * acc_ref[...]
               + weighted(p[None], vn_ref[...][None]))
        direct, cross = acc[:, :HEAD_DIM], acc[:, HEAD_DIM:]
        l_sw = swap_halves(l)
        o1 = jnp.where(low, direct, cross) / jnp.where(low, l, l_sw)
        o2 = jnp.where(low, cross, direct) / jnp.where(low, l_sw, l)
        lam = _lambda(lq_ref[...], lam_init)
        o = o1 - lam * o2
        o_ref[...] = (_rms_rows(o) * sub_ref[...] * (1.0 - lam_init)).astype(o_ref.dtype)


def _attn_decode(q, k_new, v_new, cache_k, cache_v, page_table, lambda_qk, subln, lam_init, layer, ppb=8):
    nb, nh, _ = q.shape
    n_pages = page_table.shape[1]
    assert n_pages % ppb == 0
    tok = pl.BlockSpec((None, nh, HEAD_DIM), lambda b, j, pt: (b, 0, 0))

    def page(r):
        return pl.BlockSpec((None, None, PAGE_SIZE, nh, HEAD_DIM),
                            lambda b, j, pt: (pt[b, j * ppb + r], layer, 0, 0, 0))

    pages = [page(r) for r in range(ppb)]
    grid_spec = pltpu.PrefetchScalarGridSpec(
        num_scalar_prefetch=1,
        grid=(nb, n_pages // ppb),
        in_specs=[tok, tok, tok] + pages + pages +
                 [pl.BlockSpec((4, HALF_DIM), lambda b, j, pt: (0, 0)),
                  pl.BlockSpec((1, HEAD_DIM), lambda b, j, pt: (0, 0))],
        out_specs=tok,
        scratch_shapes=[pltpu.VMEM((nh, HEAD_DIM), F32),
                        pltpu.VMEM((nh, HEAD_DIM), F32),
                        pltpu.VMEM((nh, 2 * HEAD_DIM), F32),
                        pltpu.VMEM((ppb, PAGE_SIZE, nh, HEAD_DIM), F32)],
    )
    return pl.pallas_call(
        functools.partial(_attn_decode_kernel, lam_init, ppb),
        out_shape=jax.ShapeDtypeStruct((nb, nh, HEAD_DIM), BF16),
        grid_spec=grid_spec,
        compiler_params=_cparams("arbitrary", "arbitrary"),
        name="attn_decode",
    )(page_table, q, k_new, v_new, *([cache_k] * ppb), *([cache_v] * ppb), lambda_qk, subln.reshape(1, HEAD_DIM))


def _xpos_tables(pos):
    half = HEAD_DIM // 2
    angle = jnp.repeat(1.0 / (10000.0 ** jnp.linspace(0.0, 1.0, half, dtype=F32)), 2)
    ph = pos.astype(F32)[:, None] * angle
    sin, cos = jnp.sin(ph), jnp.cos(ph)
    even = (jnp.arange(HEAD_DIM) % 2) == 0
    return cos, jnp.where(even, -sin, 0.0), jnp.where(even, 0.0, sin)


def kernel(x_prompt, x_sample, state_conv_a, state_delta, state_ret, cache_k, cache_v, page_table, c_prompt, c_sample, w_ada, b_ada, w_in, conv_a, a_log, dt_bias, norm_a, lambda_qk, subln_c, w_out, ln1_g, ln1_b, w_up, w_down, ln2_g, ln2_b):
    bp, t, d = x_prompt.shape
    nb = x_sample.shape[0]
    depth = w_in.shape[0]
    h_a = state_delta.shape[2]
    h_b = state_ret.shape[2]
    h_c = cache_k.shape[3]
    a_w, b_w, c_w = h_a * HEAD_DIM, h_b * HEAD_DIM, h_c * HEAD_DIM
    n_main = 4 * a_w
    n_gate = 2 * h_a
    past_len = page_table.shape[1] * PAGE_SIZE
    alpha = (2 * depth) ** 0.25
    rs = SAMPLE_ROWS
    assert nb + bp <= rs and x_sample.shape[1] == 1

    c_all = jnp.concatenate([c_sample, c_prompt, jnp.zeros((rs - nb - bp, d), F32)], axis=0)
    mod = _ada(c_all, w_ada, b_ada)

    def mod_p(l, i):
        return mod[l, nb:nb + bp, None, i * d:(i + 1) * d]

    def mod_s(l, i):
        return mod[l][None, :, i * d:(i + 1) * d]

    cos_p, sine_p, sino_p = _xpos_tables(jnp.arange(t))
    cos_s, sine_s, sino_s = _xpos_tables(past_len + jnp.arange(1))

    xp = x_prompt
    xs = jnp.concatenate([x_sample.reshape(nb, d), jnp.zeros((rs - nb, d), F32)], axis=0)[None]
    hp = _modulate(xp, mod_p(0, 1), mod_p(0, 0), 256)
    hs = _modulate(xs, mod_s(0, 1), mod_s(0, 0), rs)

    w_in_t = jnp.swapaxes(w_in, 1, 2)
    n_tail = w_in.shape[2] - n_main - n_gate
    w_gate_b = w_in[:, :, n_main:n_main + n_gate].astype(BF16)
    w_down_b = w_down.astype(BF16)

    st_p, st_s = [], []
    for l in range(depth):
        lam_init = 0.8 - 0.6 * math.exp(-0.3 * l)
        hp2 = hp.reshape(bp * t, d)
        hs2 = hs.reshape(rs, d)
        zp_main, zs_main = _panel_matmul(hp2, hs2, w_in_t, l, 0, n_main, transposed=True)
        zp_gate, zs_gate = _matmul(hp2, hs2, w_gate_b, l)
        zp_tail, zs_tail = _panel_matmul(hp2, hs2, w_in_t, l, n_main + n_gate, n_tail, transposed=True)
        zp_main = zp_main.reshape(bp, t, n_main)
        zp_tail = zp_tail.reshape(bp, t, -1)

        gc, beta = _gates(zp_gate.reshape(bp, t, n_gate), a_log[l], dt_bias[l])
        mix_p, delta_p = _delta_prompt(zp_main, conv_a[l], gc, beta, norm_a[l], h_a, a_w + b_w + c_w)
        mix_p, ret_p = _ret_prompt(zp_tail, cos_p, sine_p, sino_p, mix_p, a_w, h_b)
        mix_p = _attn_prompt(zp_tail, lambda_qk[l], subln_c[l], lam_init, mix_p, a_w + b_w, h_c, 4 * h_b)
        k_off = 4 * b_w + c_w
        conv_p = zp_main[:, t - (CONV_W - 1):, :3 * a_w]
        k_p = zp_tail[:, :, k_off:k_off + c_w]
        v_p = zp_tail[:, :, k_off + c_w:k_off + 2 * c_w]
        st_p.append((conv_p, delta_p, ret_p, k_p, v_p))

        zs_m = zs_main[:nb]
        zs_t = zs_tail[:nb]
        oa_s, delta_s = _delta_sample(zs_m[:, None], zs_gate[:nb, None], state_conv_a, conv_a[l], a_log[l],
                                      dt_bias[l], norm_a[l], state_delta, l, h_a)
        ob_s, ret_s = _ret_sample(zs_t[:, None], cos_s, sine_s, sino_s, state_ret, l, h_b)
        q_s = zs_t[:, 4 * b_w:4 * b_w + c_w].reshape(nb, h_c, HEAD_DIM)
        k_s = zs_t[:, k_off:k_off + c_w].reshape(nb, h_c, HEAD_DIM)
        v_s = zs_t[:, k_off + c_w:k_off + 2 * c_w].reshape(nb, h_c, HEAD_DIM)
        oc_s = _attn_decode(q_s, k_s, v_s, cache_k, cache_v, page_table, lambda_qk[l], subln_c[l], lam_init, l)
        conv_s = jnp.concatenate([state_conv_a[:, l, 1:], zs_m[:, None, :3 * a_w]], axis=1)
        st_s.append((conv_s, delta_s, ret_s, k_s[:, None], v_s[:, None]))

        mix_p = mix_p.reshape(bp * t, -1)
        mix_s = jnp.concatenate([oa_s.reshape(nb, a_w), ob_s.reshape(nb, b_w), oc_s.reshape(nb, c_w)], axis=-1)
        mix_s = jnp.concatenate([mix_s, jnp.zeros((rs - nb, mix_s.shape[1]), BF16)], axis=0)
        yp, ys = _panel_matmul(mix_p, mix_s, w_out, l, 0, d)
        xp, hp = _postnorm(xp, yp.reshape(bp, t, d), mod_p(l, 2), ln1_g[l], ln1_b[l], mod_p(l, 4), mod_p(l, 3),
                           alpha, 256)
        xs, hs = _postnorm(xs, ys[None], mod_s(l, 2), ln1_g[l], ln1_b[l], mod_s(l, 4), mod_s(l, 3), alpha, rs)

        up, us = _panel_matmul(hp.reshape(bp * t, d), hs.reshape(rs, d), w_up, l, 0, w_up.shape[2],
                               out_dtype=BF16, relu2=True)
        fp, fs = _matmul(up, us, w_down_b, l)
        last = l == depth - 1
        nl = min(l + 1, depth - 1)
        xp, hp = _postnorm(xp, fp.reshape(bp, t, d), mod_p(l, 5), ln2_g[l], ln2_b[l], mod_p(nl, 1), mod_p(nl, 0),
                           alpha, 256, with_h=not last)
        xs, hs = _postnorm(xs, fs[None], mod_s(l, 5), ln2_g[l], ln2_b[l], mod_s(nl, 1), mod_s(nl, 0), alpha, rs,
                           with_h=not last)

    outs_p = [jnp.stack([s[i] for s in st_p], axis=1) for i in range(5)]
    outs_p[3:] = [o.reshape(bp, depth, t, h_c, HEAD_DIM) for o in outs_p[3:]]
    outs_p = tuple(outs_p)
    outs_s = tuple(jnp.stack([s[i] for s in st_s], axis=1) for i in range(5))
    y_sample = xs[0, :nb].reshape(nb, 1, d)
    return (xp, y_sample) + outs_p + outs_s
```

```python
import functools
import math

import jax
import jax.numpy as jnp
from jax import lax
from jax.experimental import pallas as pl
from jax.experimental.pallas import tpu as pltpu

F32 = jnp.float32
BF16 = jnp.bfloat16

HEAD_DIM = 128
HALF_DIM = HEAD_DIM // 2
CONV_W = 4
DELTA_CHUNK = 64
DELTA_SUPER = 256
DELTA_BASE = 4
RET_CHUNK = 128
PAGE_SIZE = 128
EPS = 1e-5
SAMPLE_ROWS = 16
VMEM_LIMIT = 58 * 1024 * 1024


def _cparams(*sem):
    return pltpu.CompilerParams(dimension_semantics=sem, vmem_limit_bytes=VMEM_LIMIT)


def _sigmoid(x):
    return 1.0 / (1.0 + jnp.exp(-x))


def _silu(x):
    return x * _sigmoid(x)


def _softplus(x):
    return jnp.maximum(x, 0.0) + jnp.log1p(jnp.exp(-jnp.abs(x)))


def _dot(a, b):
    return jnp.dot(a.astype(BF16), b.astype(BF16), preferred_element_type=F32)


def _dot_nt(a, b):
    return lax.dot_general(a.astype(BF16), b.astype(BF16), (((1,), (1,)), ((), ())),
                           preferred_element_type=F32)


def _rms_rows(x):
    return x * lax.rsqrt(jnp.mean(x * x, axis=-1, keepdims=True) + EPS)


def _ada_kernel(c_ref, w0_ref, w1_ref, b_ref, o_ref):
    c = _silu(c_ref[...])
    dh = w0_ref.shape[0]
    o_ref[...] = _dot(c[:, :dh], w0_ref[...]) + _dot(c[:, dh:], w1_ref[...]) + b_ref[...]


def _ada(c_all, w_ada, b_ada, tn=1024):
    depth, d, n = w_ada.shape
    rows = c_all.shape[0]
    return pl.pallas_call(
        _ada_kernel,
        out_shape=jax.ShapeDtypeStruct((depth, rows, n), F32),
        grid=(depth, n // tn),
        in_specs=[pl.BlockSpec((rows, d), lambda l, j: (0, 0)),
                  pl.BlockSpec((None, d // 2, tn), lambda l, j: (l, 0, j)),
                  pl.BlockSpec((None, d // 2, tn), lambda l, j: (l, 1, j)),
                  pl.BlockSpec((None, 1, tn), lambda l, j: (l, 0, j))],
        out_specs=pl.BlockSpec((None, rows, tn), lambda l, j: (l, 0, j)),
        compiler_params=_cparams("arbitrary", "arbitrary"),
        name="ada_mod",
    )(c_all, w_ada, w_ada, b_ada.reshape(depth, 1, n))


def _modulate_kernel(x_ref, sc_ref, sh_ref, o_ref):
    o_ref[...] = (x_ref[...] * (1.0 + sc_ref[...]) + sh_ref[...]).astype(o_ref.dtype)


def _modulate(x, sc, sh, tm):
    g, r, d = x.shape
    rm = sc.shape[1]
    mspec = pl.BlockSpec((None, rm, d), lambda a, i: (a, 0, 0))
    return pl.pallas_call(
        _modulate_kernel,
        out_shape=jax.ShapeDtypeStruct((g, r, d), BF16),
        grid=(g, r // tm),
        in_specs=[pl.BlockSpec((None, tm, d), lambda a, i: (a, i, 0)), mspec, mspec],
        out_specs=pl.BlockSpec((None, tm, d), lambda a, i: (a, i, 0)),
        compiler_params=_cparams("arbitrary", "arbitrary"),
        name="modulate",
    )(x, sc, sh)


def _postnorm_kernel(alpha, with_h, x_ref, y_ref, gate_ref, lg_ref, lb_ref, sc_ref, sh_ref, xo_ref, *h_ref):
    v = alpha * x_ref[...] + (1.0 + gate_ref[...]) * y_ref[...]
    vc = v - jnp.mean(v, axis=-1, keepdims=True)
    var = jnp.mean(vc * vc, axis=-1, keepdims=True)
    xn = vc * lax.rsqrt(var + EPS) * lg_ref[...] + lb_ref[...]
    xo_ref[...] = xn
    if with_h:
        h_ref[0][...] = (xn * (1.0 + sc_ref[...]) + sh_ref[...]).astype(BF16)


def _postnorm(x, y, gate, ln_g, ln_b, sc, sh, alpha, tm, with_h=True):
    g, r, d = x.shape
    rm = gate.shape[1]
    row = pl.BlockSpec((None, tm, d), lambda a, i: (a, i, 0))
    mspec = pl.BlockSpec((None, rm, d), lambda a, i: (a, 0, 0))
    vec = pl.BlockSpec((1, d), lambda a, i: (0, 0))
    out_shape = [jax.ShapeDtypeStruct((g, r, d), F32)]
    out_specs = [row]
    if with_h:
        out_shape.append(jax.ShapeDtypeStruct((g, r, d), BF16))
        out_specs.append(row)
    res = pl.pallas_call(
        functools.partial(_postnorm_kernel, alpha, with_h),
        out_shape=out_shape,
        grid=(g, r // tm),
        in_specs=[row, row, mspec, vec, vec, mspec, mspec],
        out_specs=out_specs,
        compiler_params=_cparams("arbitrary", "arbitrary"),
        name="postnorm",
    )(x, y, gate, ln_g.reshape(1, d), ln_b.reshape(1, d), sc, sh)
    return (res[0], res[1]) if with_h else (res[0], None)


def _matmul_kernel(relu2, nk, a_ref, as_ref, w_ref, o_ref, os_ref, *scratch):
    i = pl.program_id(1)
    k = pl.program_id(2)

    def finish(v):
        return jnp.square(jnp.maximum(v, 0.0)) if relu2 else v

    if nk == 1:
        o_ref[...] = finish(jnp.dot(a_ref[...], w_ref[...], preferred_element_type=F32)).astype(o_ref.dtype)

        @pl.when(i == 0)
        def _():
            os_ref[...] = finish(jnp.dot(as_ref[...], w_ref[...], preferred_element_type=F32)).astype(os_ref.dtype)
    else:
        acc_ref, accs_ref = scratch

        @pl.when(k == 0)
        def _():
            acc_ref[...] = jnp.zeros_like(acc_ref)

        acc_ref[...] += jnp.dot(a_ref[...], w_ref[...], preferred_element_type=F32)

        @pl.when(k == nk - 1)
        def _():
            o_ref[...] = finish(acc_ref[...]).astype(o_ref.dtype)

        @pl.when(i == 0)
        def _():
            @pl.when(k == 0)
            def _():
                accs_ref[...] = jnp.zeros_like(accs_ref)

            accs_ref[...] += jnp.dot(as_ref[...], w_ref[...], preferred_element_type=F32)

            @pl.when(k == nk - 1)
            def _():
                os_ref[...] = finish(accs_ref[...]).astype(os_ref.dtype)


def _matmul(a, a_s, w3, layer, col0=0, n=None, out_dtype=F32, relu2=False, tm=1024, tn=1024, tk=4096):
    m, kdim = a.shape
    n = w3.shape[2] if n is None else n
    rs = a_s.shape[0]
    tn = min(tn, n)
    tk = min(tk, kdim)
    nk = kdim // tk
    assert n % tn == 0 and col0 % tn == 0 and m % tm == 0 and kdim % tk == 0
    cb0 = col0 // tn
    scratch = [] if nk == 1 else [pltpu.VMEM((tm, tn), F32), pltpu.VMEM((rs, tn), F32)]
    return pl.pallas_call(
        functools.partial(_matmul_kernel, relu2, nk),
        out_shape=[jax.ShapeDtypeStruct((m, n), out_dtype), jax.ShapeDtypeStruct((rs, n), out_dtype)],
        grid=(n // tn, m // tm, nk),
        in_specs=[pl.BlockSpec((tm, tk), lambda j, i, k: (i, k)),
                  pl.BlockSpec((rs, tk), lambda j, i, k: (0, k)),
                  pl.BlockSpec((None, tk, tn), lambda j, i, k: (layer, k, cb0 + j))],
        out_specs=[pl.BlockSpec((tm, tn), lambda j, i, k: (i, j)),
                   pl.BlockSpec((rs, tn), lambda j, i, k: (0, j))],
        scratch_shapes=scratch,
        compiler_params=_cparams("arbitrary", "arbitrary", "arbitrary"),
        name="matmul_relu2" if relu2 else "matmul",
    )(a, a_s, w3)


def _panel_matmul_kernel(relu2, n_panels, transposed, a_ref, as_ref, w_ref, o_ref, os_ref, wb_ref):
    jj = pl.program_id(0)
    i = pl.program_id(1)
    rows = w_ref.shape[0]

    def product(x, w):
        if transposed:
            v = lax.dot_general(x, w, (((1,), (1,)), ((), ())), preferred_element_type=F32)
        else:
            v = jnp.dot(x, w, preferred_element_type=F32)
        return jnp.square(jnp.maximum(v, 0.0)) if relu2 else v

    @pl.when(jj < n_panels)
    def _():
        wb_ref[jj % 2, pl.ds(pl.multiple_of(i * rows, rows), rows), :] = w_ref[...].astype(BF16)

    @pl.when(jj > 0)
    def _():
        w = wb_ref[(jj + 1) % 2]
        o_ref[...] = product(a_ref[...], w).astype(o_ref.dtype)

        @pl.when(i == 0)
        def _():
            os_ref[...] = product(as_ref[...], w).astype(os_ref.dtype)


def _panel_matmul(a, a_s, w3, layer, col0, n, transposed=False, out_dtype=F32, relu2=False, tm=1024, tn=1024):
    m, kdim = a.shape
    rs = a_s.shape[0]
    n_i = m // tm
    n_panels = n // tn
    assert n % tn == 0 and m % tm == 0

    def clamp(jj, i):
        last = jj == n_panels
        return jnp.minimum(jj, n_panels - 1), jnp.where(last, n_i - 1, i)

    if transposed:
        assert tn % (8 * n_i) == 0 and col0 % 8 == 0
        rows = tn // n_i

        def w_index(jj, i):
            panel, chunk = clamp(jj, i)
            return (layer, pl.multiple_of(col0 + panel * tn + chunk * rows, 8), 0)

        w_spec = pl.BlockSpec((None, pl.Element(rows), pl.Element(kdim)), w_index)
        wb_shape = (2, tn, kdim)
    else:
        assert kdim % (8 * n_i) == 0 and col0 % tn == 0
        rows = kdim // n_i

        def w_index(jj, i):
            panel, chunk = clamp(jj, i)
            return (layer, chunk, col0 // tn + panel)

        w_spec = pl.BlockSpec((None, rows, tn), w_index)
        wb_shape = (2, kdim, tn)

    def row_block(jj, i):
        return jnp.where(jj == 0, 0, i)

    return pl.pallas_call(
        functools.partial(_panel_matmul_kernel, relu2, n_panels, transposed),
        out_shape=[jax.ShapeDtypeStruct((m, n), out_dtype), jax.ShapeDtypeStruct((rs, n), out_dtype)],
        grid=(n_panels + 1, n_i),
        in_specs=[pl.BlockSpec((tm, kdim), lambda jj, i: (row_block(jj, i), 0)),
                  pl.BlockSpec((rs, kdim), lambda jj, i: (0, 0)), w_spec],
        out_specs=[pl.BlockSpec((tm, tn), lambda jj, i: (row_block(jj, i), jnp.maximum(jj - 1, 0))),
                   pl.BlockSpec((rs, tn), lambda jj, i: (0, jnp.maximum(jj - 1, 0)))],
        scratch_shapes=[pltpu.VMEM(wb_shape, BF16)],
        compiler_params=_cparams("arbitrary", "arbitrary"),
        name="panel_matmul_relu2" if relu2 else "panel_matmul",
    )(a, a_s, w3)


def _gates_kernel(n_heads, zg_ref, alog_ref, dtb_ref, gc_ref, beta_ref):
    zg = zg_ref[...]
    rows = zg.shape[0]
    g = -jnp.exp(alog_ref[...]) * _softplus(zg[:, :n_heads] + dtb_ref[...])
    beta = _sigmoid(zg[:, n_heads:])
    pos = lax.broadcasted_iota(jnp.int32, (rows, HEAD_DIM), 0) % DELTA_CHUNK
    for h in range(n_heads):
        gh = jnp.broadcast_to(g[:, h:h + 1], (rows, HEAD_DIM))
        s = 1
        while s < DELTA_CHUNK:
            gh = gh + jnp.where(pos >= s, pltpu.roll(gh, s, 0), 0.0)
            s *= 2
        gc_ref[h] = gh
        beta_ref[h] = jnp.broadcast_to(beta[:, h:h + 1], (rows, HEAD_DIM))


def _gates(zg, a_log, dt_bias, rows=256):
    b, t, h2 = zg.shape
    nh = h2 // 2
    out = jax.ShapeDtypeStruct((b, nh, t, HEAD_DIM), F32)
    ospec = pl.BlockSpec((None, nh, rows, HEAD_DIM), lambda a, i: (a, 0, i, 0))
    vec = pl.BlockSpec((1, nh), lambda a, i: (0, 0))
    return pl.pallas_call(
        functools.partial(_gates_kernel, nh),
        out_shape=[out, out],
        grid=(b, t // rows),
        in_specs=[pl.BlockSpec((None, rows, h2), lambda a, i: (a, i, 0)), vec, vec],
        out_specs=[ospec, ospec],
        compiler_params=_cparams("arbitrary", "arbitrary"),
        name="delta_gates",
    )(zg, a_log.reshape(1, nh), dt_bias.reshape(1, nh))


def _delta_prompt_kernel(nh, q_ref, k_ref, v_ref, za_ref, cq_ref, ck_ref, cv_ref, gc_ref, beta_ref, norm_ref,
                         o_ref, s_ref, wq_ref, ka_ref, u_ref, eg_ref):
    t = q_ref.shape[0]
    sb = DELTA_SUPER
    c = DELTA_CHUNK
    n_sb = t // sb
    cps = sb // c
    half = sb // 2
    ii = lax.broadcasted_iota(jnp.int32, (sb, sb), 0)
    jj = lax.broadcasted_iota(jnp.int32, (sb, sb), 1)
    lower = ((ii // c) == (jj // c)) & (ii >= jj)
    strict = ii > jj
    row = lax.broadcasted_iota(jnp.int32, (sb, HEAD_DIM), 0)
    pair_chunk = lax.broadcasted_iota(jnp.int32, (HEAD_DIM, 2 * c), 1) // c
    blk_xor = ii ^ jj
    base = DELTA_BASE
    base_levels = int(math.log2(base))

    def l2norm(x):
        return x * lax.rsqrt(jnp.sum(x * x, axis=-1, keepdims=True) + 1e-6)

    def prepare(n, hh):
        static = isinstance(n, int)
        r0 = n * sb if static else pl.multiple_of(n * sb, sb)
        rows = pl.ds(r0, sb)
        lanes = slice(hh * HEAD_DIM, (hh + 1) * HEAD_DIM)

        def conv_silu(x_ref, w_ref):
            x = x_ref[rows, lanes]
            w = w_ref[:, lanes]
            if static and n == 0:
                prev = jnp.zeros((8, HEAD_DIM), F32)
            else:
                prev = x_ref[pl.ds(r0 - 8 if static else pl.multiple_of(r0 - 8, 8), 8), lanes]
            y = x * w[CONV_W - 1:CONV_W]
            for s in range(1, CONV_W):
                head = jnp.broadcast_to(pltpu.roll(prev, s, 0)[None], (sb // 8, 8, HEAD_DIM))
                xs = jnp.where(row < s, head.reshape(sb, HEAD_DIM), pltpu.roll(x, s, 0))
                y = y + xs * w[CONV_W - 1 - s:CONV_W - s]
            return _silu(y)

        q = l2norm(conv_silu(q_ref, cq_ref)) * (HEAD_DIM ** -0.5)
        k = l2norm(conv_silu(k_ref, ck_ref))
        v = conv_silu(v_ref, cv_ref)
        gc = gc_ref[hh, rows]
        beta = beta_ref[hh, rows]
        kb = k * beta
        eg = jnp.exp(gc)
        gct = jnp.concatenate([gc[:half].T, gc[half:].T], axis=1)
        gc_j = jnp.broadcast_to(gct[0:1], (sb, sb))
        gc_i = jnp.concatenate([gc] * (sb // HEAD_DIM), axis=1)
        dec = jnp.exp(jnp.where(lower, gc_i - gc_j, -jnp.inf))
        kq = _dot_nt(jnp.concatenate([kb, q], axis=0), k)
        yield
        mm = jnp.where(strict, kq[:sb] * dec, 0.0)
        attn = kq[sb:] * dec
        x = jnp.where(blk_xor < base, -mm, 0.0)
        p = jnp.where(ii == jj, 1.0, x)
        xb = x.astype(BF16)
        x = jnp.dot(xb, xb, preferred_element_type=F32)
        yield
        for lvl in range(1, base_levels):
            xb = x.astype(BF16)
            if lvl < base_levels - 1:
                y = jnp.dot(jnp.concatenate([p.astype(BF16), xb], axis=0), xb, preferred_element_type=F32)
                yield
                p = p + y[:sb]
                x = y[sb:]
            else:
                y = jnp.dot(p.astype(BF16), xb, preferred_element_type=F32)
                yield
                p = p + y
        size = base
        while size < c:
            pb = p.astype(BF16)
            off = jnp.where((blk_xor >= size) & (blk_xor < 2 * size), mm, 0.0).astype(BF16)
            e = jnp.dot(off, pb, preferred_element_type=F32)
            yield
            y = jnp.dot(pb, e.astype(BF16), preferred_element_type=F32)
            yield
            p = p - y
            size *= 2
        rhs = jnp.concatenate([v * beta, kb * eg], axis=1)
        pb = p.astype(BF16)
        uw = jnp.dot(pb, rhs.astype(BF16), preferred_element_type=F32)
        yield
        mm_hi = mm.astype(BF16)
        mm_lo = (mm - mm_hi.astype(F32)).astype(BF16)
        uw_hi = uw.astype(BF16)
        uw_lo = (uw - uw_hi.astype(F32)).astype(BF16)
        resid = rhs - uw - (jnp.dot(mm_hi, uw_hi, preferred_element_type=F32)
                            + jnp.dot(mm_hi, uw_lo, preferred_element_type=F32)
                            + jnp.dot(mm_lo, uw_hi, preferred_element_type=F32))
        yield
        uw = uw + jnp.dot(pb, resid.astype(BF16), preferred_element_type=F32)
        yield
        u_ref[hh, rows] = uw[:, :HEAD_DIM]
        w = uw[:, HEAD_DIM:]
        qd = q * eg
        gl = jnp.concatenate([jnp.broadcast_to(gc[ci * c + c - 1:ci * c + c], (c, HEAD_DIM))
                              for ci in range(cps)], axis=0)
        kd = k * jnp.exp(gl - gc)
        kdt = jnp.concatenate([kd[:half].T, kd[half:].T], axis=1)
        for ci in range(cps):
            cc = n * cps + ci
            cr = slice(ci * c, (ci + 1) * c)
            pair = slice((ci // 2) * 2 * c, (ci // 2 + 1) * 2 * c)
            wq_ref[hh, cc] = jnp.concatenate([w[cr], qd[cr]], axis=0).astype(BF16)
            ka_ref[hh, cc] = jnp.concatenate([jnp.where(pair_chunk == ci % 2, kdt[:, pair], 0.0),
                                              attn[cr, pair]], axis=0).astype(BF16)
            eg_ref[hh, cc] = jnp.exp(gl[ci * c:ci * c + 8])

    def recur(n, states):
        static = isinstance(n, int)
        for ci in range(cps):
            cc = n * cps + ci
            crow = pl.ds(cc * c if static else pl.multiple_of(cc * c, c), c)
            r1 = [jnp.dot(wq_ref[hh, cc], states[hh].astype(BF16), preferred_element_type=F32) for hh in range(nh)]
            yield
            r2 = []
            for hh in range(nh):
                v_new = (u_ref[hh, crow] - r1[hh][:c]).astype(BF16)
                r2.append(jnp.dot(ka_ref[hh, cc], jnp.concatenate([v_new, v_new], axis=0),
                                  preferred_element_type=F32))
            yield
            for hh in range(nh):
                lanes = slice(hh * HEAD_DIM, (hh + 1) * HEAD_DIM)
                o = r1[hh][c:] + r2[hh][HEAD_DIM:]
                states[hh] = states[hh] * eg_ref[hh, cc][0:1] + r2[hh][:HEAD_DIM]
                o_ref[crow, lanes] = (_rms_rows(o) * norm_ref[...] * _silu(za_ref[crow, lanes])).astype(o_ref.dtype)

    def interleave(gens):
        live = list(gens)
        while live:
            live = [g for g in live if next(g, live) is not live]

    def chain(*gens):
        for g in gens:
            yield from g

    bpi = 2
    assert n_sb % bpi == 0

    def body(it, states):
        states = list(states)
        first = it * bpi
        interleave([prepare(first + r, hh) for r in range(bpi) for hh in range(nh)]
                   + [chain(*[recur(first - bpi + r, states) for r in range(bpi)])])
        return tuple(states)

    interleave([prepare(r, hh) for r in range(bpi) for hh in range(nh)])
    states = lax.fori_loop(1, n_sb // bpi, body, tuple(jnp.zeros((HEAD_DIM, HEAD_DIM), F32) for _ in range(nh)))
    states = list(states)
    interleave([chain(*[recur(n_sb - bpi + r, states) for r in range(bpi)])])
    for hh in range(nh):
        s_ref[hh] = states[hh]


def _delta_prompt(z_main, conv_w, gc, beta, norm_w, n_heads, mix_width, hpb=2):
    b, t, _ = z_main.shape
    wl = hpb * HEAD_DIM
    ng = n_heads // hpb
    n_chunks = t // DELTA_CHUNK
    col = lambda off: pl.BlockSpec((None, t, wl), lambda a, h: (a, 0, off + h))
    cw = lambda off: pl.BlockSpec((CONV_W, wl), lambda a, h: (0, off + h))
    gate = pl.BlockSpec((None, hpb, t, HEAD_DIM), lambda a, h: (a, h, 0, 0))
    return pl.pallas_call(
        functools.partial(_delta_prompt_kernel, hpb),
        out_shape=[jax.ShapeDtypeStruct((b, t, mix_width), BF16),
                   jax.ShapeDtypeStruct((b, n_heads, HEAD_DIM, HEAD_DIM), F32)],
        grid=(b, ng),
        in_specs=[col(0), col(ng), col(2 * ng), col(3 * ng),
                  cw(0), cw(ng), cw(2 * ng), gate, gate,
                  pl.BlockSpec((1, HEAD_DIM), lambda a, h: (0, 0))],
        out_specs=[pl.BlockSpec((None, t, wl), lambda a, h: (a, 0, h)),
                   pl.BlockSpec((None, hpb, HEAD_DIM, HEAD_DIM), lambda a, h: (a, h, 0, 0))],
        scratch_shapes=[pltpu.VMEM((hpb, n_chunks, 2 * DELTA_CHUNK, HEAD_DIM), BF16),
                        pltpu.VMEM((hpb, n_chunks, HEAD_DIM + DELTA_CHUNK, 2 * DELTA_CHUNK), BF16),
                        pltpu.VMEM((hpb, t, HEAD_DIM), F32),
                        pltpu.VMEM((hpb, n_chunks, 8, HEAD_DIM), F32)],
        compiler_params=_cparams("arbitrary", "arbitrary"),
        name="delta_prompt",
    )(z_main, z_main, z_main, z_main, conv_w, conv_w, conv_w, gc, beta, norm_w.reshape(1, HEAD_DIM))


def _xpos(x, cos, sin_e, sin_o):
    return x * cos + pltpu.roll(x, HEAD_DIM - 1, 1) * sin_e + pltpu.roll(x, 1, 1) * sin_o


def _log_gamma(h):
    return jnp.log1p(-jnp.exp2(-5.0 - h))


def _ret_prompt_kernel(q_ref, k_ref, v_ref, g_ref, cos_ref, sine_ref, sino_ref, mix_ref, o_ref, s_ref,
                       qs_ref, ks_ref):
    del mix_ref
    t = q_ref.shape[0]
    c = RET_CHUNK
    cos, sin_e, sin_o = cos_ref[...], sine_ref[...], sino_ref[...]
    qs_ref[...] = _xpos(q_ref[...], cos, sin_e, sin_o)
    ks_ref[...] = _xpos(k_ref[...], cos, sin_e, sin_o) * (HEAD_DIM ** -0.5)

    h = jnp.full((1, 1), pl.program_id(1), jnp.int32).astype(F32)
    lg = _log_gamma(h)
    ii = lax.broadcasted_iota(jnp.int32, (c, c), 0)
    jj = lax.broadcasted_iota(jnp.int32, (c, c), 1)
    dec = jnp.exp(jnp.where(ii >= jj, (ii - jj).astype(F32) * lg, -jnp.inf))
    pos = lax.broadcasted_iota(jnp.int32, (c, HEAD_DIM), 0).astype(F32)
    e_in = jnp.exp((pos + 1.0) * lg)
    e_out = jnp.exp((c - 1.0 - pos) * lg)
    e_all = jnp.exp(c * lg)

    unroll = 2

    def body(n, s):
        rows = [pl.ds(pl.multiple_of((n * unroll + r) * c, c), c) for r in range(unroll)]
        q = [qs_ref[rw] for rw in rows]
        k = [ks_ref[rw] for rw in rows]
        v = [v_ref[rw].astype(BF16) for rw in rows]
        qk = [_dot_nt(q[r], k[r]) for r in range(unroll)]
        upd = [_dot((k[r] * e_out).T, v[r]) for r in range(unroll)]
        for r in range(unroll):
            o = _dot(q[r] * e_in, s) + _dot(qk[r] * dec, v[r])
            s = s * e_all + upd[r]
            o_ref[rows[r]] = (_rms_rows(o) * _silu(g_ref[rows[r]])).astype(o_ref.dtype)
        return s

    s_ref[...] = lax.fori_loop(0, t // (c * unroll), body, jnp.zeros((HEAD_DIM, HEAD_DIM), F32))


def _ret_prompt(z_tail, cos, sin_e, sin_o, mix, mix_col0, n_heads):
    b, t, _ = z_tail.shape
    c0 = mix_col0 // HEAD_DIM
    col = lambda off: pl.BlockSpec((None, t, HEAD_DIM), lambda a, h: (a, 0, off + h))
    tab = pl.BlockSpec((t, HEAD_DIM), lambda a, h: (0, 0))
    return pl.pallas_call(
        _ret_prompt_kernel,
        out_shape=[jax.ShapeDtypeStruct(mix.shape, mix.dtype),
                   jax.ShapeDtypeStruct((b, n_heads, HEAD_DIM, HEAD_DIM), F32)],
        grid=(b, n_heads),
        in_specs=[col(0), col(n_heads), col(2 * n_heads), col(3 * n_heads), tab, tab, tab,
                  pl.BlockSpec(memory_space=pl.ANY)],
        out_specs=[pl.BlockSpec((None, t, HEAD_DIM), lambda a, h: (a, 0, c0 + h)),
                   pl.BlockSpec((None, None, HEAD_DIM, HEAD_DIM), lambda a, h: (a, h, 0, 0))],
        scratch_shapes=[pltpu.VMEM((t, HEAD_DIM), F32)] * 2,
        input_output_aliases={7: 0},
        compiler_params=_cparams("arbitrary", "arbitrary"),
        name="ret_prompt",
    )(z_tail, z_tail, z_tail, z_tail, cos, sin_e, sin_o, mix)


def _lambda(lq, lam_init):
    l01 = jnp.sum(lq[0:1] * lq[1:2], axis=-1, keepdims=True)
    l23 = jnp.sum(lq[2:3] * lq[3:4], axis=-1, keepdims=True)
    return jnp.exp(l01) - jnp.exp(l23) + lam_init


def _attn_prompt_kernel(lam_init, nh, tq, q_ref, k_ref, v_ref, lq_ref, sub_ref, mix_ref, o_ref,
                        q2_ref, s_ref, p_ref, m_ref, l_ref, acc_ref):
    del mix_ref
    i = pl.program_id(2)
    strip = 64
    lane = lax.broadcasted_iota(jnp.int32, (tq, HEAD_DIM), 1)
    scale = HALF_DIM ** -0.5
    for hh in range(nh):
        q = q_ref[:, hh * HEAD_DIM:(hh + 1) * HEAD_DIM]
        q2_ref[hh] = (jnp.concatenate([jnp.where(lane < HALF_DIM, q, 0.0), jnp.where(lane >= HALF_DIM, q, 0.0)],
                                      axis=0) * scale).astype(BF16)
    m_ref[...] = jnp.full(m_ref.shape, -jnp.inf, F32)
    l_ref[...] = jnp.zeros_like(l_ref)
    acc_ref[...] = jnp.zeros_like(acc_ref)
    key_idx = lax.broadcasted_iota(jnp.int32, (strip, tq), 1)
    qry_idx = lax.broadcasted_iota(jnp.int32, (strip, tq), 0)

    def step(j, diagonal):
        rows = pl.ds(pl.multiple_of(j * tq, tq), tq)
        for hh in range(nh):
            s_ref[hh] = _dot_nt(q2_ref[hh], k_ref[rows, hh * HEAD_DIM:(hh + 1) * HEAD_DIM])
        for hh in range(nh):
            for c in range(2 * tq // strip):
                sr = slice(c * strip, (c + 1) * strip)
                s = s_ref[hh, sr]
                if diagonal:
                    q0 = (c * strip) % tq
                    s = jnp.where(key_idx <= qry_idx + q0, s, -jnp.inf)
                m_old = m_ref[hh, sr]
                m_new = jnp.maximum(m_old, jnp.max(s, axis=-1, keepdims=True))
                alpha = jnp.exp(m_old - m_new)
                p = jnp.exp(s - jnp.concatenate([m_new] * (tq // HEAD_DIM), axis=1))
                l_ref[hh, sr] = alpha * l_ref[hh, sr] + jnp.sum(p, axis=-1, keepdims=True)
                m_ref[hh, sr] = m_new
                p_ref[hh, sr] = p.astype(BF16)
                acc_ref[hh, sr] = alpha * acc_ref[hh, sr]
            acc_ref[hh] += jnp.dot(p_ref[hh], v_ref[rows, hh * HEAD_DIM:(hh + 1) * HEAD_DIM].astype(BF16),
                                   preferred_element_type=F32)

    def body(j, carry):
        step(j, False)
        return carry

    lax.fori_loop(0, i, body, 0)
    step(i, True)
    lam = _lambda(lq_ref[...], lam_init)
    for hh in range(nh):
        o2 = acc_ref[hh] / l_ref[hh]
        o = o2[:tq] - lam * o2[tq:]
        o_ref[:, hh * HEAD_DIM:(hh + 1) * HEAD_DIM] = (_rms_rows(o) * sub_ref[...] * (1.0 - lam_init)).astype(o_ref.dtype)


def _attn_prompt(z_tail, lambda_qk, subln, lam_init, mix, mix_col0, n_heads, col0, tq=256, hpb=2):
    b, t, _ = z_tail.shape
    wl = hpb * HEAD_DIM
    ng = n_heads // hpb
    c0 = col0 // hpb
    m0 = mix_col0 // wl
    return pl.pallas_call(
        functools.partial(_attn_prompt_kernel, lam_init, hpb, tq),
        out_shape=jax.ShapeDtypeStruct(mix.shape, mix.dtype),
        grid=(b, ng, t // tq),
        in_specs=[pl.BlockSpec((None, tq, wl), lambda a, h, i: (a, i, c0 + h)),
                  pl.BlockSpec((None, t, wl), lambda a, h, i: (a, 0, c0 + ng + h)),
                  pl.BlockSpec((None, t, wl), lambda a, h, i: (a, 0, c0 + 2 * ng + h)),
                  pl.BlockSpec((4, HALF_DIM), lambda a, h, i: (0, 0)),
                  pl.BlockSpec((1, HEAD_DIM), lambda a, h, i: (0, 0)),
                  pl.BlockSpec(memory_space=pl.ANY)],
        out_specs=pl.BlockSpec((None, tq, wl), lambda a, h, i: (a, i, m0 + h)),
        scratch_shapes=[pltpu.VMEM((hpb, 2 * tq, HEAD_DIM), BF16),
                        pltpu.VMEM((hpb, 2 * tq, tq), F32),
                        pltpu.VMEM((hpb, 2 * tq, tq), BF16),
                        pltpu.VMEM((hpb, 2 * tq, HEAD_DIM), F32),
                        pltpu.VMEM((hpb, 2 * tq, HEAD_DIM), F32),
                        pltpu.VMEM((hpb, 2 * tq, HEAD_DIM), F32)],
        input_output_aliases={5: 0},
        compiler_params=_cparams("arbitrary", "arbitrary", "arbitrary"),
        name="attn_prompt",
    )(z_tail, z_tail, z_tail, lambda_qk, subln.reshape(1, HEAD_DIM), mix)


def _lane_pick(x, h):
    lane = lax.broadcasted_iota(jnp.int32, x.shape, 1)
    return jnp.sum(jnp.where(lane == h, x, 0.0), axis=-1, keepdims=True)


def _row_to_col(x):
    ii = lax.broadcasted_iota(jnp.int32, (HEAD_DIM, HEAD_DIM), 0)
    jj = lax.broadcasted_iota(jnp.int32, (HEAD_DIM, HEAD_DIM), 1)
    return jnp.sum(jnp.where(ii == jj, jnp.broadcast_to(x, (HEAD_DIM, HEAD_DIM)), 0.0), axis=-1, keepdims=True)


def _rows8(*rows):
    pad = jnp.zeros((8 - len(rows), HEAD_DIM), F32)
    return jnp.concatenate(list(rows) + [pad], axis=0)


def _delta_sample_kernel(hps, q_ref, k_ref, v_ref, za_ref, bq_ref, bk_ref, bv_ref, cq_ref, ck_ref, cv_ref,
                         zg_ref, alog_ref, dtb_ref, norm_ref, s_ref, o_ref, so_ref):
    n_heads = alog_ref.shape[1]
    zg = zg_ref[...]

    def l2norm(x):
        return x * lax.rsqrt(jnp.sum(x * x, axis=-1, keepdims=True) + 1e-6)

    for hh in range(hps):
        h = pl.program_id(1) * hps + hh
        lanes = slice(hh * HEAD_DIM, (hh + 1) * HEAD_DIM)

        def conv_silu(x_ref, buf_ref, w_ref):
            w = w_ref[:, lanes]
            buf = buf_ref[:, lanes]
            y = x_ref[:, lanes] * w[CONV_W - 1:CONV_W]
            for j in range(CONV_W - 1):
                y = y + buf[j:j + 1] * w[j:j + 1]
            return _silu(y)

        q = l2norm(conv_silu(q_ref, bq_ref, cq_ref)) * (HEAD_DIM ** -0.5)
        k = l2norm(conv_silu(k_ref, bk_ref, ck_ref))
        v = conv_silu(v_ref, bv_ref, cv_ref)
        a_a = _lane_pick(zg, h)
        b_a = _lane_pick(zg, h + n_heads)
        g = -jnp.exp(_lane_pick(alog_ref[...], h)) * _softplus(a_a + _lane_pick(dtb_ref[...], h))
        beta = _sigmoid(b_a)
        eg = jnp.exp(g)
        s = s_ref[hh]
        ks_qs = _dot(_rows8(k, q), s)
        v_new = beta * (v - eg * ks_qs[0:1])
        qk = jnp.sum(q * k, axis=-1, keepdims=True)
        o = eg * ks_qs[1:2] + qk * v_new
        so_ref[hh] = s * eg + _row_to_col(k) * v_new
        o_ref[:, lanes] = (_rms_rows(o) * norm_ref[...] * _silu(za_ref[:, lanes])).astype(o_ref.dtype)


def _delta_sample(zs_main, zs_gate, conv_buf, conv_w, a_log, dt_bias, norm_w, state, layer, n_heads, hps=8):
    nb = state.shape[0]
    wl = hps * HEAD_DIM
    ng = n_heads // hps
    col = lambda off: pl.BlockSpec((None, 1, wl), lambda a, h: (a, 0, off + h))
    buf = lambda off: pl.BlockSpec((None, None, CONV_W - 1, wl), lambda a, h: (a, layer, 0, off + h))
    cw = lambda off: pl.BlockSpec((CONV_W, wl), lambda a, h: (0, off + h))
    vec = pl.BlockSpec((1, n_heads), lambda a, h: (0, 0))
    return pl.pallas_call(
        functools.partial(_delta_sample_kernel, hps),
        out_shape=[jax.ShapeDtypeStruct((nb, 1, n_heads * HEAD_DIM), BF16),
                   jax.ShapeDtypeStruct((nb, n_heads, HEAD_DIM, HEAD_DIM), F32)],
        grid=(nb, ng),
        in_specs=[col(0), col(ng), col(2 * ng), col(3 * ng),
                  buf(0), buf(ng), buf(2 * ng), cw(0), cw(ng), cw(2 * ng),
                  pl.BlockSpec((None, 1, 2 * n_heads), lambda a, h: (a, 0, 0)), vec, vec,
                  pl.BlockSpec((1, HEAD_DIM), lambda a, h: (0, 0)),
                  pl.BlockSpec((None, None, hps, HEAD_DIM, HEAD_DIM), lambda a, h: (a, layer, h, 0, 0))],
        out_specs=[pl.BlockSpec((None, 1, wl), lambda a, h: (a, 0, h)),
                   pl.BlockSpec((None, hps, HEAD_DIM, HEAD_DIM), lambda a, h: (a, h, 0, 0))],
        compiler_params=_cparams("arbitrary", "arbitrary"),
        name="delta_sample",
    )(zs_main, zs_main, zs_main, zs_main, conv_buf, conv_buf, conv_buf, conv_w, conv_w, conv_w,
      zs_gate, a_log.reshape(1, n_heads), dt_bias.reshape(1, n_heads), norm_w.reshape(1, HEAD_DIM), state)


def _ret_sample_kernel(hps, q_ref, k_ref, v_ref, g_ref, cos_ref, sine_ref, sino_ref, s_ref, o_ref, so_ref):
    cos, sin_e, sin_o = cos_ref[...], sine_ref[...], sino_ref[...]
    for hh in range(hps):
        lanes = slice(hh * HEAD_DIM, (hh + 1) * HEAD_DIM)
        h = jnp.full((1, 1), pl.program_id(1) * hps + hh, jnp.int32).astype(F32)
        gamma = jnp.exp(_log_gamma(h))
        q = _xpos(q_ref[:, lanes], cos, sin_e, sin_o)
        k = _xpos(k_ref[:, lanes], cos, sin_e, sin_o) * (HEAD_DIM ** -0.5)
        v = v_ref[:, lanes]
        s = s_ref[hh]
        qs = _dot(_rows8(q), s)
        qk = jnp.sum(q * k, axis=-1, keepdims=True)
        o = qk * v + gamma * qs[0:1]
        so_ref[hh] = s * gamma + _row_to_col(k) * v
        o_ref[:, lanes] = (_rms_rows(o) * _silu(g_ref[:, lanes])).astype(o_ref.dtype)


def _ret_sample(zs_tail, cos, sin_e, sin_o, state, layer, n_heads, hps=8):
    nb = state.shape[0]
    wl = hps * HEAD_DIM
    ng = n_heads // hps
    col = lambda off: pl.BlockSpec((None, 1, wl), lambda a, h: (a, 0, off + h))
    tab = pl.BlockSpec((1, HEAD_DIM), lambda a, h: (0, 0))
    return pl.pallas_call(
        functools.partial(_ret_sample_kernel, hps),
        out_shape=[jax.ShapeDtypeStruct((nb, 1, n_heads * HEAD_DIM), BF16),
                   jax.ShapeDtypeStruct((nb, n_heads, HEAD_DIM, HEAD_DIM), F32)],
        grid=(nb, ng),
        in_specs=[col(0), col(ng), col(2 * ng), col(3 * ng), tab, tab, tab,
                  pl.BlockSpec((None, None, hps, HEAD_DIM, HEAD_DIM), lambda a, h: (a, layer, h, 0, 0))],
        out_specs=[pl.BlockSpec((None, 1, wl), lambda a, h: (a, 0, h)),
                   pl.BlockSpec((None, hps, HEAD_DIM, HEAD_DIM), lambda a, h: (a, h, 0, 0))],
        compiler_params=_cparams("arbitrary", "arbitrary"),
        name="ret_sample",
    )(zs_tail, zs_tail, zs_tail, zs_tail, cos, sin_e, sin_o, state)


def _attn_decode_kernel(lam_init, ppb, pt_ref, q_ref, kn_ref, vn_ref, *rest):
    kc_refs, vc_refs = rest[:ppb], rest[ppb:2 * ppb]
    lq_ref, sub_ref, o_ref, m_ref, l_ref, acc_ref, s_ref = rest[2 * ppb:]
    j = pl.program_id(1)
    n_steps = pl.num_programs(1)
    nh = q_ref.shape[0]
    scale = HALF_DIM ** -0.5
    di = lax.broadcasted_iota(jnp.int32, (HEAD_DIM, 2 * HEAD_DIM), 0)
    ci = lax.broadcasted_iota(jnp.int32, (HEAD_DIM, 2 * HEAD_DIM), 1)
    expand = ((di // HALF_DIM) == (ci // HEAD_DIM)).astype(BF16)
    q = q_ref[...] * (scale * math.log2(math.e))
    strip = 8
    n_strips = PAGE_SIZE // strip

    @pl.when(j == 0)
    def _():
        m_ref[...] = jnp.full(m_ref.shape, -jnp.inf, F32)
        l_ref[...] = jnp.zeros_like(l_ref)
        acc_ref[...] = jnp.zeros_like(acc_ref)

    low = lax.broadcasted_iota(jnp.int32, (nh, HEAD_DIM), 1) < HALF_DIM

    def swap_halves(x):
        return pltpu.roll(x.reshape(-1, HEAD_DIM), HALF_DIM, 1).reshape(x.shape)

    def scores(kt):
        n_tok = kt.shape[0]
        prod = (kt * q[None]).reshape(n_tok * nh, HEAD_DIM)
        s = jnp.dot(prod.astype(BF16), expand, preferred_element_type=F32).reshape(n_tok, nh, 2 * HEAD_DIM)
        return jnp.where(low[None], s[:, :, :HEAD_DIM], s[:, :, HEAD_DIM:])

    def weighted(p, vt):
        return jnp.concatenate([jnp.sum(p * vt, axis=0), jnp.sum(swap_halves(p) * vt, axis=0)], axis=-1)

    m, l, acc = m_ref[...], l_ref[...], acc_ref[...]
    s_ref[0] = scores(kc_refs[0][...])
    for pg in range(ppb):
        if pg + 1 < ppb:
            s_ref[pg + 1] = scores(kc_refs[pg + 1][...])
        m_new = m
        for i in range(n_strips):
            m_new = jnp.maximum(m_new, jnp.max(s_ref[pg, i * strip:(i + 1) * strip], axis=0))
        alpha = jnp.exp2(m - m_new)
        l, acc, m = alpha * l, jnp.concatenate([alpha, swap_halves(alpha)], axis=-1) * acc, m_new
        for i in range(n_strips):
            p = jnp.exp2(s_ref[pg, i * strip:(i + 1) * strip] - m[None])
            l = l + jnp.sum(p, axis=0)
            acc = acc + weighted(p, vc_refs[pg][i * strip:(i + 1) * strip])
    m_ref[...] = m
    l_ref[...] = l
    acc_ref[...] = acc

    @pl.when(j == n_steps - 1)
    def _():
        s = scores(kn_ref[...][None])[0]
        m_old = m_ref[...]
        m_new = jnp.maximum(m_old, s)
        alpha = jnp.exp2(m_old - m_new)
        p = jnp.exp2(s - m_new)
        l = alpha * l_ref[...] + p
        acc = (jnp.concatenate([alpha, swap_halves(alpha)], axis=-1) * acc_ref[...]
               + weighted(p[None], vn_ref[...][None]))
        direct, cross = acc[:, :HEAD_DIM], acc[:, HEAD_DIM:]
        l_sw = swap_halves(l)
        o1 = jnp.where(low, direct, cross) / jnp.where(low, l, l_sw)
        o2 = jnp.where(low, cross, direct) / jnp.where(low, l_sw, l)
        lam = _lambda(lq_ref[...], lam_init)
        o = o1 - lam * o2
        o_ref[...] = (_rms_rows(o) * sub_ref[...] * (1.0 - lam_init)).astype(o_ref.dtype)


def _attn_decode(q, k_new, v_new, cache_k, cache_v, page_table, lambda_qk, subln, lam_init, layer, ppb=8):
    nb, nh, _ = q.shape
    n_pages = page_table.shape[1]
    assert n_pages % ppb == 0
    tok = pl.BlockSpec((None, nh, HEAD_DIM), lambda b, j, pt: (b, 0, 0))

    def page(r):
        return pl.BlockSpec((None, None, PAGE_SIZE, nh, HEAD_DIM),
                            lambda b, j, pt: (pt[b, j * ppb + r], layer, 0, 0, 0))

    pages = [page(r) for r in range(ppb)]
    grid_spec = pltpu.PrefetchScalarGridSpec(
        num_scalar_prefetch=1,
        grid=(nb, n_pages // ppb),
        in_specs=[tok, tok, tok] + pages + pages +
                 [pl.BlockSpec((4, HALF_DIM), lambda b, j, pt: (0, 0)),
                  pl.BlockSpec((1, HEAD_DIM), lambda b, j, pt: (0, 0))],
        out_specs=tok,
        scratch_shapes=[pltpu.VMEM((nh, HEAD_DIM), F32),
                        pltpu.VMEM((nh, HEAD_DIM), F32),
                        pltpu.VMEM((nh, 2 * HEAD_DIM), F32),
                        pltpu.VMEM((ppb, PAGE_SIZE, nh, HEAD_DIM), F32)],
    )
    return pl.pallas_call(
        functools.partial(_attn_decode_kernel, lam_init, ppb),
        out_shape=jax.ShapeDtypeStruct((nb, nh, HEAD_DIM), BF16),
        grid_spec=grid_spec,
        compiler_params=_cparams("arbitrary", "arbitrary"),
        name="attn_decode",
    )(page_table, q, k_new, v_new, *([cache_k] * ppb), *([cache_v] * ppb), lambda_qk, subln.reshape(1, HEAD_DIM))


def _xpos_tables(pos):
    half = HEAD_DIM // 2
    angle = jnp.repeat(1.0 / (10000.0 ** jnp.linspace(0.0, 1.0, half, dtype=F32)), 2)
    ph = pos.astype(F32)[:, None] * angle
    sin, cos = jnp.sin(ph), jnp.cos(ph)
    even = (jnp.arange(HEAD_DIM) % 2) == 0
    return cos, jnp.where(even, -sin, 0.0), jnp.where(even, 0.0, sin)


def kernel(x_prompt, x_sample, state_conv_a, state_delta, state_ret, cache_k, cache_v, page_table, c_prompt, c_sample, w_ada, b_ada, w_in, conv_a, a_log, dt_bias, norm_a, lambda_qk, subln_c, w_out, ln1_g, ln1_b, w_up, w_down, ln2_g, ln2_b):
    bp, t, d = x_prompt.shape
    nb = x_sample.shape[0]
    depth = w_in.shape[0]
    h_a = state_delta.shape[2]
    h_b = state_ret.shape[2]
    h_c = cache_k.shape[3]
    a_w, b_w, c_w = h_a * HEAD_DIM, h_b * HEAD_DIM, h_c * HEAD_DIM
    n_main = 4 * a_w
    n_gate = 2 * h_a
    past_len = page_table.shape[1] * PAGE_SIZE
    alpha = (2 * depth) ** 0.25
    rs = SAMPLE_ROWS
    assert nb + bp <= rs and x_sample.shape[1] == 1

    c_all = jnp.concatenate([c_sample, c_prompt, jnp.zeros((rs - nb - bp, d), F32)], axis=0)
    mod = _ada(c_all, w_ada, b_ada)

    def mod_p(l, i):
        return mod[l, nb:nb + bp, None, i * d:(i + 1) * d]

    def mod_s(l, i):
        return mod[l][None, :, i * d:(i + 1) * d]

    cos_p, sine_p, sino_p = _xpos_tables(jnp.arange(t))
    cos_s, sine_s, sino_s = _xpos_tables(past_len + jnp.arange(1))

    xp = x_prompt
    xs = jnp.concatenate([x_sample.reshape(nb, d), jnp.zeros((rs - nb, d), F32)], axis=0)[None]
    hp = _modulate(xp, mod_p(0, 1), mod_p(0, 0), 256)
    hs = _modulate(xs, mod_s(0, 1), mod_s(0, 0), rs)

    w_in_t = jnp.swapaxes(w_in, 1, 2)
    n_tail = w_in.shape[2] - n_main - n_gate
    w_gate_b = w_in[:, :, n_main:n_main + n_gate].astype(BF16)
    w_down_b = w_down.astype(BF16)

    st_p, st_s = [], []
    for l in range(depth):
        lam_init = 0.8 - 0.6 * math.exp(-0.3 * l)
        hp2 = hp.reshape(bp * t, d)
        hs2 = hs.reshape(rs, d)
        zp_main, zs_main = _panel_matmul(hp2, hs2, w_in_t, l, 0, n_main, transposed=True)
        zp_gate, zs_gate = _matmul(hp2, hs2, w_gate_b, l)
        zp_tail, zs_tail = _panel_matmul(hp2, hs2, w_in_t, l, n_main + n_gate, n_tail, transposed=True)
        zp_main = zp_main.reshape(bp, t, n_main)
        zp_tail = zp_tail.reshape(bp, t, -1)

        gc, beta = _gates(zp_gate.reshape(bp, t, n_gate), a_log[l], dt_bias[l])
        mix_p, delta_p = _delta_prompt(zp_main, conv_a[l], gc, beta, norm_a[l], h_a, a_w + b_w + c_w)
        mix_p, ret_p = _ret_prompt(zp_tail, cos_p, sine_p, sino_p, mix_p, a_w, h_b)
        mix_p = _attn_prompt(zp_tail, lambda_qk[l], subln_c[l], lam_init, mix_p, a_w + b_w, h_c, 4 * h_b)
        k_off = 4 * b_w + c_w
        conv_p = zp_main[:, t - (CONV_W - 1):, :3 * a_w]
        k_p = zp_tail[:, :, k_off:k_off + c_w]
        v_p = zp_tail[:, :, k_off + c_w:k_off + 2 * c_w]
        st_p.append((conv_p, delta_p, ret_p, k_p, v_p))

        zs_m = zs_main[:nb]
        zs_t = zs_tail[:nb]
        oa_s, delta_s = _delta_sample(zs_m[:, None], zs_gate[:nb, None], state_conv_a, conv_a[l], a_log[l],
                                      dt_bias[l], norm_a[l], state_delta, l, h_a)
        ob_s, ret_s = _ret_sample(zs_t[:, None], cos_s, sine_s, sino_s, state_ret, l, h_b)
        q_s = zs_t[:, 4 * b_w:4 * b_w + c_w].reshape(nb, h_c, HEAD_DIM)
        k_s = zs_t[:, k_off:k_off + c_w].reshape(nb, h_c, HEAD_DIM)
        v_s = zs_t[:, k_off + c_w:k_off + 2 * c_w].reshape(nb, h_c, HEAD_DIM)
        oc_s = _attn_decode(q_s, k_s, v_s, cache_k, cache_v, page_table, lambda_qk[l], subln_c[l], lam_init, l)
        conv_s = jnp.concatenate([state_conv_a[:, l, 1:], zs_m[:, None, :3 * a_w]], axis=1)
        st_s.append((conv_s, delta_s, ret_s, k_s[:, None], v_s[:, None]))

        mix_p = mix_p.reshape(bp * t, -1)
        mix_s = jnp.concatenate([oa_s.reshape(nb, a_w), ob_s.reshape(nb, b_w), oc_s.reshape(nb, c_w)], axis=-1)
        mix_s = jnp.concatenate([mix_s, jnp.zeros((rs - nb, mix_s.shape[1]), BF16)], axis=0)
        yp, ys = _panel_matmul(mix_p, mix_s, w_out, l, 0, d)
        xp, hp = _postnorm(xp, yp.reshape(bp, t, d), mod_p(l, 2), ln1_g[l], ln1_b[l], mod_p(l, 4), mod_p(l, 3),
                           alpha, 256)
        xs, hs = _postnorm(xs, ys[None], mod_s(l, 2), ln1_g[l], ln1_b[l], mod_s(l, 4), mod_s(l, 3), alpha, rs)

        up, us = _panel_matmul(hp.reshape(bp * t, d), hs.reshape(rs, d), w_up, l, 0, w_up.shape[2],
                               out_dtype=BF16, relu2=True)
        fp, fs = _matmul(up, us, w_down_b, l)
        last = l == depth - 1
        nl = min(l + 1, depth - 1)
        xp, hp = _postnorm(xp, fp.reshape(bp, t, d), mod_p(l, 5), ln2_g[l], ln2_b[l], mod_p(nl, 1), mod_p(nl, 0),
                           alpha, 256, with_h=not last)
        xs, hs = _postnorm(xs, fs[None], mod_s(l, 5), ln2_g[l], ln2_b[l], mod_s(nl, 1), mod_s(nl, 0), alpha, rs,
                           with_h=not last)

    outs_p = [jnp.stack([s[i] for s in st_p], axis=1) for i in range(5)]
    outs_p[3:] = [o.reshape(bp, depth, t, h_c, HEAD_DIM) for o in outs_p[3:]]
    outs_p = tuple(outs_p)
    outs_s = tuple(jnp.stack([s[i] for s in st_s], axis=1) for i in range(5))
    y_sample = xs[0, :nb].reshape(nb, 1, d)
    return (xp, y_sample) + outs_p + outs_s
```

```python
import functools
import math

import jax
import jax.numpy as jnp
from jax import lax
from jax.experimental import pallas as pl
from jax.experimental.pallas import tpu as pltpu

F32 = jnp.float32
BF16 = jnp.bfloat16

SUBLANES = 8
HEAD_DIM = 128
HALF_DIM = HEAD_DIM // 2
CONV_W = 4
DELTA_CHUNK = 64
DELTA_SUPER = 256
DELTA_BASE = 4
RET_CHUNK = 128
PAGE_SIZE = 128
EPS = 1e-5
SAMPLE_ROWS = 16
VMEM_LIMIT = 58 * 1024 * 1024


def _cparams(*sem):
    return pltpu.CompilerParams(dimension_semantics=sem, vmem_limit_bytes=VMEM_LIMIT)


def _sigmoid(x):
    return 1.0 / (1.0 + jnp.exp(-x))


def _silu(x):
    return x * _sigmoid(x)


def _softplus(x):
    return jnp.maximum(x, 0.0) + jnp.log1p(jnp.exp(-jnp.abs(x)))


def _dot(a, b):
    return jnp.dot(a.astype(BF16), b.astype(BF16), preferred_element_type=F32)


def _dot_nt(a, b):
    return lax.dot_general(a.astype(BF16), b.astype(BF16), (((1,), (1,)), ((), ())),
                           preferred_element_type=F32)


def _rms_rows(x):
    return x * lax.rsqrt(jnp.mean(x * x, axis=-1, keepdims=True) + EPS)


def _ada_kernel(c_ref, w0_ref, w1_ref, b_ref, o_ref):
    c = _silu(c_ref[...])
    dh = w0_ref.shape[0]
    o_ref[...] = _dot(c[:, :dh], w0_ref[...]) + _dot(c[:, dh:], w1_ref[...]) + b_ref[...]


def _ada(c_all, w_ada, b_ada, tn=1024):
    depth, d, n = w_ada.shape
    rows = c_all.shape[0]
    return pl.pallas_call(
        _ada_kernel,
        out_shape=jax.ShapeDtypeStruct((depth, rows, n), F32),
        grid=(depth, n // tn),
        in_specs=[pl.BlockSpec((rows, d), lambda l, j: (0, 0)),
                  pl.BlockSpec((None, d // 2, tn), lambda l, j: (l, 0, j)),
                  pl.BlockSpec((None, d // 2, tn), lambda l, j: (l, 1, j)),
                  pl.BlockSpec((None, 1, tn), lambda l, j: (l, 0, j))],
        out_specs=pl.BlockSpec((None, rows, tn), lambda l, j: (l, 0, j)),
        compiler_params=_cparams("arbitrary", "arbitrary"),
        name="ada_mod",
    )(c_all, w_ada, w_ada, b_ada.reshape(depth, 1, n))


def _modulate_kernel(x_ref, sc_ref, sh_ref, o_ref):
    o_ref[...] = (x_ref[...] * (1.0 + sc_ref[...]) + sh_ref[...]).astype(o_ref.dtype)


def _modulate(x, sc, sh, tm):
    g, r, d = x.shape
    rm = sc.shape[1]
    mspec = pl.BlockSpec((None, rm, d), lambda a, i: (a, 0, 0))
    return pl.pallas_call(
        _modulate_kernel,
        out_shape=jax.ShapeDtypeStruct((g, r, d), BF16),
        grid=(g, r // tm),
        in_specs=[pl.BlockSpec((None, tm, d), lambda a, i: (a, i, 0)), mspec, mspec],
        out_specs=pl.BlockSpec((None, tm, d), lambda a, i: (a, i, 0)),
        compiler_params=_cparams("arbitrary", "arbitrary"),
        name="modulate",
    )(x, sc, sh)


def _postnorm_kernel(alpha, with_h, x_ref, y_ref, gate_ref, lg_ref, lb_ref, sc_ref, sh_ref, xo_ref, *h_ref):
    v = alpha * x_ref[...] + (1.0 + gate_ref[...]) * y_ref[...]
    vc = v - jnp.mean(v, axis=-1, keepdims=True)
    var = jnp.mean(vc * vc, axis=-1, keepdims=True)
    xn = vc * lax.rsqrt(var + EPS) * lg_ref[...] + lb_ref[...]
    xo_ref[...] = xn
    if with_h:
        h_ref[0][...] = (xn * (1.0 + sc_ref[...]) + sh_ref[...]).astype(BF16)


def _postnorm(x, y, gate, ln_g, ln_b, sc, sh, alpha, tm, with_h=True):
    g, r, d = x.shape
    rm = gate.shape[1]
    row = pl.BlockSpec((None, tm, d), lambda a, i: (a, i, 0))
    mspec = pl.BlockSpec((None, rm, d), lambda a, i: (a, 0, 0))
    vec = pl.BlockSpec((1, d), lambda a, i: (0, 0))
    out_shape = [jax.ShapeDtypeStruct((g, r, d), F32)]
    out_specs = [row]
    if with_h:
        out_shape.append(jax.ShapeDtypeStruct((g, r, d), BF16))
        out_specs.append(row)
    res = pl.pallas_call(
        functools.partial(_postnorm_kernel, alpha, with_h),
        out_shape=out_shape,
        grid=(g, r // tm),
        in_specs=[row, row, mspec, vec, vec, mspec, mspec],
        out_specs=out_specs,
        compiler_params=_cparams("arbitrary", "arbitrary"),
        name="postnorm",
    )(x, y, gate, ln_g.reshape(1, d), ln_b.reshape(1, d), sc, sh)
    return (res[0], res[1]) if with_h else (res[0], None)


def _matmul_kernel(relu2, nk, a_ref, as_ref, w_ref, o_ref, os_ref, *scratch):
    i = pl.program_id(1)
    k = pl.program_id(2)

    def finish(v):
        return jnp.square(jnp.maximum(v, 0.0)) if relu2 else v

    if nk == 1:
        o_ref[...] = finish(jnp.dot(a_ref[...], w_ref[...], preferred_element_type=F32)).astype(o_ref.dtype)

        @pl.when(i == 0)
        def _():
            os_ref[...] = finish(jnp.dot(as_ref[...], w_ref[...], preferred_element_type=F32)).astype(os_ref.dtype)
    else:
        acc_ref, accs_ref = scratch

        @pl.when(k == 0)
        def _():
            acc_ref[...] = jnp.zeros_like(acc_ref)

        acc_ref[...] += jnp.dot(a_ref[...], w_ref[...], preferred_element_type=F32)

        @pl.when(k == nk - 1)
        def _():
            o_ref[...] = finish(acc_ref[...]).astype(o_ref.dtype)

        @pl.when(i == 0)
        def _():
            @pl.when(k == 0)
            def _():
                accs_ref[...] = jnp.zeros_like(accs_ref)

            accs_ref[...] += jnp.dot(as_ref[...], w_ref[...], preferred_element_type=F32)

            @pl.when(k == nk - 1)
            def _():
                os_ref[...] = finish(accs_ref[...]).astype(os_ref.dtype)


def _matmul(a, a_s, w3, layer, col0=0, n=None, out_dtype=F32, relu2=False, tm=1024, tn=1024, tk=4096):
    m, kdim = a.shape
    n = w3.shape[2] if n is None else n
    rs = a_s.shape[0]
    tn = min(tn, n)
    tk = min(tk, kdim)
    nk = kdim // tk
    assert n % tn == 0 and col0 % tn == 0 and m % tm == 0 and kdim % tk == 0
    cb0 = col0 // tn
    scratch = [] if nk == 1 else [pltpu.VMEM((tm, tn), F32), pltpu.VMEM((rs, tn), F32)]
    return pl.pallas_call(
        functools.partial(_matmul_kernel, relu2, nk),
        out_shape=[jax.ShapeDtypeStruct((m, n), out_dtype), jax.ShapeDtypeStruct((rs, n), out_dtype)],
        grid=(n // tn, m // tm, nk),
        in_specs=[pl.BlockSpec((tm, tk), lambda j, i, k: (i, k)),
                  pl.BlockSpec((rs, tk), lambda j, i, k: (0, k)),
                  pl.BlockSpec((None, tk, tn), lambda j, i, k: (layer, k, cb0 + j))],
        out_specs=[pl.BlockSpec((tm, tn), lambda j, i, k: (i, j)),
                   pl.BlockSpec((rs, tn), lambda j, i, k: (0, j))],
        scratch_shapes=scratch,
        compiler_params=_cparams("arbitrary", "arbitrary", "arbitrary"),
        name="matmul_relu2" if relu2 else "matmul",
    )(a, a_s, w3)


def _panel_matmul_kernel(relu2, n_panels, transposed, a_ref, as_ref, w_ref, o_ref, os_ref, wb_ref):
    jj = pl.program_id(0)
    i = pl.program_id(1)
    rows = w_ref.shape[0]

    def product(x, w):
        if transposed:
            v = lax.dot_general(x, w, (((1,), (1,)), ((), ())), preferred_element_type=F32)
        else:
            v = jnp.dot(x, w, preferred_element_type=F32)
        return jnp.square(jnp.maximum(v, 0.0)) if relu2 else v

    @pl.when(jj < n_panels)
    def _():
        wb_ref[jj % 2, pl.ds(pl.multiple_of(i * rows, rows), rows), :] = w_ref[...].astype(BF16)

    @pl.when(jj > 0)
    def _():
        w = wb_ref[(jj + 1) % 2]
        o_ref[...] = product(a_ref[...], w).astype(o_ref.dtype)

        @pl.when(i == 0)
        def _():
            os_ref[...] = product(as_ref[...], w).astype(os_ref.dtype)


def _panel_matmul(a, a_s, w3, layer, col0, n, transposed=False, out_dtype=F32, relu2=False, tm=1024, tn=1024):
    m, kdim = a.shape
    rs = a_s.shape[0]
    n_i = m // tm
    n_panels = n // tn
    assert n % tn == 0 and m % tm == 0

    def clamp(jj, i):
        last = jj == n_panels
        return jnp.minimum(jj, n_panels - 1), jnp.where(last, n_i - 1, i)

    if transposed:
        assert tn % (SUBLANES * n_i) == 0 and col0 % SUBLANES == 0
        rows = tn // n_i

        def w_index(jj, i):
            panel, chunk = clamp(jj, i)
            return (layer, pl.multiple_of(col0 + panel * tn + chunk * rows, SUBLANES), 0)

        w_spec = pl.BlockSpec((None, pl.Element(rows), pl.Element(kdim)), w_index)
        wb_shape = (2, tn, kdim)
    else:
        assert kdim % (SUBLANES * n_i) == 0 and col0 % tn == 0
        rows = kdim // n_i

        def w_index(jj, i):
            panel, chunk = clamp(jj, i)
            return (layer, chunk, col0 // tn + panel)

        w_spec = pl.BlockSpec((None, rows, tn), w_index)
        wb_shape = (2, kdim, tn)

    def row_block(jj, i):
        return jnp.where(jj == 0, 0, i)

    return pl.pallas_call(
        functools.partial(_panel_matmul_kernel, relu2, n_panels, transposed),
        out_shape=[jax.ShapeDtypeStruct((m, n), out_dtype), jax.ShapeDtypeStruct((rs, n), out_dtype)],
        grid=(n_panels + 1, n_i),
        in_specs=[pl.BlockSpec((tm, kdim), lambda jj, i: (row_block(jj, i), 0)),
                  pl.BlockSpec((rs, kdim), lambda jj, i: (0, 0)), w_spec],
        out_specs=[pl.BlockSpec((tm, tn), lambda jj, i: (row_block(jj, i), jnp.maximum(jj - 1, 0))),
                   pl.BlockSpec((rs, tn), lambda jj, i: (0, jnp.maximum(jj - 1, 0)))],
        scratch_shapes=[pltpu.VMEM(wb_shape, BF16)],
        compiler_params=_cparams("arbitrary", "arbitrary"),
        name="panel_matmul_relu2" if relu2 else "panel_matmul",
    )(a, a_s, w3)


def _gates_kernel(n_heads, zg_ref, alog_ref, dtb_ref, gc_ref, beta_ref):
    zg = zg_ref[...]
    rows = zg.shape[0]
    g = -jnp.exp(alog_ref[...]) * _softplus(zg[:, :n_heads] + dtb_ref[...])
    beta = _sigmoid(zg[:, n_heads:])
    pos = lax.broadcasted_iota(jnp.int32, (rows, HEAD_DIM), 0) % DELTA_CHUNK
    for h in range(n_heads):
        gh = jnp.broadcast_to(g[:, h:h + 1], (rows, HEAD_DIM))
        s = 1
        while s < DELTA_CHUNK:
            gh = gh + jnp.where(pos >= s, pltpu.roll(gh, s, 0), 0.0)
            s *= 2
        gc_ref[h] = gh
        beta_ref[h] = jnp.broadcast_to(beta[:, h:h + 1], (rows, HEAD_DIM))


def _gates(zg, a_log, dt_bias, rows=256):
    b, t, h2 = zg.shape
    nh = h2 // 2
    out = jax.ShapeDtypeStruct((b, nh, t, HEAD_DIM), F32)
    ospec = pl.BlockSpec((None, nh, rows, HEAD_DIM), lambda a, i: (a, 0, i, 0))
    vec = pl.BlockSpec((1, nh), lambda a, i: (0, 0))
    return pl.pallas_call(
        functools.partial(_gates_kernel, nh),
        out_shape=[out, out],
        grid=(b, t // rows),
        in_specs=[pl.BlockSpec((None, rows, h2), lambda a, i: (a, i, 0)), vec, vec],
        out_specs=[ospec, ospec],
        compiler_params=_cparams("arbitrary", "arbitrary"),
        name="delta_gates",
    )(zg, a_log.reshape(1, nh), dt_bias.reshape(1, nh))


def _delta_prompt_kernel(nh, q_ref, k_ref, v_ref, za_ref, cq_ref, ck_ref, cv_ref, gc_ref, beta_ref, norm_ref,
                         o_ref, s_ref, wq_ref, ka_ref, u_ref, eg_ref):
    t = q_ref.shape[0]
    sb = DELTA_SUPER
    c = DELTA_CHUNK
    n_sb = t // sb
    cps = sb // c
    half = sb // 2
    ii = lax.broadcasted_iota(jnp.int32, (sb, sb), 0)
    jj = lax.broadcasted_iota(jnp.int32, (sb, sb), 1)
    lower = ((ii // c) == (jj // c)) & (ii >= jj)
    strict = ii > jj
    row = lax.broadcasted_iota(jnp.int32, (sb, HEAD_DIM), 0)
    pair_chunk = lax.broadcasted_iota(jnp.int32, (HEAD_DIM, 2 * c), 1) // c
    blk_xor = ii ^ jj
    base = DELTA_BASE
    base_levels = int(math.log2(base))

    def l2norm(x):
        return x * lax.rsqrt(jnp.sum(x * x, axis=-1, keepdims=True) + 1e-6)

    def prepare(n, hh):
        static = isinstance(n, int)
        r0 = n * sb if static else pl.multiple_of(n * sb, sb)
        rows = pl.ds(r0, sb)
        lanes = slice(hh * HEAD_DIM, (hh + 1) * HEAD_DIM)

        def conv_silu(x_ref, w_ref):
            x = x_ref[rows, lanes]
            w = w_ref[:, lanes]
            if static and n == 0:
                prev = jnp.zeros((SUBLANES, HEAD_DIM), F32)
            else:
                p0 = r0 - SUBLANES
                prev = x_ref[pl.ds(p0 if static else pl.multiple_of(p0, SUBLANES), SUBLANES), lanes]
            y = x * w[CONV_W - 1:CONV_W]
            for s in range(1, CONV_W):
                head = jnp.broadcast_to(pltpu.roll(prev, s, 0)[None], (sb // SUBLANES, SUBLANES, HEAD_DIM))
                xs = jnp.where(row < s, head.reshape(sb, HEAD_DIM), pltpu.roll(x, s, 0))
                y = y + xs * w[CONV_W - 1 - s:CONV_W - s]
            return _silu(y)

        q = l2norm(conv_silu(q_ref, cq_ref)) * (HEAD_DIM ** -0.5)
        k = l2norm(conv_silu(k_ref, ck_ref))
        v = conv_silu(v_ref, cv_ref)
        gc = gc_ref[hh, rows]
        beta = beta_ref[hh, rows]
        kb = k * beta
        eg = jnp.exp(gc)
        gct = jnp.concatenate([gc[:half].T, gc[half:].T], axis=1)
        gc_j = jnp.broadcast_to(gct[0:1], (sb, sb))
        gc_i = jnp.concatenate([gc] * (sb // HEAD_DIM), axis=1)
        dec = jnp.exp(jnp.where(lower, gc_i - gc_j, -jnp.inf))
        kq = _dot_nt(jnp.concatenate([kb, q], axis=0), k)
        yield
        mm = jnp.where(strict, kq[:sb] * dec, 0.0)
        attn = kq[sb:] * dec
        x = jnp.where(blk_xor < base, -mm, 0.0)
        p = jnp.where(ii == jj, 1.0, x)
        xb = x.astype(BF16)
        x = jnp.dot(xb, xb, preferred_element_type=F32)
        yield
        for lvl in range(1, base_levels):
            xb = x.astype(BF16)
            if lvl < base_levels - 1:
                y = jnp.dot(jnp.concatenate([p.astype(BF16), xb], axis=0), xb, preferred_element_type=F32)
                yield
                p = p + y[:sb]
                x = y[sb:]
            else:
                y = jnp.dot(p.astype(BF16), xb, preferred_element_type=F32)
                yield
                p = p + y
        size = base
        while size < c:
            pb = p.astype(BF16)
            off = jnp.where((blk_xor >= size) & (blk_xor < 2 * size), mm, 0.0).astype(BF16)
            e = jnp.dot(off, pb, preferred_element_type=F32)
            yield
            y = jnp.dot(pb, e.astype(BF16), preferred_element_type=F32)
            yield
            p = p - y
            size *= 2
        rhs = jnp.concatenate([v * beta, kb * eg], axis=1)
        pb = p.astype(BF16)
        uw = jnp.dot(pb, rhs.astype(BF16), preferred_element_type=F32)
        yield
        mm_hi = mm.astype(BF16)
        mm_lo = (mm - mm_hi.astype(F32)).astype(BF16)
        uw_hi = uw.astype(BF16)
        uw_lo = (uw - uw_hi.astype(F32)).astype(BF16)
        resid = rhs - uw - (jnp.dot(mm_hi, uw_hi, preferred_element_type=F32)
                            + jnp.dot(mm_hi, uw_lo, preferred_element_type=F32)
                            + jnp.dot(mm_lo, uw_hi, preferred_element_type=F32))
        yield
        uw = uw + jnp.dot(pb, resid.astype(BF16), preferred_element_type=F32)
        yield
        u_ref[hh, rows] = uw[:, :HEAD_DIM]
        w = uw[:, HEAD_DIM:]
        qd = q * eg
        gl = jnp.concatenate([jnp.broadcast_to(gc[ci * c + c - 1:ci * c + c], (c, HEAD_DIM))
                              for ci in range(cps)], axis=0)
        kd = k * jnp.exp(gl - gc)
        kdt = jnp.concatenate([kd[:half].T, kd[half:].T], axis=1)
        for ci in range(cps):
            cc = n * cps + ci
            cr = slice(ci * c, (ci + 1) * c)
            pair = slice((ci // 2) * 2 * c, (ci // 2 + 1) * 2 * c)
            wq_ref[hh, cc] = jnp.concatenate([w[cr], qd[cr]], axis=0).astype(BF16)
            ka_ref[hh, cc] = jnp.concatenate([jnp.where(pair_chunk == ci % 2, kdt[:, pair], 0.0),
                                              attn[cr, pair]], axis=0).astype(BF16)
            eg_ref[hh, cc] = jnp.exp(gl[ci * c:ci * c + SUBLANES])

    def recur(n, states):
        static = isinstance(n, int)
        for ci in range(cps):
            cc = n * cps + ci
            crow = pl.ds(cc * c if static else pl.multiple_of(cc * c, c), c)
            r1 = [jnp.dot(wq_ref[hh, cc], states[hh].astype(BF16), preferred_element_type=F32) for hh in range(nh)]
            yield
            r2 = []
            for hh in range(nh):
                v_new = (u_ref[hh, crow] - r1[hh][:c]).astype(BF16)
                r2.append(jnp.dot(ka_ref[hh, cc], jnp.concatenate([v_new, v_new], axis=0),
                                  preferred_element_type=F32))
            yield
            for hh in range(nh):
                lanes = slice(hh * HEAD_DIM, (hh + 1) * HEAD_DIM)
                o = r1[hh][c:] + r2[hh][HEAD_DIM:]
                states[hh] = states[hh] * eg_ref[hh, cc][0:1] + r2[hh][:HEAD_DIM]
                o_ref[crow, lanes] = (_rms_rows(o) * norm_ref[...] * _silu(za_ref[crow, lanes])).astype(o_ref.dtype)

    def interleave(gens):
        live = list(gens)
        while live:
            live = [g for g in live if next(g, live) is not live]

    def chain(*gens):
        for g in gens:
            yield from g

    bpi = 2
    assert n_sb % bpi == 0

    def body(it, states):
        states = list(states)
        first = it * bpi
        interleave([prepare(first + r, hh) for r in range(bpi) for hh in range(nh)]
                   + [chain(*[recur(first - bpi + r, states) for r in range(bpi)])])
        return tuple(states)

    interleave([prepare(r, hh) for r in range(bpi) for hh in range(nh)])
    states = lax.fori_loop(1, n_sb // bpi, body, tuple(jnp.zeros((HEAD_DIM, HEAD_DIM), F32) for _ in range(nh)))
    states = list(states)
    interleave([chain(*[recur(n_sb - bpi + r, states) for r in range(bpi)])])
    for hh in range(nh):
        s_ref[hh] = states[hh]


def _delta_prompt(z_main, conv_w, gc, beta, norm_w, n_heads, mix_width, hpb=2):
    b, t, _ = z_main.shape
    wl = hpb * HEAD_DIM
    ng = n_heads // hpb
    n_chunks = t // DELTA_CHUNK
    col = lambda off: pl.BlockSpec((None, t, wl), lambda a, h: (a, 0, off + h))
    cw = lambda off: pl.BlockSpec((CONV_W, wl), lambda a, h: (0, off + h))
    gate = pl.BlockSpec((None, hpb, t, HEAD_DIM), lambda a, h: (a, h, 0, 0))
    return pl.pallas_call(
        functools.partial(_delta_prompt_kernel, hpb),
        out_shape=[jax.ShapeDtypeStruct((b, t, mix_width), BF16),
                   jax.ShapeDtypeStruct((b, n_heads, HEAD_DIM, HEAD_DIM), F32)],
        grid=(b, ng),
        in_specs=[col(0), col(ng), col(2 * ng), col(3 * ng),
                  cw(0), cw(ng), cw(2 * ng), gate, gate,
                  pl.BlockSpec((1, HEAD_DIM), lambda a, h: (0, 0))],
        out_specs=[pl.BlockSpec((None, t, wl), lambda a, h: (a, 0, h)),
                   pl.BlockSpec((None, hpb, HEAD_DIM, HEAD_DIM), lambda a, h: (a, h, 0, 0))],
        scratch_shapes=[pltpu.VMEM((hpb, n_chunks, 2 * DELTA_CHUNK, HEAD_DIM), BF16),
                        pltpu.VMEM((hpb, n_chunks, HEAD_DIM + DELTA_CHUNK, 2 * DELTA_CHUNK), BF16),
                        pltpu.VMEM((hpb, t, HEAD_DIM), F32),
                        pltpu.VMEM((hpb, n_chunks, SUBLANES, HEAD_DIM), F32)],
        compiler_params=_cparams("arbitrary", "arbitrary"),
        name="delta_prompt",
    )(z_main, z_main, z_main, z_main, conv_w, conv_w, conv_w, gc, beta, norm_w.reshape(1, HEAD_DIM))


def _xpos(x, cos, sin_e, sin_o):
    return x * cos + pltpu.roll(x, HEAD_DIM - 1, 1) * sin_e + pltpu.roll(x, 1, 1) * sin_o


def _log_gamma(h):
    return jnp.log1p(-jnp.exp2(-5.0 - h))


def _ret_prompt_kernel(q_ref, k_ref, v_ref, g_ref, cos_ref, sine_ref, sino_ref, mix_ref, o_ref, s_ref,
                       qs_ref, ks_ref):
    del mix_ref
    t = q_ref.shape[0]
    c = RET_CHUNK
    cos, sin_e, sin_o = cos_ref[...], sine_ref[...], sino_ref[...]
    qs_ref[...] = _xpos(q_ref[...], cos, sin_e, sin_o)
    ks_ref[...] = _xpos(k_ref[...], cos, sin_e, sin_o) * (HEAD_DIM ** -0.5)

    h = jnp.full((1, 1), pl.program_id(1), jnp.int32).astype(F32)
    lg = _log_gamma(h)
    ii = lax.broadcasted_iota(jnp.int32, (c, c), 0)
    jj = lax.broadcasted_iota(jnp.int32, (c, c), 1)
    dec = jnp.exp(jnp.where(ii >= jj, (ii - jj).astype(F32) * lg, -jnp.inf))
    pos = lax.broadcasted_iota(jnp.int32, (c, HEAD_DIM), 0).astype(F32)
    e_in = jnp.exp((pos + 1.0) * lg)
    e_out = jnp.exp((c - 1.0 - pos) * lg)
    e_all = jnp.exp(c * lg)

    unroll = 2

    def body(n, s):
        rows = [pl.ds(pl.multiple_of((n * unroll + r) * c, c), c) for r in range(unroll)]
        q = [qs_ref[rw] for rw in rows]
        k = [ks_ref[rw] for rw in rows]
        v = [v_ref[rw].astype(BF16) for rw in rows]
        qk = [_dot_nt(q[r], k[r]) for r in range(unroll)]
        upd = [_dot((k[r] * e_out).T, v[r]) for r in range(unroll)]
        for r in range(unroll):
            o = _dot(q[r] * e_in, s) + _dot(qk[r] * dec, v[r])
            s = s * e_all + upd[r]
            o_ref[rows[r]] = (_rms_rows(o) * _silu(g_ref[rows[r]])).astype(o_ref.dtype)
        return s

    s_ref[...] = lax.fori_loop(0, t // (c * unroll), body, jnp.zeros((HEAD_DIM, HEAD_DIM), F32))


def _ret_prompt(z_tail, cos, sin_e, sin_o, mix, mix_col0, n_heads):
    b, t, _ = z_tail.shape
    c0 = mix_col0 // HEAD_DIM
    col = lambda off: pl.BlockSpec((None, t, HEAD_DIM), lambda a, h: (a, 0, off + h))
    tab = pl.BlockSpec((t, HEAD_DIM), lambda a, h: (0, 0))
    return pl.pallas_call(
        _ret_prompt_kernel,
        out_shape=[jax.ShapeDtypeStruct(mix.shape, mix.dtype),
                   jax.ShapeDtypeStruct((b, n_heads, HEAD_DIM, HEAD_DIM), F32)],
        grid=(b, n_heads),
        in_specs=[col(0), col(n_heads), col(2 * n_heads), col(3 * n_heads), tab, tab, tab,
                  pl.BlockSpec(memory_space=pl.ANY)],
        out_specs=[pl.BlockSpec((None, t, HEAD_DIM), lambda a, h: (a, 0, c0 + h)),
                   pl.BlockSpec((None, None, HEAD_DIM, HEAD_DIM), lambda a, h: (a, h, 0, 0))],
        scratch_shapes=[pltpu.VMEM((t, HEAD_DIM), F32)] * 2,
        input_output_aliases={7: 0},
        compiler_params=_cparams("arbitrary", "arbitrary"),
        name="ret_prompt",
    )(z_tail, z_tail, z_tail, z_tail, cos, sin_e, sin_o, mix)


def _lambda(lq, lam_init):
    l01 = jnp.sum(lq[0:1] * lq[1:2], axis=-1, keepdims=True)
    l23 = jnp.sum(lq[2:3] * lq[3:4], axis=-1, keepdims=True)
    return jnp.exp(l01) - jnp.exp(l23) + lam_init


def _attn_prompt_kernel(lam_init, nh, tq, q_ref, k_ref, v_ref, lq_ref, sub_ref, mix_ref, o_ref,
                        q2_ref, s_ref, p_ref, m_ref, l_ref, acc_ref):
    del mix_ref
    i = pl.program_id(2)
    strip = 64
    lane = lax.broadcasted_iota(jnp.int32, (tq, HEAD_DIM), 1)
    scale = HALF_DIM ** -0.5
    for hh in range(nh):
        q = q_ref[:, hh * HEAD_DIM:(hh + 1) * HEAD_DIM]
        q2_ref[hh] = (jnp.concatenate([jnp.where(lane < HALF_DIM, q, 0.0), jnp.where(lane >= HALF_DIM, q, 0.0)],
                                      axis=0) * scale).astype(BF16)
    m_ref[...] = jnp.full(m_ref.shape, -jnp.inf, F32)
    l_ref[...] = jnp.zeros_like(l_ref)
    acc_ref[...] = jnp.zeros_like(acc_ref)
    key_idx = lax.broadcasted_iota(jnp.int32, (strip, tq), 1)
    qry_idx = lax.broadcasted_iota(jnp.int32, (strip, tq), 0)

    def step(j, diagonal):
        rows = pl.ds(pl.multiple_of(j * tq, tq), tq)
        for hh in range(nh):
            s_ref[hh] = _dot_nt(q2_ref[hh], k_ref[rows, hh * HEAD_DIM:(hh + 1) * HEAD_DIM])
        for hh in range(nh):
            for c in range(2 * tq // strip):
                sr = slice(c * strip, (c + 1) * strip)
                s = s_ref[hh, sr]
                if diagonal:
                    q0 = (c * strip) % tq
                    s = jnp.where(key_idx <= qry_idx + q0, s, -jnp.inf)
                m_old = m_ref[hh, sr]
                m_new = jnp.maximum(m_old, jnp.max(s, axis=-1, keepdims=True))
                alpha = jnp.exp(m_old - m_new)
                p = jnp.exp(s - jnp.concatenate([m_new] * (tq // HEAD_DIM), axis=1))
                l_ref[hh, sr] = alpha * l_ref[hh, sr] + jnp.sum(p, axis=-1, keepdims=True)
                m_ref[hh, sr] = m_new
                p_ref[hh, sr] = p.astype(BF16)
                acc_ref[hh, sr] = alpha * acc_ref[hh, sr]
            acc_ref[hh] += jnp.dot(p_ref[hh], v_ref[rows, hh * HEAD_DIM:(hh + 1) * HEAD_DIM].astype(BF16),
                                   preferred_element_type=F32)

    def body(j, carry):
        step(j, False)
        return carry

    lax.fori_loop(0, i, body, 0)
    step(i, True)
    lam = _lambda(lq_ref[...], lam_init)
    for hh in range(nh):
        o2 = acc_ref[hh] / l_ref[hh]
        o = o2[:tq] - lam * o2[tq:]
        o_ref[:, hh * HEAD_DIM:(hh + 1) * HEAD_DIM] = (_rms_rows(o) * sub_ref[...] * (1.0 - lam_init)).astype(o_ref.dtype)


def _attn_prompt(z_tail, lambda_qk, subln, lam_init, mix, mix_col0, n_heads, col0, tq=256, hpb=2):
    b, t, _ = z_tail.shape
    wl = hpb * HEAD_DIM
    ng = n_heads // hpb
    c0 = col0 // hpb
    m0 = mix_col0 // wl
    return pl.pallas_call(
        functools.partial(_attn_prompt_kernel, lam_init, hpb, tq),
        out_shape=jax.ShapeDtypeStruct(mix.shape, mix.dtype),
        grid=(b, ng, t // tq),
        in_specs=[pl.BlockSpec((None, tq, wl), lambda a, h, i: (a, i, c0 + h)),
                  pl.BlockSpec((None, t, wl), lambda a, h, i: (a, 0, c0 + ng + h)),
                  pl.BlockSpec((None, t, wl), lambda a, h, i: (a, 0, c0 + 2 * ng + h)),
                  pl.BlockSpec((4, HALF_DIM), lambda a, h, i: (0, 0)),
                  pl.BlockSpec((1, HEAD_DIM), lambda a, h, i: (0, 0)),
                  pl.BlockSpec(memory_space=pl.ANY)],
        out_specs=pl.BlockSpec((None, tq, wl), lambda a, h, i: (a, i, m0 + h)),
        scratch_shapes=[pltpu.VMEM((hpb, 2 * tq, HEAD_DIM), BF16),
                        pltpu.VMEM((hpb, 2 * tq, tq), F32),
                        pltpu.VMEM((hpb, 2 * tq, tq), BF16),
                        pltpu.VMEM((hpb, 2 * tq, HEAD_DIM), F32),
                        pltpu.VMEM((hpb, 2 * tq, HEAD_DIM), F32),
                        pltpu.VMEM((hpb, 2 * tq, HEAD_DIM), F32)],
        input_output_aliases={5: 0},
        compiler_params=_cparams("arbitrary", "arbitrary", "arbitrary"),
        name="attn_prompt",
    )(z_tail, z_tail, z_tail, lambda_qk, subln.reshape(1, HEAD_DIM), mix)


def _lane_pick(x, h):
    lane = lax.broadcasted_iota(jnp.int32, x.shape, 1)
    return jnp.sum(jnp.where(lane == h, x, 0.0), axis=-1, keepdims=True)


def _row_to_col(x):
    ii = lax.broadcasted_iota(jnp.int32, (HEAD_DIM, HEAD_DIM), 0)
    jj = lax.broadcasted_iota(jnp.int32, (HEAD_DIM, HEAD_DIM), 1)
    return jnp.sum(jnp.where(ii == jj, jnp.broadcast_to(x, (HEAD_DIM, HEAD_DIM)), 0.0), axis=-1, keepdims=True)


def _rows8(*rows):
    pad = jnp.zeros((SUBLANES - len(rows), HEAD_DIM), F32)
    return jnp.concatenate(list(rows) + [pad], axis=0)


def _delta_sample_kernel(hps, q_ref, k_ref, v_ref, za_ref, bq_ref, bk_ref, bv_ref, cq_ref, ck_ref, cv_ref,
                         zg_ref, alog_ref, dtb_ref, norm_ref, s_ref, o_ref, so_ref):
    n_heads = alog_ref.shape[1]
    zg = zg_ref[...]

    def l2norm(x):
        return x * lax.rsqrt(jnp.sum(x * x, axis=-1, keepdims=True) + 1e-6)

    for hh in range(hps):
        h = pl.program_id(1) * hps + hh
        lanes = slice(hh * HEAD_DIM, (hh + 1) * HEAD_DIM)

        def conv_silu(x_ref, buf_ref, w_ref):
            w = w_ref[:, lanes]
            buf = buf_ref[:, lanes]
            y = x_ref[:, lanes] * w[CONV_W - 1:CONV_W]
            for j in range(CONV_W - 1):
                y = y + buf[j:j + 1] * w[j:j + 1]
            return _silu(y)

        q = l2norm(conv_silu(q_ref, bq_ref, cq_ref)) * (HEAD_DIM ** -0.5)
        k = l2norm(conv_silu(k_ref, bk_ref, ck_ref))
        v = conv_silu(v_ref, bv_ref, cv_ref)
        a_a = _lane_pick(zg, h)
        b_a = _lane_pick(zg, h + n_heads)
        g = -jnp.exp(_lane_pick(alog_ref[...], h)) * _softplus(a_a + _lane_pick(dtb_ref[...], h))
        beta = _sigmoid(b_a)
        eg = jnp.exp(g)
        s = s_ref[hh]
        ks_qs = _dot(_rows8(k, q), s)
        v_new = beta * (v - eg * ks_qs[0:1])
        qk = jnp.sum(q * k, axis=-1, keepdims=True)
        o = eg * ks_qs[1:2] + qk * v_new
        so_ref[hh] = s * eg + _row_to_col(k) * v_new
        o_ref[:, lanes] = (_rms_rows(o) * norm_ref[...] * _silu(za_ref[:, lanes])).astype(o_ref.dtype)


def _delta_sample(zs_main, zs_gate, conv_buf, conv_w, a_log, dt_bias, norm_w, state, layer, n_heads, hps=8):
    nb = state.shape[0]
    wl = hps * HEAD_DIM
    ng = n_heads // hps
    col = lambda off: pl.BlockSpec((None, 1, wl), lambda a, h: (a, 0, off + h))
    buf = lambda off: pl.BlockSpec((None, None, CONV_W - 1, wl), lambda a, h: (a, layer, 0, off + h))
    cw = lambda off: pl.BlockSpec((CONV_W, wl), lambda a, h: (0, off + h))
    vec = pl.BlockSpec((1, n_heads), lambda a, h: (0, 0))
    return pl.pallas_call(
        functools.partial(_delta_sample_kernel, hps),
        out_shape=[jax.ShapeDtypeStruct((nb, 1, n_heads * HEAD_DIM), BF16),
                   jax.ShapeDtypeStruct((nb, n_heads, HEAD_DIM, HEAD_DIM), F32)],
        grid=(nb, ng),
        in_specs=[col(0), col(ng), col(2 * ng), col(3 * ng),
                  buf(0), buf(ng), buf(2 * ng), cw(0), cw(ng), cw(2 * ng),
                  pl.BlockSpec((None, 1, 2 * n_heads), lambda a, h: (a, 0, 0)), vec, vec,
                  pl.BlockSpec((1, HEAD_DIM), lambda a, h: (0, 0)),
                  pl.BlockSpec((None, None, hps, HEAD_DIM, HEAD_DIM), lambda a, h: (a, layer, h, 0, 0))],
        out_specs=[pl.BlockSpec((None, 1, wl), lambda a, h: (a, 0, h)),
                   pl.BlockSpec((None, hps, HEAD_DIM, HEAD_DIM), lambda a, h: (a, h, 0, 0))],
        compiler_params=_cparams("arbitrary", "arbitrary"),
        name="delta_sample",
    )(zs_main, zs_main, zs_main, zs_main, conv_buf, conv_buf, conv_buf, conv_w, conv_w, conv_w,
      zs_gate, a_log.reshape(1, n_heads), dt_bias.reshape(1, n_heads), norm_w.reshape(1, HEAD_DIM), state)


def _ret_sample_kernel(hps, q_ref, k_ref, v_ref, g_ref, cos_ref, sine_ref, sino_ref, s_ref, o_ref, so_ref):
    cos, sin_e, sin_o = cos_ref[...], sine_ref[...], sino_ref[...]
    for hh in range(hps):
        lanes = slice(hh * HEAD_DIM, (hh + 1) * HEAD_DIM)
        h = jnp.full((1, 1), pl.program_id(1) * hps + hh, jnp.int32).astype(F32)
        gamma = jnp.exp(_log_gamma(h))
        q = _xpos(q_ref[:, lanes], cos, sin_e, sin_o)
        k = _xpos(k_ref[:, lanes], cos, sin_e, sin_o) * (HEAD_DIM ** -0.5)
        v = v_ref[:, lanes]
        s = s_ref[hh]
        qs = _dot(_rows8(q), s)
        qk = jnp.sum(q * k, axis=-1, keepdims=True)
        o = qk * v + gamma * qs[0:1]
        so_ref[hh] = s * gamma + _row_to_col(k) * v
        o_ref[:, lanes] = (_rms_rows(o) * _silu(g_ref[:, lanes])).astype(o_ref.dtype)


def _ret_sample(zs_tail, cos, sin_e, sin_o, state, layer, n_heads, hps=8):
    nb = state.shape[0]
    wl = hps * HEAD_DIM
    ng = n_heads // hps
    col = lambda off: pl.BlockSpec((None, 1, wl), lambda a, h: (a, 0, off + h))
    tab = pl.BlockSpec((1, HEAD_DIM), lambda a, h: (0, 0))
    return pl.pallas_call(
        functools.partial(_ret_sample_kernel, hps),
        out_shape=[jax.ShapeDtypeStruct((nb, 1, n_heads * HEAD_DIM), BF16),
                   jax.ShapeDtypeStruct((nb, n_heads, HEAD_DIM, HEAD_DIM), F32)],
        grid=(nb, ng),
        in_specs=[col(0), col(ng), col(2 * ng), col(3 * ng), tab, tab, tab,
                  pl.BlockSpec((None, None, hps, HEAD_DIM, HEAD_DIM), lambda a, h: (a, layer, h, 0, 0))],
        out_specs=[pl.BlockSpec((None, 1, wl), lambda a, h: (a, 0, h)),
                   pl.BlockSpec((None, hps, HEAD_DIM, HEAD_DIM), lambda a, h: (a, h, 0, 0))],
        compiler_params=_cparams("arbitrary", "arbitrary"),
        name="ret_sample",
    )(zs_tail, zs_tail, zs_tail, zs_tail, cos, sin_e, sin_o, state)


def _attn_decode_kernel(lam_init, ppb, pt_ref, q_ref, kn_ref, vn_ref, *rest):
    kc_refs, vc_refs = rest[:ppb], rest[ppb:2 * ppb]
    lq_ref, sub_ref, o_ref, m_ref, l_ref, acc_ref, s_ref = rest[2 * ppb:]
    j = pl.program_id(1)
    n_steps = pl.num_programs(1)
    nh = q_ref.shape[0]
    scale = HALF_DIM ** -0.5
    di = lax.broadcasted_iota(jnp.int32, (HEAD_DIM, 2 * HEAD_DIM), 0)
    ci = lax.broadcasted_iota(jnp.int32, (HEAD_DIM, 2 * HEAD_DIM), 1)
    expand = ((di // HALF_DIM) == (ci // HEAD_DIM)).astype(BF16)
    q = q_ref[...] * (scale * math.log2(math.e))
    strip = 8
    n_strips = PAGE_SIZE // strip

    @pl.when(j == 0)
    def _():
        m_ref[...] = jnp.full(m_ref.shape, -jnp.inf, F32)
        l_ref[...] = jnp.zeros_like(l_ref)
        acc_ref[...] = jnp.zeros_like(acc_ref)

    low = lax.broadcasted_iota(jnp.int32, (nh, HEAD_DIM), 1) < HALF_DIM

    def swap_halves(x):
        return pltpu.roll(x.reshape(-1, HEAD_DIM), HALF_DIM, 1).reshape(x.shape)

    def scores(kt):
        n_tok = kt.shape[0]
        prod = (kt * q[None]).reshape(n_tok * nh, HEAD_DIM)
        s = jnp.dot(prod.astype(BF16), expand, preferred_element_type=F32).reshape(n_tok, nh, 2 * HEAD_DIM)
        return jnp.where(low[None], s[:, :, :HEAD_DIM], s[:, :, HEAD_DIM:])

    def weighted(p, vt):
        return jnp.concatenate([jnp.sum(p * vt, axis=0), jnp.sum(swap_halves(p) * vt, axis=0)], axis=-1)

    m, l, acc = m_ref[...], l_ref[...], acc_ref[...]
    s_ref[0] = scores(kc_refs[0][...])
    for pg in range(ppb):
        if pg + 1 < ppb:
            s_ref[pg + 1] = scores(kc_refs[pg + 1][...])
        m_new = m
        for i in range(n_strips):
            m_new = jnp.maximum(m_new, jnp.max(s_ref[pg, i * strip:(i + 1) * strip], axis=0))
        alpha = jnp.exp2(m - m_new)
        l, acc, m = alpha * l, jnp.concatenate([alpha, swap_halves(alpha)], axis=-1) * acc, m_new
        for i in range(n_strips):
            p = jnp.exp2(s_ref[pg, i * strip:(i + 1) * strip] - m[None])
            l = l + jnp.sum(p, axis=0)
            acc = acc + weighted(p, vc_refs[pg][i * strip:(i + 1) * strip])
    m_ref[...] = m
    l_ref[...] = l
    acc_ref[...] = acc

    @pl.when(j == n_steps - 1)
    def _():
        s = scores(kn_ref[...][None])[0]
        m_old = m_ref[...]
        m_new = jnp.maximum(m_old, s)
        alpha = jnp.exp2(m_old - m_new)
        p = jnp.exp2(s - m_new)
        l = alpha * l_ref[...] + p
        acc = (jnp.concatenate([alpha, swap_halves(alpha)], axis=-1) * acc_ref[...]
               + weighted(p[None], vn_ref[...][None]))
        direct, cross = acc[:, :HEAD_DIM], acc[:, HEAD_DIM:]
        l_sw = swap_halves(l)
        o1 = jnp.where(low, direct, cross) / jnp.where(low, l, l_sw)
        o2 = jnp.where(low, cross, direct) / jnp.where(low, l_sw, l)
        lam = _lambda(lq_ref[...], lam_init)
        o = o1 - lam * o2
        o_ref[...] = (_rms_rows(o) * sub_ref[...] * (1.0 - lam_init)).astype(o_ref.dtype)


def _attn_decode(q, k_new, v_new, cache_k, cache_v, page_table, lambda_qk, subln, lam_init, layer, ppb=16):
    nb, nh, _ = q.shape
    n_pages = page_table.shape[1]
    assert n_pages % ppb == 0
    tok = pl.BlockSpec((None, nh, HEAD_DIM), lambda b, j, pt: (b, 0, 0))

    def page(r):
        return pl.BlockSpec((None, None, PAGE_SIZE, nh, HEAD_DIM),
                            lambda b, j, pt: (pt[b, j * ppb + r], layer, 0, 0, 0))

    pages = [page(r) for r in range(ppb)]
    grid_spec = pltpu.PrefetchScalarGridSpec(
        num_scalar_prefetch=1,
        grid=(nb, n_pages // ppb),
        in_specs=[tok, tok, tok] + pages + pages +
                 [pl.BlockSpec((4, HALF_DIM), lambda b, j, pt: (0, 0)),
                  pl.BlockSpec((1, HEAD_DIM), lambda b, j, pt: (0, 0))],
        out_specs=tok,
        scratch_shapes=[pltpu.VMEM((nh, HEAD_DIM), F32),
                        pltpu.VMEM((nh, HEAD_DIM), F32),
                        pltpu.VMEM((nh, 2 * HEAD_DIM), F32),
                        pltpu.VMEM((ppb, PAGE_SIZE, nh, HEAD_DIM), F32)],
    )
    return pl.pallas_call(
        functools.partial(_attn_decode_kernel, lam_init, ppb),
        out_shape=jax.ShapeDtypeStruct((nb, nh, HEAD_DIM), BF16),
        grid_spec=grid_spec,
        compiler_params=_cparams("arbitrary", "arbitrary"),
        name="attn_decode",
    )(page_table, q, k_new, v_new, *([cache_k] * ppb), *([cache_v] * ppb), lambda_qk, subln.reshape(1, HEAD_DIM))


def _xpos_tables(pos):
    half = HEAD_DIM // 2
    angle = jnp.repeat(1.0 / (10000.0 ** jnp.linspace(0.0, 1.0, half, dtype=F32)), 2)
    ph = pos.astype(F32)[:, None] * angle
    sin, cos = jnp.sin(ph), jnp.cos(ph)
    even = (jnp.arange(HEAD_DIM) % 2) == 0
    return cos, jnp.where(even, -sin, 0.0), jnp.where(even, 0.0, sin)


def kernel(x_prompt, x_sample, state_conv_a, state_delta, state_ret, cache_k, cache_v, page_table, c_prompt, c_sample, w_ada, b_ada, w_in, conv_a, a_log, dt_bias, norm_a, lambda_qk, subln_c, w_out, ln1_g, ln1_b, w_up, w_down, ln2_g, ln2_b):
    bp, t, d = x_prompt.shape
    nb = x_sample.shape[0]
    depth = w_in.shape[0]
    h_a = state_delta.shape[2]
    h_b = state_ret.shape[2]
    h_c = cache_k.shape[3]
    a_w, b_w, c_w = h_a * HEAD_DIM, h_b * HEAD_DIM, h_c * HEAD_DIM
    n_main = 4 * a_w
    n_gate = 2 * h_a
    past_len = page_table.shape[1] * PAGE_SIZE
    alpha = (2 * depth) ** 0.25
    rs = SAMPLE_ROWS
    assert nb + bp <= rs and x_sample.shape[1] == 1

    c_all = jnp.concatenate([c_sample, c_prompt, jnp.zeros((rs - nb - bp, d), F32)], axis=0)
    mod = _ada(c_all, w_ada, b_ada)

    def mod_p(l, i):
        return mod[l, nb:nb + bp, None, i * d:(i + 1) * d]

    def mod_s(l, i):
        return mod[l][None, :, i * d:(i + 1) * d]

    cos_p, sine_p, sino_p = _xpos_tables(jnp.arange(t))
    cos_s, sine_s, sino_s = _xpos_tables(past_len + jnp.arange(1))

    xp = x_prompt
    xs = jnp.concatenate([x_sample.reshape(nb, d), jnp.zeros((rs - nb, d), F32)], axis=0)[None]
    hp = _modulate(xp, mod_p(0, 1), mod_p(0, 0), 256)
    hs = _modulate(xs, mod_s(0, 1), mod_s(0, 0), rs)

    w_in_t = jnp.swapaxes(w_in, 1, 2)
    n_tail = w_in.shape[2] - n_main - n_gate
    w_gate_b = w_in[:, :, n_main:n_main + n_gate].astype(BF16)
    w_down_b = w_down.astype(BF16)

    st_p, st_s = [], []
    for l in range(depth):
        lam_init = 0.8 - 0.6 * math.exp(-0.3 * l)
        hp2 = hp.reshape(bp * t, d)
        hs2 = hs.reshape(rs, d)
        zp_main, zs_main = _panel_matmul(hp2, hs2, w_in_t, l, 0, n_main, transposed=True)
        zp_gate, zs_gate = _matmul(hp2, hs2, w_gate_b, l)
        zp_tail, zs_tail = _panel_matmul(hp2, hs2, w_in_t, l, n_main + n_gate, n_tail, transposed=True)
        zp_main = zp_main.reshape(bp, t, n_main)
        zp_tail = zp_tail.reshape(bp, t, -1)

        gc, beta = _gates(zp_gate.reshape(bp, t, n_gate), a_log[l], dt_bias[l])
        mix_p, delta_p = _delta_prompt(zp_main, conv_a[l], gc, beta, norm_a[l], h_a, a_w + b_w + c_w)
        mix_p, ret_p = _ret_prompt(zp_tail, cos_p, sine_p, sino_p, mix_p, a_w, h_b)
        mix_p = _attn_prompt(zp_tail, lambda_qk[l], subln_c[l], lam_init, mix_p, a_w + b_w, h_c, 4 * h_b)
        k_off = 4 * b_w + c_w
        conv_p = zp_main[:, t - (CONV_W - 1):, :3 * a_w]
        k_p = zp_tail[:, :, k_off:k_off + c_w]
        v_p = zp_tail[:, :, k_off + c_w:k_off + 2 * c_w]
        st_p.append((conv_p, delta_p, ret_p, k_p, v_p))

        zs_m = zs_main[:nb]
        zs_t = zs_tail[:nb]
        oa_s, delta_s = _delta_sample(zs_m[:, None], zs_gate[:nb, None], state_conv_a, conv_a[l], a_log[l],
                                      dt_bias[l], norm_a[l], state_delta, l, h_a)
        ob_s, ret_s = _ret_sample(zs_t[:, None], cos_s, sine_s, sino_s, state_ret, l, h_b)
        q_s = zs_t[:, 4 * b_w:4 * b_w + c_w].reshape(nb, h_c, HEAD_DIM)
        k_s = zs_t[:, k_off:k_off + c_w].reshape(nb, h_c, HEAD_DIM)
        v_s = zs_t[:, k_off + c_w:k_off + 2 * c_w].reshape(nb, h_c, HEAD_DIM)
        oc_s = _attn_decode(q_s, k_s, v_s, cache_k, cache_v, page_table, lambda_qk[l], subln_c[l], lam_init, l)
        conv_s = jnp.concatenate([state_conv_a[:, l, 1:], zs_m[:, None, :3 * a_w]], axis=1)
        st_s.append((conv_s, delta_s, ret_s, k_s[:, None], v_s[:, None]))

        mix_p = mix_p.reshape(bp * t, -1)
        mix_s = jnp.concatenate([oa_s.reshape(nb, a_w), ob_s.reshape(nb, b_w), oc_s.reshape(nb, c_w)], axis=-1)
        mix_s = jnp.concatenate([mix_s, jnp.zeros((rs - nb, mix_s.shape[1]), BF16)], axis=0)
        yp, ys = _panel_matmul(mix_p, mix_s, w_out, l, 0, d)
        xp, hp = _postnorm(xp, yp.reshape(bp, t, d), mod_p(l, 2), ln1_g[l], ln1_b[l], mod_p(l, 4), mod_p(l, 3),
                           alpha, 256)
        xs, hs = _postnorm(xs, ys[None], mod_s(l, 2), ln1_g[l], ln1_b[l], mod_s(l, 4), mod_s(l, 3), alpha, rs)

        up, us = _panel_matmul(hp.reshape(bp * t, d), hs.reshape(rs, d), w_up, l, 0, w_up.shape[2],
                               out_dtype=BF16, relu2=True)
        fp, fs = _matmul(up, us, w_down_b, l)
        last = l == depth - 1
        nl = min(l + 1, depth - 1)
        xp, hp = _postnorm(xp, fp.reshape(bp, t, d), mod_p(l, 5), ln2_g[l], ln2_b[l], mod_p(nl, 1), mod_p(nl, 0),
                           alpha, 256, with_h=not last)
        xs, hs = _postnorm(xs, fs[None], mod_s(l, 5), ln2_g[l], ln2_b[l], mod_s(nl, 1), mod_s(nl, 0), alpha, rs,
                           with_h=not last)

    outs_p = [jnp.stack([s[i] for s in st_p], axis=1) for i in range(5)]
    outs_p[3:] = [o.reshape(bp, depth, t, h_c, HEAD_DIM) for o in outs_p[3:]]
    outs_p = tuple(outs_p)
    outs_s = tuple(jnp.stack([s[i] for s in st_s], axis=1) for i in range(5))
    y_sample = xs[0, :nb].reshape(nb, 1, d)
    return (xp, y_sample) + outs_p + outs_s
```

```python
import functools
import math

import jax
import jax.numpy as jnp
from jax import lax
from jax.experimental import pallas as pl
from jax.experimental.pallas import tpu as pltpu

F32 = jnp.float32
BF16 = jnp.bfloat16

SUBLANES = 8
HEAD_DIM = 128
HALF_DIM = HEAD_DIM // 2
CONV_W = 4
DELTA_CHUNK = 64
DELTA_SUPER = 256
DELTA_BASE = 4
RET_CHUNK = 128
PAGE_SIZE = 128
EPS = 1e-5
SAMPLE_ROWS = 16
VMEM_LIMIT = 58 * 1024 * 1024


def _cparams(*sem):
    return pltpu.CompilerParams(dimension_semantics=sem, vmem_limit_bytes=VMEM_LIMIT)


def _sigmoid(x):
    return 1.0 / (1.0 + jnp.exp(-x))


def _silu(x):
    return x * _sigmoid(x)


def _softplus(x):
    return jnp.maximum(x, 0.0) + jnp.log1p(jnp.exp(-jnp.abs(x)))


def _dot(a, b):
    return jnp.dot(a.astype(BF16), b.astype(BF16), preferred_element_type=F32)


def _dot_nt(a, b):
    return lax.dot_general(a.astype(BF16), b.astype(BF16), (((1,), (1,)), ((), ())),
                           preferred_element_type=F32)


def _rms_rows(x):
    return x * lax.rsqrt(jnp.mean(x * x, axis=-1, keepdims=True) + EPS)


def _ada_kernel(c_ref, w0_ref, w1_ref, b_ref, o_ref):
    c = _silu(c_ref[...])
    dh = w0_ref.shape[0]
    o_ref[...] = _dot(c[:, :dh], w0_ref[...]) + _dot(c[:, dh:], w1_ref[...]) + b_ref[...]


def _ada(c_all, w_ada, b_ada, tn=1024):
    depth, d, n = w_ada.shape
    rows = c_all.shape[0]
    return pl.pallas_call(
        _ada_kernel,
        out_shape=jax.ShapeDtypeStruct((depth, rows, n), F32),
        grid=(depth, n // tn),
        in_specs=[pl.BlockSpec((rows, d), lambda l, j: (0, 0)),
                  pl.BlockSpec((None, d // 2, tn), lambda l, j: (l, 0, j)),
                  pl.BlockSpec((None, d // 2, tn), lambda l, j: (l, 1, j)),
                  pl.BlockSpec((None, 1, tn), lambda l, j: (l, 0, j))],
        out_specs=pl.BlockSpec((None, rows, tn), lambda l, j: (l, 0, j)),
        compiler_params=_cparams("arbitrary", "arbitrary"),
        name="ada_mod",
    )(c_all, w_ada, w_ada, b_ada.reshape(depth, 1, n))


def _modulate_kernel(x_ref, sc_ref, sh_ref, o_ref):
    o_ref[...] = (x_ref[...] * (1.0 + sc_ref[...]) + sh_ref[...]).astype(o_ref.dtype)


def _modulate(x, sc, sh, tm):
    g, r, d = x.shape
    rm = sc.shape[1]
    mspec = pl.BlockSpec((None, rm, d), lambda a, i: (a, 0, 0))
    return pl.pallas_call(
        _modulate_kernel,
        out_shape=jax.ShapeDtypeStruct((g, r, d), BF16),
        grid=(g, r // tm),
        in_specs=[pl.BlockSpec((None, tm, d), lambda a, i: (a, i, 0)), mspec, mspec],
        out_specs=pl.BlockSpec((None, tm, d), lambda a, i: (a, i, 0)),
        compiler_params=_cparams("arbitrary", "arbitrary"),
        name="modulate",
    )(x, sc, sh)


def _postnorm_kernel(alpha, with_h, x_ref, y_ref, gate_ref, lg_ref, lb_ref, sc_ref, sh_ref, xo_ref, *h_ref):
    v = alpha * x_ref[...] + (1.0 + gate_ref[...]) * y_ref[...]
    vc = v - jnp.mean(v, axis=-1, keepdims=True)
    var = jnp.mean(vc * vc, axis=-1, keepdims=True)
    xn = vc * lax.rsqrt(var + EPS) * lg_ref[...] + lb_ref[...]
    xo_ref[...] = xn
    if with_h:
        h_ref[0][...] = (xn * (1.0 + sc_ref[...]) + sh_ref[...]).astype(BF16)


def _postnorm(x, y, gate, ln_g, ln_b, sc, sh, alpha, tm, with_h=True):
    g, r, d = x.shape
    rm = gate.shape[1]
    row = pl.BlockSpec((None, tm, d), lambda a, i: (a, i, 0))
    mspec = pl.BlockSpec((None, rm, d), lambda a, i: (a, 0, 0))
    vec = pl.BlockSpec((1, d), lambda a, i: (0, 0))
    out_shape = [jax.ShapeDtypeStruct((g, r, d), F32)]
    out_specs = [row]
    if with_h:
        out_shape.append(jax.ShapeDtypeStruct((g, r, d), BF16))
        out_specs.append(row)
    res = pl.pallas_call(
        functools.partial(_postnorm_kernel, alpha, with_h),
        out_shape=out_shape,
        grid=(g, r // tm),
        in_specs=[row, row, mspec, vec, vec, mspec, mspec],
        out_specs=out_specs,
        compiler_params=_cparams("arbitrary", "arbitrary"),
        name="postnorm",
    )(x, y, gate, ln_g.reshape(1, d), ln_b.reshape(1, d), sc, sh)
    return (res[0], res[1]) if with_h else (res[0], None)


def _matmul_kernel(relu2, nk, a_ref, as_ref, w_ref, o_ref, os_ref, *scratch):
    i = pl.program_id(1)
    k = pl.program_id(2)

    def finish(v):
        return jnp.square(jnp.maximum(v, 0.0)) if relu2 else v

    if nk == 1:
        o_ref[...] = finish(jnp.dot(a_ref[...], w_ref[...], preferred_element_type=F32)).astype(o_ref.dtype)

        @pl.when(i == 0)
        def _():
            os_ref[...] = finish(jnp.dot(as_ref[...], w_ref[...], preferred_element_type=F32)).astype(os_ref.dtype)
    else:
        acc_ref, accs_ref = scratch

        @pl.when(k == 0)
        def _():
            acc_ref[...] = jnp.zeros_like(acc_ref)

        acc_ref[...] += jnp.dot(a_ref[...], w_ref[...], preferred_element_type=F32)

        @pl.when(k == nk - 1)
        def _():
            o_ref[...] = finish(acc_ref[...]).astype(o_ref.dtype)

        @pl.when(i == 0)
        def _():
            @pl.when(k == 0)
            def _():
                accs_ref[...] = jnp.zeros_like(accs_ref)

            accs_ref[...] += jnp.dot(as_ref[...], w_ref[...], preferred_element_type=F32)

            @pl.when(k == nk - 1)
            def _():
                os_ref[...] = finish(accs_ref[...]).astype(os_ref.dtype)


def _matmul(a, a_s, w3, layer, col0=0, n=None, out_dtype=F32, relu2=False, tm=1024, tn=1024, tk=4096):
    m, kdim = a.shape
    n = w3.shape[2] if n is None else n
    rs = a_s.shape[0]
    tn = min(tn, n)
    tk = min(tk, kdim)
    nk = kdim // tk
    assert n % tn == 0 and col0 % tn == 0 and m % tm == 0 and kdim % tk == 0
    cb0 = col0 // tn
    scratch = [] if nk == 1 else [pltpu.VMEM((tm, tn), F32), pltpu.VMEM((rs, tn), F32)]
    return pl.pallas_call(
        functools.partial(_matmul_kernel, relu2, nk),
        out_shape=[jax.ShapeDtypeStruct((m, n), out_dtype), jax.ShapeDtypeStruct((rs, n), out_dtype)],
        grid=(n // tn, m // tm, nk),
        in_specs=[pl.BlockSpec((tm, tk), lambda j, i, k: (i, k)),
                  pl.BlockSpec((rs, tk), lambda j, i, k: (0, k)),
                  pl.BlockSpec((None, tk, tn), lambda j, i, k: (layer, k, cb0 + j))],
        out_specs=[pl.BlockSpec((tm, tn), lambda j, i, k: (i, j)),
                   pl.BlockSpec((rs, tn), lambda j, i, k: (0, j))],
        scratch_shapes=scratch,
        compiler_params=_cparams("arbitrary", "arbitrary", "arbitrary"),
        name="matmul_relu2" if relu2 else "matmul",
    )(a, a_s, w3)


def _panel_matmul_kernel(relu2, n_panels, transposed, a_ref, as_ref, w_ref, o_ref, os_ref, wb_ref):
    jj = pl.program_id(0)
    i = pl.program_id(1)
    rows = w_ref.shape[0]

    def product(x, w):
        if transposed:
            v = lax.dot_general(x, w, (((1,), (1,)), ((), ())), preferred_element_type=F32)
        else:
            v = jnp.dot(x, w, preferred_element_type=F32)
        return jnp.square(jnp.maximum(v, 0.0)) if relu2 else v

    @pl.when(jj < n_panels)
    def _():
        wb_ref[jj % 2, pl.ds(pl.multiple_of(i * rows, rows), rows), :] = w_ref[...].astype(BF16)

    @pl.when(jj > 0)
    def _():
        w = wb_ref[(jj + 1) % 2]
        o_ref[...] = product(a_ref[...], w).astype(o_ref.dtype)

        @pl.when(i == 0)
        def _():
            os_ref[...] = product(as_ref[...], w).astype(os_ref.dtype)


def _panel_matmul(a, a_s, w3, layer, col0, n, transposed=False, out_dtype=F32, relu2=False, tm=1024, tn=1024):
    m, kdim = a.shape
    rs = a_s.shape[0]
    n_i = m // tm
    n_panels = n // tn
    assert n % tn == 0 and m % tm == 0

    def clamp(jj, i):
        last = jj == n_panels
        return jnp.minimum(jj, n_panels - 1), jnp.where(last, n_i - 1, i)

    if transposed:
        assert tn % (SUBLANES * n_i) == 0 and col0 % SUBLANES == 0
        rows = tn // n_i

        def w_index(jj, i):
            panel, chunk = clamp(jj, i)
            return (layer, pl.multiple_of(col0 + panel * tn + chunk * rows, SUBLANES), 0)

        w_spec = pl.BlockSpec((None, pl.Element(rows), pl.Element(kdim)), w_index)
        wb_shape = (2, tn, kdim)
    else:
        assert kdim % (SUBLANES * n_i) == 0 and col0 % tn == 0
        rows = kdim // n_i

        def w_index(jj, i):
            panel, chunk = clamp(jj, i)
            return (layer, chunk, col0 // tn + panel)

        w_spec = pl.BlockSpec((None, rows, tn), w_index)
        wb_shape = (2, kdim, tn)

    def row_block(jj, i):
        return jnp.where(jj == 0, 0, i)

    return pl.pallas_call(
        functools.partial(_panel_matmul_kernel, relu2, n_panels, transposed),
        out_shape=[jax.ShapeDtypeStruct((m, n), out_dtype), jax.ShapeDtypeStruct((rs, n), out_dtype)],
        grid=(n_panels + 1, n_i),
        in_specs=[pl.BlockSpec((tm, kdim), lambda jj, i: (row_block(jj, i), 0)),
                  pl.BlockSpec((rs, kdim), lambda jj, i: (0, 0)), w_spec],
        out_specs=[pl.BlockSpec((tm, tn), lambda jj, i: (row_block(jj, i), jnp.maximum(jj - 1, 0))),
                   pl.BlockSpec((rs, tn), lambda jj, i: (0, jnp.maximum(jj - 1, 0)))],
        scratch_shapes=[pltpu.VMEM(wb_shape, BF16)],
        compiler_params=_cparams("arbitrary", "arbitrary"),
        name="panel_matmul_relu2" if relu2 else "panel_matmul",
    )(a, a_s, w3)


def _gates_kernel(n_heads, zg_ref, alog_ref, dtb_ref, gc_ref, beta_ref):
    zg = zg_ref[...]
    rows = zg.shape[0]
    g = -jnp.exp(alog_ref[...]) * _softplus(zg[:, :n_heads] + dtb_ref[...])
    beta = _sigmoid(zg[:, n_heads:])
    pos = lax.broadcasted_iota(jnp.int32, (rows, HEAD_DIM), 0) % DELTA_CHUNK
    for h in range(n_heads):
        gh = jnp.broadcast_to(g[:, h:h + 1], (rows, HEAD_DIM))
        s = 1
        while s < DELTA_CHUNK:
            gh = gh + jnp.where(pos >= s, pltpu.roll(gh, s, 0), 0.0)
            s *= 2
        gc_ref[h] = gh
        beta_ref[h] = jnp.broadcast_to(beta[:, h:h + 1], (rows, HEAD_DIM))


def _gates(zg, a_log, dt_bias, rows=256):
    b, t, h2 = zg.shape
    nh = h2 // 2
    out = jax.ShapeDtypeStruct((b, nh, t, HEAD_DIM), F32)
    ospec = pl.BlockSpec((None, nh, rows, HEAD_DIM), lambda a, i: (a, 0, i, 0))
    vec = pl.BlockSpec((1, nh), lambda a, i: (0, 0))
    return pl.pallas_call(
        functools.partial(_gates_kernel, nh),
        out_shape=[out, out],
        grid=(b, t // rows),
        in_specs=[pl.BlockSpec((None, rows, h2), lambda a, i: (a, i, 0)), vec, vec],
        out_specs=[ospec, ospec],
        compiler_params=_cparams("arbitrary", "arbitrary"),
        name="delta_gates",
    )(zg, a_log.reshape(1, nh), dt_bias.reshape(1, nh))


def _delta_prompt_kernel(nh, q_ref, k_ref, v_ref, za_ref, cq_ref, ck_ref, cv_ref, gc_ref, beta_ref, norm_ref,
                         o_ref, s_ref, wq_ref, ka_ref, u_ref, eg_ref):
    t = q_ref.shape[0]
    sb = DELTA_SUPER
    c = DELTA_CHUNK
    n_sb = t // sb
    cps = sb // c
    half = sb // 2
    ii = lax.broadcasted_iota(jnp.int32, (sb, sb), 0)
    jj = lax.broadcasted_iota(jnp.int32, (sb, sb), 1)
    lower = ((ii // c) == (jj // c)) & (ii >= jj)
    strict = ii > jj
    row = lax.broadcasted_iota(jnp.int32, (sb, HEAD_DIM), 0)
    pair_chunk = lax.broadcasted_iota(jnp.int32, (HEAD_DIM, 2 * c), 1) // c
    blk_xor = ii ^ jj
    base = DELTA_BASE
    base_levels = int(math.log2(base))

    def l2norm(x):
        return x * lax.rsqrt(jnp.sum(x * x, axis=-1, keepdims=True) + 1e-6)

    def prepare(n, hh):
        static = isinstance(n, int)
        r0 = n * sb if static else pl.multiple_of(n * sb, sb)
        rows = pl.ds(r0, sb)
        lanes = slice(hh * HEAD_DIM, (hh + 1) * HEAD_DIM)

        def conv_silu(x_ref, w_ref):
            x = x_ref[rows, lanes]
            w = w_ref[:, lanes]
            if static and n == 0:
                prev = jnp.zeros((SUBLANES, HEAD_DIM), F32)
            else:
                p0 = r0 - SUBLANES
                prev = x_ref[pl.ds(p0 if static else pl.multiple_of(p0, SUBLANES), SUBLANES), lanes]
            y = x * w[CONV_W - 1:CONV_W]
            for s in range(1, CONV_W):
                head = jnp.broadcast_to(pltpu.roll(prev, s, 0)[None], (sb // SUBLANES, SUBLANES, HEAD_DIM))
                xs = jnp.where(row < s, head.reshape(sb, HEAD_DIM), pltpu.roll(x, s, 0))
                y = y + xs * w[CONV_W - 1 - s:CONV_W - s]
            return _silu(y)

        q = l2norm(conv_silu(q_ref, cq_ref)) * (HEAD_DIM ** -0.5)
        k = l2norm(conv_silu(k_ref, ck_ref))
        v = conv_silu(v_ref, cv_ref)
        gc = gc_ref[hh, rows]
        beta = beta_ref[hh, rows]
        kb = k * beta
        eg = jnp.exp(gc)
        gct = jnp.concatenate([gc[:half].T, gc[half:].T], axis=1)
        gc_j = jnp.broadcast_to(gct[0:1], (sb, sb))
        gc_i = jnp.concatenate([gc] * (sb // HEAD_DIM), axis=1)
        dec = jnp.exp(jnp.where(lower, gc_i - gc_j, -jnp.inf))
        kq = _dot_nt(jnp.concatenate([kb, q], axis=0), k)
        yield
        mm = jnp.where(strict, kq[:sb] * dec, 0.0)
        attn = kq[sb:] * dec
        x = jnp.where(blk_xor < base, -mm, 0.0)
        p = jnp.where(ii == jj, 1.0, x)
        xb = x.astype(BF16)
        x = jnp.dot(xb, xb, preferred_element_type=F32)
        yield
        for lvl in range(1, base_levels):
            xb = x.astype(BF16)
            if lvl < base_levels - 1:
                y = jnp.dot(jnp.concatenate([p.astype(BF16), xb], axis=0), xb, preferred_element_type=F32)
                yield
                p = p + y[:sb]
                x = y[sb:]
            else:
                y = jnp.dot(p.astype(BF16), xb, preferred_element_type=F32)
                yield
                p = p + y
        size = base
        while size < c:
            pb = p.astype(BF16)
            off = jnp.where((blk_xor >= size) & (blk_xor < 2 * size), mm, 0.0).astype(BF16)
            e = jnp.dot(off, pb, preferred_element_type=F32)
            yield
            y = jnp.dot(pb, e.astype(BF16), preferred_element_type=F32)
            yield
            p = p - y
            size *= 2
        rhs = jnp.concatenate([v * beta, kb * eg], axis=1)
        pb = p.astype(BF16)
        uw = jnp.dot(pb, rhs.astype(BF16), preferred_element_type=F32)
        yield
        mm_hi = mm.astype(BF16)
        mm_lo = (mm - mm_hi.astype(F32)).astype(BF16)
        uw_hi = uw.astype(BF16)
        uw_lo = (uw - uw_hi.astype(F32)).astype(BF16)
        resid = rhs - uw - (jnp.dot(mm_hi, uw_hi, preferred_element_type=F32)
                            + jnp.dot(mm_hi, uw_lo, preferred_element_type=F32)
                            + jnp.dot(mm_lo, uw_hi, preferred_element_type=F32))
        yield
        uw = uw + jnp.dot(pb, resid.astype(BF16), preferred_element_type=F32)
        yield
        u_ref[hh, rows] = uw[:, :HEAD_DIM]
        w = uw[:, HEAD_DIM:]
        qd = q * eg
        gl = jnp.concatenate([jnp.broadcast_to(gc[ci * c + c - 1:ci * c + c], (c, HEAD_DIM))
                              for ci in range(cps)], axis=0)
        kd = k * jnp.exp(gl - gc)
        kdt = jnp.concatenate([kd[:half].T, kd[half:].T], axis=1)
        for ci in range(cps):
            cc = n * cps + ci
            cr = slice(ci * c, (ci + 1) * c)
            pair = slice((ci // 2) * 2 * c, (ci // 2 + 1) * 2 * c)
            wq_ref[hh, cc] = jnp.concatenate([w[cr], qd[cr]], axis=0).astype(BF16)
            ka_ref[hh, cc] = jnp.concatenate([jnp.where(pair_chunk == ci % 2, kdt[:, pair], 0.0),
                                              attn[cr, pair]], axis=0).astype(BF16)
            eg_ref[hh, cc] = jnp.exp(gl[ci * c:ci * c + SUBLANES])

    def recur(n, states):
        static = isinstance(n, int)
        for ci in range(cps):
            cc = n * cps + ci
            crow = pl.ds(cc * c if static else pl.multiple_of(cc * c, c), c)
            r1 = [jnp.dot(wq_ref[hh, cc], states[hh].astype(BF16), preferred_element_type=F32) for hh in range(nh)]
            yield
            r2 = []
            for hh in range(nh):
                v_new = (u_ref[hh, crow] - r1[hh][:c]).astype(BF16)
                r2.append(jnp.dot(ka_ref[hh, cc], jnp.concatenate([v_new, v_new], axis=0),
                                  preferred_element_type=F32))
            yield
            for hh in range(nh):
                lanes = slice(hh * HEAD_DIM, (hh + 1) * HEAD_DIM)
                o = r1[hh][c:] + r2[hh][HEAD_DIM:]
                states[hh] = states[hh] * eg_ref[hh, cc][0:1] + r2[hh][:HEAD_DIM]
                o_ref[crow, lanes] = (_rms_rows(o) * norm_ref[...] * _silu(za_ref[crow, lanes])).astype(o_ref.dtype)

    def interleave(gens):
        live = list(gens)
        while live:
            live = [g for g in live if next(g, live) is not live]

    def chain(*gens):
        for g in gens:
            yield from g

    bpi = 2
    assert n_sb % bpi == 0

    def body(it, states):
        states = list(states)
        first = it * bpi
        interleave([prepare(first + r, hh) for r in range(bpi) for hh in range(nh)]
                   + [chain(*[recur(first - bpi + r, states) for r in range(bpi)])])
        return tuple(states)

    interleave([prepare(r, hh) for r in range(bpi) for hh in range(nh)])
    states = lax.fori_loop(1, n_sb // bpi, body, tuple(jnp.zeros((HEAD_DIM, HEAD_DIM), F32) for _ in range(nh)))
    states = list(states)
    interleave([chain(*[recur(n_sb - bpi + r, states) for r in range(bpi)])])
    for hh in range(nh):
        s_ref[hh] = states[hh]


def _delta_prompt(z_main, conv_w, gc, beta, norm_w, n_heads, mix_width, hpb=2):
    b, t, _ = z_main.shape
    wl = hpb * HEAD_DIM
    ng = n_heads // hpb
    n_chunks = t // DELTA_CHUNK
    col = lambda off: pl.BlockSpec((None, t, wl), lambda a, h: (a, 0, off + h))
    cw = lambda off: pl.BlockSpec((CONV_W, wl), lambda a, h: (0, off + h))
    gate = pl.BlockSpec((None, hpb, t, HEAD_DIM), lambda a, h: (a, h, 0, 0))
    return pl.pallas_call(
        functools.partial(_delta_prompt_kernel, hpb),
        out_shape=[jax.ShapeDtypeStruct((b, t, mix_width), BF16),
                   jax.ShapeDtypeStruct((b, n_heads, HEAD_DIM, HEAD_DIM), F32)],
        grid=(b, ng),
        in_specs=[col(0), col(ng), col(2 * ng), col(3 * ng),
                  cw(0), cw(ng), cw(2 * ng), gate, gate,
                  pl.BlockSpec((1, HEAD_DIM), lambda a, h: (0, 0))],
        out_specs=[pl.BlockSpec((None, t, wl), lambda a, h: (a, 0, h)),
                   pl.BlockSpec((None, hpb, HEAD_DIM, HEAD_DIM), lambda a, h: (a, h, 0, 0))],
        scratch_shapes=[pltpu.VMEM((hpb, n_chunks, 2 * DELTA_CHUNK, HEAD_DIM), BF16),
                        pltpu.VMEM((hpb, n_chunks, HEAD_DIM + DELTA_CHUNK, 2 * DELTA_CHUNK), BF16),
                        pltpu.VMEM((hpb, t, HEAD_DIM), F32),
                        pltpu.VMEM((hpb, n_chunks, SUBLANES, HEAD_DIM), F32)],
        compiler_params=_cparams("arbitrary", "arbitrary"),
        name="delta_prompt",
    )(z_main, z_main, z_main, z_main, conv_w, conv_w, conv_w, gc, beta, norm_w.reshape(1, HEAD_DIM))


def _xpos(x, cos, sin_e, sin_o):
    return x * cos + pltpu.roll(x, HEAD_DIM - 1, 1) * sin_e + pltpu.roll(x, 1, 1) * sin_o


def _log_gamma(h):
    return jnp.log1p(-jnp.exp2(-5.0 - h))


def _ret_prompt_kernel(q_ref, k_ref, v_ref, g_ref, cos_ref, sine_ref, sino_ref, mix_ref, o_ref, s_ref,
                       qs_ref, ks_ref):
    del mix_ref
    t = q_ref.shape[0]
    c = RET_CHUNK
    cos, sin_e, sin_o = cos_ref[...], sine_ref[...], sino_ref[...]
    qs_ref[...] = _xpos(q_ref[...], cos, sin_e, sin_o)
    ks_ref[...] = _xpos(k_ref[...], cos, sin_e, sin_o) * (HEAD_DIM ** -0.5)

    h = jnp.full((1, 1), pl.program_id(1), jnp.int32).astype(F32)
    lg = _log_gamma(h)
    ii = lax.broadcasted_iota(jnp.int32, (c, c), 0)
    jj = lax.broadcasted_iota(jnp.int32, (c, c), 1)
    dec = jnp.exp(jnp.where(ii >= jj, (ii - jj).astype(F32) * lg, -jnp.inf))
    pos = lax.broadcasted_iota(jnp.int32, (c, HEAD_DIM), 0).astype(F32)
    e_in = jnp.exp((pos + 1.0) * lg)
    e_out = jnp.exp((c - 1.0 - pos) * lg)
    e_all = jnp.exp(c * lg)

    unroll = 2

    def body(n, s):
        rows = [pl.ds(pl.multiple_of((n * unroll + r) * c, c), c) for r in range(unroll)]
        q = [qs_ref[rw] for rw in rows]
        k = [ks_ref[rw] for rw in rows]
        v = [v_ref[rw].astype(BF16) for rw in rows]
        qk = [_dot_nt(q[r], k[r]) for r in range(unroll)]
        upd = [_dot((k[r] * e_out).T, v[r]) for r in range(unroll)]
        for r in range(unroll):
            o = _dot(q[r] * e_in, s) + _dot(qk[r] * dec, v[r])
            s = s * e_all + upd[r]
            o_ref[rows[r]] = (_rms_rows(o) * _silu(g_ref[rows[r]])).astype(o_ref.dtype)
        return s

    s_ref[...] = lax.fori_loop(0, t // (c * unroll), body, jnp.zeros((HEAD_DIM, HEAD_DIM), F32))


def _ret_prompt(z_tail, cos, sin_e, sin_o, mix, mix_col0, n_heads):
    b, t, _ = z_tail.shape
    c0 = mix_col0 // HEAD_DIM
    col = lambda off: pl.BlockSpec((None, t, HEAD_DIM), lambda a, h: (a, 0, off + h))
    tab = pl.BlockSpec((t, HEAD_DIM), lambda a, h: (0, 0))
    return pl.pallas_call(
        _ret_prompt_kernel,
        out_shape=[jax.ShapeDtypeStruct(mix.shape, mix.dtype),
                   jax.ShapeDtypeStruct((b, n_heads, HEAD_DIM, HEAD_DIM), F32)],
        grid=(b, n_heads),
        in_specs=[col(0), col(n_heads), col(2 * n_heads), col(3 * n_heads), tab, tab, tab,
                  pl.BlockSpec(memory_space=pl.ANY)],
        out_specs=[pl.BlockSpec((None, t, HEAD_DIM), lambda a, h: (a, 0, c0 + h)),
                   pl.BlockSpec((None, None, HEAD_DIM, HEAD_DIM), lambda a, h: (a, h, 0, 0))],
        scratch_shapes=[pltpu.VMEM((t, HEAD_DIM), F32)] * 2,
        input_output_aliases={7: 0},
        compiler_params=_cparams("arbitrary", "arbitrary"),
        name="ret_prompt",
    )(z_tail, z_tail, z_tail, z_tail, cos, sin_e, sin_o, mix)


def _lambda(lq, lam_init):
    l01 = jnp.sum(lq[0:1] * lq[1:2], axis=-1, keepdims=True)
    l23 = jnp.sum(lq[2:3] * lq[3:4], axis=-1, keepdims=True)
    return jnp.exp(l01) - jnp.exp(l23) + lam_init


def _attn_prompt_kernel(lam_init, nh, tq, n_alias, q_ref, k_ref, v_ref, lq_ref, sub_ref, *rest):
    o_ref, kout_ref, vout_ref, q2_ref, s_ref, p_ref, m_ref, l_ref, acc_ref = rest[1 + n_alias:]
    i = pl.program_id(2)

    @pl.when(i == 0)
    def _():
        kout_ref[...] = k_ref[...]
        vout_ref[...] = v_ref[...]

    strip = 64
    lane = lax.broadcasted_iota(jnp.int32, (tq, HEAD_DIM), 1)
    scale = HALF_DIM ** -0.5
    for hh in range(nh):
        q = q_ref[:, hh * HEAD_DIM:(hh + 1) * HEAD_DIM]
        q2_ref[hh] = (jnp.concatenate([jnp.where(lane < HALF_DIM, q, 0.0), jnp.where(lane >= HALF_DIM, q, 0.0)],
                                      axis=0) * scale).astype(BF16)
    m_ref[...] = jnp.full(m_ref.shape, -jnp.inf, F32)
    l_ref[...] = jnp.zeros_like(l_ref)
    acc_ref[...] = jnp.zeros_like(acc_ref)
    key_idx = lax.broadcasted_iota(jnp.int32, (strip, tq), 1)
    qry_idx = lax.broadcasted_iota(jnp.int32, (strip, tq), 0)

    def step(j, diagonal):
        rows = pl.ds(pl.multiple_of(j * tq, tq), tq)
        for hh in range(nh):
            s_ref[hh] = _dot_nt(q2_ref[hh], k_ref[rows, hh * HEAD_DIM:(hh + 1) * HEAD_DIM])
        for hh in range(nh):
            for c in range(2 * tq // strip):
                sr = slice(c * strip, (c + 1) * strip)
                s = s_ref[hh, sr]
                if diagonal:
                    q0 = (c * strip) % tq
                    s = jnp.where(key_idx <= qry_idx + q0, s, -jnp.inf)
                m_old = m_ref[hh, sr]
                m_new = jnp.maximum(m_old, jnp.max(s, axis=-1, keepdims=True))
                alpha = jnp.exp(m_old - m_new)
                p = jnp.exp(s - jnp.concatenate([m_new] * (tq // HEAD_DIM), axis=1))
                l_ref[hh, sr] = alpha * l_ref[hh, sr] + jnp.sum(p, axis=-1, keepdims=True)
                m_ref[hh, sr] = m_new
                p_ref[hh, sr] = p.astype(BF16)
                acc_ref[hh, sr] = alpha * acc_ref[hh, sr]
            acc_ref[hh] += jnp.dot(p_ref[hh], v_ref[rows, hh * HEAD_DIM:(hh + 1) * HEAD_DIM].astype(BF16),
                                   preferred_element_type=F32)

    def body(j, carry):
        step(j, False)
        return carry

    lax.fori_loop(0, i, body, 0)
    step(i, True)
    lam = _lambda(lq_ref[...], lam_init)
    for hh in range(nh):
        o2 = acc_ref[hh] / l_ref[hh]
        o = o2[:tq] - lam * o2[tq:]
        o_ref[:, hh * HEAD_DIM:(hh + 1) * HEAD_DIM] = (_rms_rows(o) * sub_ref[...] * (1.0 - lam_init)).astype(o_ref.dtype)


def _attn_prompt(z_tail, lambda_qk, subln, lam_init, mix, mix_col0, kv_cache, layer, depth, n_heads, col0,
                 tq=256, hpb=2):
    b, t, _ = z_tail.shape
    wl = hpb * HEAD_DIM
    ng = n_heads // hpb
    c0 = col0 // hpb
    m0 = mix_col0 // wl
    kv_shape = jax.ShapeDtypeStruct((b, depth, t, n_heads * HEAD_DIM), z_tail.dtype)
    kv_spec = pl.BlockSpec((None, None, t, wl), lambda a, h, i: (a, layer, 0, h))
    kv_in = () if kv_cache is None else tuple(kv_cache)
    aliases = {5: 0}
    aliases.update({6 + r: 1 + r for r in range(len(kv_in))})
    return pl.pallas_call(
        functools.partial(_attn_prompt_kernel, lam_init, hpb, tq, len(kv_in)),
        out_shape=[jax.ShapeDtypeStruct(mix.shape, mix.dtype), kv_shape, kv_shape],
        grid=(b, ng, t // tq),
        in_specs=[pl.BlockSpec((None, tq, wl), lambda a, h, i: (a, i, c0 + h)),
                  pl.BlockSpec((None, t, wl), lambda a, h, i: (a, 0, c0 + ng + h)),
                  pl.BlockSpec((None, t, wl), lambda a, h, i: (a, 0, c0 + 2 * ng + h)),
                  pl.BlockSpec((4, HALF_DIM), lambda a, h, i: (0, 0)),
                  pl.BlockSpec((1, HEAD_DIM), lambda a, h, i: (0, 0))]
                 + [pl.BlockSpec(memory_space=pl.ANY)] * (1 + len(kv_in)),
        out_specs=[pl.BlockSpec((None, tq, wl), lambda a, h, i: (a, i, m0 + h)), kv_spec, kv_spec],
        scratch_shapes=[pltpu.VMEM((hpb, 2 * tq, HEAD_DIM), BF16),
                        pltpu.VMEM((hpb, 2 * tq, tq), F32),
                        pltpu.VMEM((hpb, 2 * tq, tq), BF16),
                        pltpu.VMEM((hpb, 2 * tq, HEAD_DIM), F32),
                        pltpu.VMEM((hpb, 2 * tq, HEAD_DIM), F32),
                        pltpu.VMEM((hpb, 2 * tq, HEAD_DIM), F32)],
        input_output_aliases=aliases,
        compiler_params=_cparams("arbitrary", "arbitrary", "arbitrary"),
        name="attn_prompt",
    )(z_tail, z_tail, z_tail, lambda_qk, subln.reshape(1, HEAD_DIM), mix, *kv_in)


def _lane_pick(x, h):
    lane = lax.broadcasted_iota(jnp.int32, x.shape, 1)
    return jnp.sum(jnp.where(lane == h, x, 0.0), axis=-1, keepdims=True)


def _row_to_col(x):
    ii = lax.broadcasted_iota(jnp.int32, (HEAD_DIM, HEAD_DIM), 0)
    jj = lax.broadcasted_iota(jnp.int32, (HEAD_DIM, HEAD_DIM), 1)
    return jnp.sum(jnp.where(ii == jj, jnp.broadcast_to(x, (HEAD_DIM, HEAD_DIM)), 0.0), axis=-1, keepdims=True)


def _rows8(*rows):
    pad = jnp.zeros((SUBLANES - len(rows), HEAD_DIM), F32)
    return jnp.concatenate(list(rows) + [pad], axis=0)


def _delta_sample_kernel(hps, q_ref, k_ref, v_ref, za_ref, bq_ref, bk_ref, bv_ref, cq_ref, ck_ref, cv_ref,
                         zg_ref, alog_ref, dtb_ref, norm_ref, s_ref, o_ref, so_ref):
    n_heads = alog_ref.shape[1]
    zg = zg_ref[...]

    def l2norm(x):
        return x * lax.rsqrt(jnp.sum(x * x, axis=-1, keepdims=True) + 1e-6)

    for hh in range(hps):
        h = pl.program_id(1) * hps + hh
        lanes = slice(hh * HEAD_DIM, (hh + 1) * HEAD_DIM)

        def conv_silu(x_ref, buf_ref, w_ref):
            w = w_ref[:, lanes]
            buf = buf_ref[:, lanes]
            y = x_ref[:, lanes] * w[CONV_W - 1:CONV_W]
            for j in range(CONV_W - 1):
                y = y + buf[j:j + 1] * w[j:j + 1]
            return _silu(y)

        q = l2norm(conv_silu(q_ref, bq_ref, cq_ref)) * (HEAD_DIM ** -0.5)
        k = l2norm(conv_silu(k_ref, bk_ref, ck_ref))
        v = conv_silu(v_ref, bv_ref, cv_ref)
        a_a = _lane_pick(zg, h)
        b_a = _lane_pick(zg, h + n_heads)
        g = -jnp.exp(_lane_pick(alog_ref[...], h)) * _softplus(a_a + _lane_pick(dtb_ref[...], h))
        beta = _sigmoid(b_a)
        eg = jnp.exp(g)
        s = s_ref[hh]
        ks_qs = _dot(_rows8(k, q), s)
        v_new = beta * (v - eg * ks_qs[0:1])
        qk = jnp.sum(q * k, axis=-1, keepdims=True)
        o = eg * ks_qs[1:2] + qk * v_new
        so_ref[hh] = s * eg + _row_to_col(k) * v_new
        o_ref[:, lanes] = (_rms_rows(o) * norm_ref[...] * _silu(za_ref[:, lanes])).astype(o_ref.dtype)


def _delta_sample(zs_main, zs_gate, conv_buf, conv_w, a_log, dt_bias, norm_w, state, layer, n_heads, hps=8):
    nb = state.shape[0]
    wl = hps * HEAD_DIM
    ng = n_heads // hps
    col = lambda off: pl.BlockSpec((None, 1, wl), lambda a, h: (a, 0, off + h))
    buf = lambda off: pl.BlockSpec((None, None, CONV_W - 1, wl), lambda a, h: (a, layer, 0, off + h))
    cw = lambda off: pl.BlockSpec((CONV_W, wl), lambda a, h: (0, off + h))
    vec = pl.BlockSpec((1, n_heads), lambda a, h: (0, 0))
    return pl.pallas_call(
        functools.partial(_delta_sample_kernel, hps),
        out_shape=[jax.ShapeDtypeStruct((nb, 1, n_heads * HEAD_DIM), BF16),
                   jax.ShapeDtypeStruct((nb, n_heads, HEAD_DIM, HEAD_DIM), F32)],
        grid=(nb, ng),
        in_specs=[col(0), col(ng), col(2 * ng), col(3 * ng),
                  buf(0), buf(ng), buf(2 * ng), cw(0), cw(ng), cw(2 * ng),
                  pl.BlockSpec((None, 1, 2 * n_heads), lambda a, h: (a, 0, 0)), vec, vec,
                  pl.BlockSpec((1, HEAD_DIM), lambda a, h: (0, 0)),
                  pl.BlockSpec((None, None, hps, HEAD_DIM, HEAD_DIM), lambda a, h: (a, layer, h, 0, 0))],
        out_specs=[pl.BlockSpec((None, 1, wl), lambda a, h: (a, 0, h)),
                   pl.BlockSpec((None, hps, HEAD_DIM, HEAD_DIM), lambda a, h: (a, h, 0, 0))],
        compiler_params=_cparams("arbitrary", "arbitrary"),
        name="delta_sample",
    )(zs_main, zs_main, zs_main, zs_main, conv_buf, conv_buf, conv_buf, conv_w, conv_w, conv_w,
      zs_gate, a_log.reshape(1, n_heads), dt_bias.reshape(1, n_heads), norm_w.reshape(1, HEAD_DIM), state)


def _ret_sample_kernel(hps, q_ref, k_ref, v_ref, g_ref, cos_ref, sine_ref, sino_ref, s_ref, o_ref, so_ref):
    cos, sin_e, sin_o = cos_ref[...], sine_ref[...], sino_ref[...]
    for hh in range(hps):
        lanes = slice(hh * HEAD_DIM, (hh + 1) * HEAD_DIM)
        h = jnp.full((1, 1), pl.program_id(1) * hps + hh, jnp.int32).astype(F32)
        gamma = jnp.exp(_log_gamma(h))
        q = _xpos(q_ref[:, lanes], cos, sin_e, sin_o)
        k = _xpos(k_ref[:, lanes], cos, sin_e, sin_o) * (HEAD_DIM ** -0.5)
        v = v_ref[:, lanes]
        s = s_ref[hh]
        qs = _dot(_rows8(q), s)
        qk = jnp.sum(q * k, axis=-1, keepdims=True)
        o = qk * v + gamma * qs[0:1]
        so_ref[hh] = s * gamma + _row_to_col(k) * v
        o_ref[:, lanes] = (_rms_rows(o) * _silu(g_ref[:, lanes])).astype(o_ref.dtype)


def _ret_sample(zs_tail, cos, sin_e, sin_o, state, layer, n_heads, hps=8):
    nb = state.shape[0]
    wl = hps * HEAD_DIM
    ng = n_heads // hps
    col = lambda off: pl.BlockSpec((None, 1, wl), lambda a, h: (a, 0, off + h))
    tab = pl.BlockSpec((1, HEAD_DIM), lambda a, h: (0, 0))
    return pl.pallas_call(
        functools.partial(_ret_sample_kernel, hps),
        out_shape=[jax.ShapeDtypeStruct((nb, 1, n_heads * HEAD_DIM), BF16),
                   jax.ShapeDtypeStruct((nb, n_heads, HEAD_DIM, HEAD_DIM), F32)],
        grid=(nb, ng),
        in_specs=[col(0), col(ng), col(2 * ng), col(3 * ng), tab, tab, tab,
                  pl.BlockSpec((None, None, hps, HEAD_DIM, HEAD_DIM), lambda a, h: (a, layer, h, 0, 0))],
        out_specs=[pl.BlockSpec((None, 1, wl), lambda a, h: (a, 0, h)),
                   pl.BlockSpec((None, hps, HEAD_DIM, HEAD_DIM), lambda a, h: (a, h, 0, 0))],
        compiler_params=_cparams("arbitrary", "arbitrary"),
        name="ret_sample",
    )(zs_tail, zs_tail, zs_tail, zs_tail, cos, sin_e, sin_o, state)


def _attn_decode_kernel(lam_init, ppb, pt_ref, q_ref, kn_ref, vn_ref, *rest):
    kc_refs, vc_refs = rest[:ppb], rest[ppb:2 * ppb]
    lq_ref, sub_ref, o_ref, m_ref, l_ref, acc_ref, s_ref = rest[2 * ppb:]
    j = pl.program_id(1)
    n_steps = pl.num_programs(1)
    nh = q_ref.shape[0]
    scale = HALF_DIM ** -0.5
    di = lax.broadcasted_iota(jnp.int32, (HEAD_DIM, 2 * HEAD_DIM), 0)
    ci = lax.broadcasted_iota(jnp.int32, (HEAD_DIM, 2 * HEAD_DIM), 1)
    expand = ((di // HALF_DIM) == (ci // HEAD_DIM)).astype(BF16)
    q = q_ref[...] * (scale * math.log2(math.e))
    strip = 8
    n_strips = PAGE_SIZE // strip

    @pl.when(j == 0)
    def _():
        m_ref[...] = jnp.full(m_ref.shape, -jnp.inf, F32)
        l_ref[...] = jnp.zeros_like(l_ref)
        acc_ref[...] = jnp.zeros_like(acc_ref)

    low = lax.broadcasted_iota(jnp.int32, (nh, HEAD_DIM), 1) < HALF_DIM

    def swap_halves(x):
        return pltpu.roll(x.reshape(-1, HEAD_DIM), HALF_DIM, 1).reshape(x.shape)

    def scores(kt):
        n_tok = kt.shape[0]
        prod = (kt * q[None]).reshape(n_tok * nh, HEAD_DIM)
        s = jnp.dot(prod.astype(BF16), expand, preferred_element_type=F32).reshape(n_tok, nh, 2 * HEAD_DIM)
        return jnp.where(low[None], s[:, :, :HEAD_DIM], s[:, :, HEAD_DIM:])

    def weighted(p, vt):
        return jnp.concatenate([jnp.sum(p * vt, axis=0), jnp.sum(swap_halves(p) * vt, axis=0)], axis=-1)

    m, l, acc = m_ref[...], l_ref[...], acc_ref[...]
    s_ref[0] = scores(kc_refs[0][...])
    for pg in range(ppb):
        if pg + 1 < ppb:
            s_ref[pg + 1] = scores(kc_refs[pg + 1][...])
        m_new = m
        for i in range(n_strips):
            m_new = jnp.maximum(m_new, jnp.max(s_ref[pg, i * strip:(i + 1) * strip], axis=0))
        alpha = jnp.exp2(m - m_new)
        l, acc, m = alpha * l, jnp.concatenate([alpha, swap_halves(alpha)], axis=-1) * acc, m_new
        for i in range(n_strips):
            p = jnp.exp2(s_ref[pg, i * strip:(i + 1) * strip] - m[None])
            l = l + jnp.sum(p, axis=0)
            acc = acc + weighted(p, vc_refs[pg][i * strip:(i + 1) * strip])
    m_ref[...] = m
    l_ref[...] = l
    acc_ref[...] = acc

    @pl.when(j == n_steps - 1)
    def _():
        s = scores(kn_ref[...][None])[0]
        m_old = m_ref[...]
        m_new = jnp.maximum(m_old, s)
        alpha = jnp.exp2(m_old - m_new)
        p = jnp.exp2(s - m_new)
        l = alpha * l_ref[...] + p
        acc = (jnp.concatenate([alpha, swap_halves(alpha)], axis=-1) * acc_ref[...]
               + weighted(p[None], vn_ref[...][None]))
        direct, cross = acc[:, :HEAD_DIM], acc[:, HEAD_DIM:]
        l_sw = swap_halves(l)
        o1 = jnp.where(low, direct, cross) / jnp.where(low, l, l_sw)
        o2 = jnp.where(low, cross, direct) / jnp.where(low, l_sw, l)
        lam = _lambda(lq_ref[...], lam_init)
        o = o1 - lam * o2
        o_ref[...] = (_rms_rows(o) * sub_ref[...] * (1.0 - lam_init)).astype(o_ref.dtype)


def _attn_decode(q, k_new, v_new, cache_k, cache_v, page_table, lambda_qk, subln, lam_init, layer, ppb=16):
    nb, nh, _ = q.shape
    n_pages = page_table.shape[1]
    assert n_pages % ppb == 0
    tok = pl.BlockSpec((None, nh, HEAD_DIM), lambda b, j, pt: (b, 0, 0))

    def page(r):
        return pl.BlockSpec((None, None, PAGE_SIZE, nh, HEAD_DIM),
                            lambda b, j, pt: (pt[b, j * ppb + r], layer, 0, 0, 0))

    pages = [page(r) for r in range(ppb)]
    grid_spec = pltpu.PrefetchScalarGridSpec(
        num_scalar_prefetch=1,
        grid=(nb, n_pages // ppb),
        in_specs=[tok, tok, tok] + pages + pages +
                 [pl.BlockSpec((4, HALF_DIM), lambda b, j, pt: (0, 0)),
                  pl.BlockSpec((1, HEAD_DIM), lambda b, j, pt: (0, 0))],
        out_specs=tok,
        scratch_shapes=[pltpu.VMEM((nh, HEAD_DIM), F32),
                        pltpu.VMEM((nh, HEAD_DIM), F32),
                        pltpu.VMEM((nh, 2 * HEAD_DIM), F32),
                        pltpu.VMEM((ppb, PAGE_SIZE, nh, HEAD_DIM), F32)],
    )
    return pl.pallas_call(
        functools.partial(_attn_decode_kernel, lam_init, ppb),
        out_shape=jax.ShapeDtypeStruct((nb, nh, HEAD_DIM), BF16),
        grid_spec=grid_spec,
        compiler_params=_cparams("arbitrary", "arbitrary"),
        name="attn_decode",
    )(page_table, q, k_new, v_new, *([cache_k] * ppb), *([cache_v] * ppb), lambda_qk, subln.reshape(1, HEAD_DIM))


def _xpos_tables(pos):
    half = HEAD_DIM // 2
    angle = jnp.repeat(1.0 / (10000.0 ** jnp.linspace(0.0, 1.0, half, dtype=F32)), 2)
    ph = pos.astype(F32)[:, None] * angle
    sin, cos = jnp.sin(ph), jnp.cos(ph)
    even = (jnp.arange(HEAD_DIM) % 2) == 0
    return cos, jnp.where(even, -sin, 0.0), jnp.where(even, 0.0, sin)


def kernel(x_prompt, x_sample, state_conv_a, state_delta, state_ret, cache_k, cache_v, page_table, c_prompt, c_sample, w_ada, b_ada, w_in, conv_a, a_log, dt_bias, norm_a, lambda_qk, subln_c, w_out, ln1_g, ln1_b, w_up, w_down, ln2_g, ln2_b):
    bp, t, d = x_prompt.shape
    nb = x_sample.shape[0]
    depth = w_in.shape[0]
    h_a = state_delta.shape[2]
    h_b = state_ret.shape[2]
    h_c = cache_k.shape[3]
    a_w, b_w, c_w = h_a * HEAD_DIM, h_b * HEAD_DIM, h_c * HEAD_DIM
    n_main = 4 * a_w
    n_gate = 2 * h_a
    past_len = page_table.shape[1] * PAGE_SIZE
    alpha = (2 * depth) ** 0.25
    rs = SAMPLE_ROWS
    assert nb + bp <= rs and x_sample.shape[1] == 1

    c_all = jnp.concatenate([c_sample, c_prompt, jnp.zeros((rs - nb - bp, d), F32)], axis=0)
    mod = _ada(c_all, w_ada, b_ada)

    def mod_p(l, i):
        return mod[l, nb:nb + bp, None, i * d:(i + 1) * d]

    def mod_s(l, i):
        return mod[l][None, :, i * d:(i + 1) * d]

    cos_p, sine_p, sino_p = _xpos_tables(jnp.arange(t))
    cos_s, sine_s, sino_s = _xpos_tables(past_len + jnp.arange(1))

    xp = x_prompt
    xs = jnp.concatenate([x_sample.reshape(nb, d), jnp.zeros((rs - nb, d), F32)], axis=0)[None]
    hp = _modulate(xp, mod_p(0, 1), mod_p(0, 0), 256)
    hs = _modulate(xs, mod_s(0, 1), mod_s(0, 0), rs)

    w_in_t = jnp.swapaxes(w_in, 1, 2)
    n_tail = w_in.shape[2] - n_main - n_gate
    w_gate_b = w_in[:, :, n_main:n_main + n_gate].astype(BF16)
    w_down_b = w_down.astype(BF16)

    st_p, st_s = [], []
    kv_p = None
    for l in range(depth):
        lam_init = 0.8 - 0.6 * math.exp(-0.3 * l)
        hp2 = hp.reshape(bp * t, d)
        hs2 = hs.reshape(rs, d)
        zp_main, zs_main = _panel_matmul(hp2, hs2, w_in_t, l, 0, n_main, transposed=True)
        zp_gate, zs_gate = _matmul(hp2, hs2, w_gate_b, l)
        zp_tail, zs_tail = _panel_matmul(hp2, hs2, w_in_t, l, n_main + n_gate, n_tail, transposed=True)
        zp_main = zp_main.reshape(bp, t, n_main)
        zp_tail = zp_tail.reshape(bp, t, -1)

        gc, beta = _gates(zp_gate.reshape(bp, t, n_gate), a_log[l], dt_bias[l])
        mix_p, delta_p = _delta_prompt(zp_main, conv_a[l], gc, beta, norm_a[l], h_a, a_w + b_w + c_w)
        mix_p, ret_p = _ret_prompt(zp_tail, cos_p, sine_p, sino_p, mix_p, a_w, h_b)
        mix_p, *kv_p = _attn_prompt(zp_tail, lambda_qk[l], subln_c[l], lam_init, mix_p, a_w + b_w, kv_p, l, depth,
                                    h_c, 4 * h_b)
        k_off = 4 * b_w + c_w
        conv_p = zp_main[:, t - (CONV_W - 1):, :3 * a_w]
        st_p.append((conv_p, delta_p, ret_p))

        zs_m = zs_main[:nb]
        zs_t = zs_tail[:nb]
        oa_s, delta_s = _delta_sample(zs_m[:, None], zs_gate[:nb, None], state_conv_a, conv_a[l], a_log[l],
                                      dt_bias[l], norm_a[l], state_delta, l, h_a)
        ob_s, ret_s = _ret_sample(zs_t[:, None], cos_s, sine_s, sino_s, state_ret, l, h_b)
        q_s = zs_t[:, 4 * b_w:4 * b_w + c_w].reshape(nb, h_c, HEAD_DIM)
        k_s = zs_t[:, k_off:k_off + c_w].reshape(nb, h_c, HEAD_DIM)
        v_s = zs_t[:, k_off + c_w:k_off + 2 * c_w].reshape(nb, h_c, HEAD_DIM)
        oc_s = _attn_decode(q_s, k_s, v_s, cache_k, cache_v, page_table, lambda_qk[l], subln_c[l], lam_init, l)
        conv_s = jnp.concatenate([state_conv_a[:, l, 1:], zs_m[:, None, :3 * a_w]], axis=1)
        st_s.append((conv_s, delta_s, ret_s, k_s[:, None], v_s[:, None]))

        mix_p = mix_p.reshape(bp * t, -1)
        mix_s = jnp.concatenate([oa_s.reshape(nb, a_w), ob_s.reshape(nb, b_w), oc_s.reshape(nb, c_w)], axis=-1)
        mix_s = jnp.concatenate([mix_s, jnp.zeros((rs - nb, mix_s.shape[1]), BF16)], axis=0)
        yp, ys = _panel_matmul(mix_p, mix_s, w_out, l, 0, d)
        xp, hp = _postnorm(xp, yp.reshape(bp, t, d), mod_p(l, 2), ln1_g[l], ln1_b[l], mod_p(l, 4), mod_p(l, 3),
                           alpha, 256)
        xs, hs = _postnorm(xs, ys[None], mod_s(l, 2), ln1_g[l], ln1_b[l], mod_s(l, 4), mod_s(l, 3), alpha, rs)

        up, us = _panel_matmul(hp.reshape(bp * t, d), hs.reshape(rs, d), w_up, l, 0, w_up.shape[2],
                               out_dtype=BF16, relu2=True)
        fp, fs = _matmul(up, us, w_down_b, l)
        last = l == depth - 1
        nl = min(l + 1, depth - 1)
        xp, hp = _postnorm(xp, fp.reshape(bp, t, d), mod_p(l, 5), ln2_g[l], ln2_b[l], mod_p(nl, 1), mod_p(nl, 0),
                           alpha, 256, with_h=not last)
        xs, hs = _postnorm(xs, fs[None], mod_s(l, 5), ln2_g[l], ln2_b[l], mod_s(nl, 1), mod_s(nl, 0), alpha, rs,
                           with_h=not last)

    outs_p = tuple(jnp.stack([s[i] for s in st_p], axis=1) for i in range(3))
    outs_p += tuple(o.reshape(bp, depth, t, h_c, HEAD_DIM) for o in kv_p)
    outs_s = tuple(jnp.stack([s[i] for s in st_s], axis=1) for i in range(5))
    y_sample = xs[0, :nb].reshape(nb, 1, d)
    return (xp, y_sample) + outs_p + outs_s
```

```python
import functools
import math

import jax
import jax.numpy as jnp
from jax import lax
from jax.experimental import pallas as pl
from jax.experimental.pallas import tpu as pltpu

F32 = jnp.float32
BF16 = jnp.bfloat16

SUBLANES = 8
HEAD_DIM = 128
HALF_DIM = HEAD_DIM // 2
CONV_W = 4
DELTA_CHUNK = 64
DELTA_SUPER = 256
DELTA_BASE = 4
RET_CHUNK = 128
PAGE_SIZE = 128
EPS = 1e-5
SAMPLE_ROWS = 16
VMEM_LIMIT = 58 * 1024 * 1024


def _cparams(*sem):
    return pltpu.CompilerParams(dimension_semantics=sem, vmem_limit_bytes=VMEM_LIMIT)


def _sigmoid(x):
    return 1.0 / (1.0 + jnp.exp(-x))


def _silu(x):
    return x * _sigmoid(x)


def _softplus(x):
    return jnp.maximum(x, 0.0) + jnp.log1p(jnp.exp(-jnp.abs(x)))


def _dot(a, b):
    return jnp.dot(a.astype(BF16), b.astype(BF16), preferred_element_type=F32)


def _dot_nt(a, b):
    return lax.dot_general(a.astype(BF16), b.astype(BF16), (((1,), (1,)), ((), ())),
                           preferred_element_type=F32)


def _rms_rows(x):
    return x * lax.rsqrt(jnp.mean(x * x, axis=-1, keepdims=True) + EPS)


def _ada_kernel(c_ref, w0_ref, w1_ref, b_ref, o_ref):
    c = _silu(c_ref[...])
    dh = w0_ref.shape[0]
    o_ref[...] = _dot(c[:, :dh], w0_ref[...]) + _dot(c[:, dh:], w1_ref[...]) + b_ref[...]


def _ada(c_all, w_ada, b_ada, tn=1024):
    depth, d, n = w_ada.shape
    rows = c_all.shape[0]
    return pl.pallas_call(
        _ada_kernel,
        out_shape=jax.ShapeDtypeStruct((depth, rows, n), F32),
        grid=(depth, n // tn),
        in_specs=[pl.BlockSpec((rows, d), lambda l, j: (0, 0)),
                  pl.BlockSpec((None, d // 2, tn), lambda l, j: (l, 0, j)),
                  pl.BlockSpec((None, d // 2, tn), lambda l, j: (l, 1, j)),
                  pl.BlockSpec((None, 1, tn), lambda l, j: (l, 0, j))],
        out_specs=pl.BlockSpec((None, rows, tn), lambda l, j: (l, 0, j)),
        compiler_params=_cparams("arbitrary", "arbitrary"),
        name="ada_mod",
    )(c_all, w_ada, w_ada, b_ada.reshape(depth, 1, n))


def _modulate_kernel(x_ref, sc_ref, sh_ref, o_ref):
    o_ref[...] = (x_ref[...] * (1.0 + sc_ref[...]) + sh_ref[...]).astype(o_ref.dtype)


def _modulate(x, sc, sh, tm):
    g, r, d = x.shape
    rm = sc.shape[1]
    mspec = pl.BlockSpec((None, rm, d), lambda a, i: (a, 0, 0))
    return pl.pallas_call(
        _modulate_kernel,
        out_shape=jax.ShapeDtypeStruct((g, r, d), BF16),
        grid=(g, r // tm),
        in_specs=[pl.BlockSpec((None, tm, d), lambda a, i: (a, i, 0)), mspec, mspec],
        out_specs=pl.BlockSpec((None, tm, d), lambda a, i: (a, i, 0)),
        compiler_params=_cparams("arbitrary", "arbitrary"),
        name="modulate",
    )(x, sc, sh)


def _postnorm_kernel(alpha, with_h, x_ref, y_ref, gate_ref, lg_ref, lb_ref, sc_ref, sh_ref, xo_ref, *h_ref):
    v = alpha * x_ref[...] + (1.0 + gate_ref[...]) * y_ref[...]
    vc = v - jnp.mean(v, axis=-1, keepdims=True)
    var = jnp.mean(vc * vc, axis=-1, keepdims=True)
    xn = vc * lax.rsqrt(var + EPS) * lg_ref[...] + lb_ref[...]
    xo_ref[...] = xn
    if with_h:
        h_ref[0][...] = (xn * (1.0 + sc_ref[...]) + sh_ref[...]).astype(BF16)


def _postnorm(x, y, gate, ln_g, ln_b, sc, sh, alpha, tm, with_h=True):
    g, r, d = x.shape
    rm = gate.shape[1]
    row = pl.BlockSpec((None, tm, d), lambda a, i: (a, i, 0))
    mspec = pl.BlockSpec((None, rm, d), lambda a, i: (a, 0, 0))
    vec = pl.BlockSpec((1, d), lambda a, i: (0, 0))
    out_shape = [jax.ShapeDtypeStruct((g, r, d), F32)]
    out_specs = [row]
    if with_h:
        out_shape.append(jax.ShapeDtypeStruct((g, r, d), BF16))
        out_specs.append(row)
    res = pl.pallas_call(
        functools.partial(_postnorm_kernel, alpha, with_h),
        out_shape=out_shape,
        grid=(g, r // tm),
        in_specs=[row, row, mspec, vec, vec, mspec, mspec],
        out_specs=out_specs,
        compiler_params=_cparams("arbitrary", "arbitrary"),
        name="postnorm",
    )(x, y, gate, ln_g.reshape(1, d), ln_b.reshape(1, d), sc, sh)
    return (res[0], res[1]) if with_h else (res[0], None)


def _matmul_kernel(relu2, nk, a_ref, as_ref, w_ref, o_ref, os_ref, *scratch):
    i = pl.program_id(1)
    k = pl.program_id(2)

    def finish(v):
        return jnp.square(jnp.maximum(v, 0.0)) if relu2 else v

    if nk == 1:
        o_ref[...] = finish(jnp.dot(a_ref[...], w_ref[...], preferred_element_type=F32)).astype(o_ref.dtype)

        @pl.when(i == 0)
        def _():
            os_ref[...] = finish(jnp.dot(as_ref[...], w_ref[...], preferred_element_type=F32)).astype(os_ref.dtype)
    else:
        acc_ref, accs_ref = scratch

        @pl.when(k == 0)
        def _():
            acc_ref[...] = jnp.zeros_like(acc_ref)

        acc_ref[...] += jnp.dot(a_ref[...], w_ref[...], preferred_element_type=F32)

        @pl.when(k == nk - 1)
        def _():
            o_ref[...] = finish(acc_ref[...]).astype(o_ref.dtype)

        @pl.when(i == 0)
        def _():
            @pl.when(k == 0)
            def _():
                accs_ref[...] = jnp.zeros_like(accs_ref)

            accs_ref[...] += jnp.dot(as_ref[...], w_ref[...], preferred_element_type=F32)

            @pl.when(k == nk - 1)
            def _():
                os_ref[...] = finish(accs_ref[...]).astype(os_ref.dtype)


def _matmul(a, a_s, w3, layer, col0=0, n=None, out_dtype=F32, relu2=False, tm=1024, tn=1024, tk=4096):
    m, kdim = a.shape
    n = w3.shape[2] if n is None else n
    rs = a_s.shape[0]
    tn = min(tn, n)
    tk = min(tk, kdim)
    nk = kdim // tk
    assert n % tn == 0 and col0 % tn == 0 and m % tm == 0 and kdim % tk == 0
    cb0 = col0 // tn
    scratch = [] if nk == 1 else [pltpu.VMEM((tm, tn), F32), pltpu.VMEM((rs, tn), F32)]
    return pl.pallas_call(
        functools.partial(_matmul_kernel, relu2, nk),
        out_shape=[jax.ShapeDtypeStruct((m, n), out_dtype), jax.ShapeDtypeStruct((rs, n), out_dtype)],
        grid=(n // tn, m // tm, nk),
        in_specs=[pl.BlockSpec((tm, tk), lambda j, i, k: (i, k)),
                  pl.BlockSpec((rs, tk), lambda j, i, k: (0, k)),
                  pl.BlockSpec((None, tk, tn), lambda j, i, k: (layer, k, cb0 + j))],
        out_specs=[pl.BlockSpec((tm, tn), lambda j, i, k: (i, j)),
                   pl.BlockSpec((rs, tn), lambda j, i, k: (0, j))],
        scratch_shapes=scratch,
        compiler_params=_cparams("arbitrary", "arbitrary", "arbitrary"),
        name="matmul_relu2" if relu2 else "matmul",
    )(a, a_s, w3)


def _panel_matmul_kernel(relu2, n_panels, transposed, a_ref, as_ref, w_ref, o_ref, os_ref, wb_ref):
    jj = pl.program_id(0)
    i = pl.program_id(1)
    rows = w_ref.shape[0]

    def product(x, w):
        if transposed:
            v = lax.dot_general(x, w, (((1,), (1,)), ((), ())), preferred_element_type=F32)
        else:
            v = jnp.dot(x, w, preferred_element_type=F32)
        return jnp.square(jnp.maximum(v, 0.0)) if relu2 else v

    @pl.when(jj < n_panels)
    def _():
        wb_ref[jj % 2, pl.ds(pl.multiple_of(i * rows, rows), rows), :] = w_ref[...].astype(BF16)

    @pl.when(jj > 0)
    def _():
        w = wb_ref[(jj + 1) % 2]
        o_ref[...] = product(a_ref[...], w).astype(o_ref.dtype)

        @pl.when(i == 0)
        def _():
            os_ref[...] = product(as_ref[...], w).astype(os_ref.dtype)


def _panel_matmul(a, a_s, w3, layer, col0, n, transposed=False, out_dtype=F32, relu2=False, tm=1024, tn=1024):
    m, kdim = a.shape
    rs = a_s.shape[0]
    n_i = m // tm
    n_panels = n // tn
    assert n % tn == 0 and m % tm == 0

    def clamp(jj, i):
        last = jj == n_panels
        return jnp.minimum(jj, n_panels - 1), jnp.where(last, n_i - 1, i)

    if transposed:
        assert tn % (SUBLANES * n_i) == 0 and col0 % SUBLANES == 0
        rows = tn // n_i

        def w_index(jj, i):
            panel, chunk = clamp(jj, i)
            return (layer, pl.multiple_of(col0 + panel * tn + chunk * rows, SUBLANES), 0)

        w_spec = pl.BlockSpec((None, pl.Element(rows), pl.Element(kdim)), w_index)
        wb_shape = (2, tn, kdim)
    else:
        assert kdim % (SUBLANES * n_i) == 0 and col0 % tn == 0
        rows = kdim // n_i

        def w_index(jj, i):
            panel, chunk = clamp(jj, i)
            return (layer, chunk, col0 // tn + panel)

        w_spec = pl.BlockSpec((None, rows, tn), w_index)
        wb_shape = (2, kdim, tn)

    def row_block(jj, i):
        return jnp.where(jj == 0, 0, i)

    return pl.pallas_call(
        functools.partial(_panel_matmul_kernel, relu2, n_panels, transposed),
        out_shape=[jax.ShapeDtypeStruct((m, n), out_dtype), jax.ShapeDtypeStruct((rs, n), out_dtype)],
        grid=(n_panels + 1, n_i),
        in_specs=[pl.BlockSpec((tm, kdim), lambda jj, i: (row_block(jj, i), 0)),
                  pl.BlockSpec((rs, kdim), lambda jj, i: (0, 0)), w_spec],
        out_specs=[pl.BlockSpec((tm, tn), lambda jj, i: (row_block(jj, i), jnp.maximum(jj - 1, 0))),
                   pl.BlockSpec((rs, tn), lambda jj, i: (0, jnp.maximum(jj - 1, 0)))],
        scratch_shapes=[pltpu.VMEM(wb_shape, BF16)],
        compiler_params=_cparams("arbitrary", "arbitrary"),
        name="panel_matmul_relu2" if relu2 else "panel_matmul",
    )(a, a_s, w3)


def _gates_kernel(n_heads, zg_ref, alog_ref, dtb_ref, gc_ref, beta_ref):
    zg = zg_ref[...]
    rows = zg.shape[0]
    g = -jnp.exp(alog_ref[...]) * _softplus(zg[:, :n_heads] + dtb_ref[...])
    beta = _sigmoid(zg[:, n_heads:])
    pos = lax.broadcasted_iota(jnp.int32, (rows, HEAD_DIM), 0) % DELTA_CHUNK
    for h in range(n_heads):
        gh = jnp.broadcast_to(g[:, h:h + 1], (rows, HEAD_DIM))
        s = 1
        while s < DELTA_CHUNK:
            gh = gh + jnp.where(pos >= s, pltpu.roll(gh, s, 0), 0.0)
            s *= 2
        gc_ref[h] = gh
        beta_ref[h] = jnp.broadcast_to(beta[:, h:h + 1], (rows, HEAD_DIM))


def _gates(zg, a_log, dt_bias, rows=256):
    b, t, h2 = zg.shape
    nh = h2 // 2
    out = jax.ShapeDtypeStruct((b, nh, t, HEAD_DIM), F32)
    ospec = pl.BlockSpec((None, nh, rows, HEAD_DIM), lambda a, i: (a, 0, i, 0))
    vec = pl.BlockSpec((1, nh), lambda a, i: (0, 0))
    return pl.pallas_call(
        functools.partial(_gates_kernel, nh),
        out_shape=[out, out],
        grid=(b, t // rows),
        in_specs=[pl.BlockSpec((None, rows, h2), lambda a, i: (a, i, 0)), vec, vec],
        out_specs=[ospec, ospec],
        compiler_params=_cparams("arbitrary", "arbitrary"),
        name="delta_gates",
    )(zg, a_log.reshape(1, nh), dt_bias.reshape(1, nh))


def _delta_prompt_kernel(nh, q_ref, k_ref, v_ref, za_ref, cq_ref, ck_ref, cv_ref, gc_ref, beta_ref, norm_ref,
                         o_ref, s_ref, wq_ref, ka_ref, u_ref, eg_ref):
    t = q_ref.shape[0]
    sb = DELTA_SUPER
    c = DELTA_CHUNK
    n_sb = t // sb
    cps = sb // c
    half = sb // 2
    ii = lax.broadcasted_iota(jnp.int32, (sb, sb), 0)
    jj = lax.broadcasted_iota(jnp.int32, (sb, sb), 1)
    lower = ((ii // c) == (jj // c)) & (ii >= jj)
    strict = ii > jj
    row = lax.broadcasted_iota(jnp.int32, (sb, HEAD_DIM), 0)
    pair_chunk = lax.broadcasted_iota(jnp.int32, (HEAD_DIM, 2 * c), 1) // c
    blk_xor = ii ^ jj
    base = DELTA_BASE
    base_levels = int(math.log2(base))

    def l2norm(x):
        return x * lax.rsqrt(jnp.sum(x * x, axis=-1, keepdims=True) + 1e-6)

    def prepare(n, hh):
        static = isinstance(n, int)
        r0 = n * sb if static else pl.multiple_of(n * sb, sb)
        rows = pl.ds(r0, sb)
        lanes = slice(hh * HEAD_DIM, (hh + 1) * HEAD_DIM)

        def conv_silu(x_ref, w_ref):
            x = x_ref[rows, lanes]
            w = w_ref[:, lanes]
            if static and n == 0:
                prev = jnp.zeros((SUBLANES, HEAD_DIM), F32)
            else:
                p0 = r0 - SUBLANES
                prev = x_ref[pl.ds(p0 if static else pl.multiple_of(p0, SUBLANES), SUBLANES), lanes]
            y = x * w[CONV_W - 1:CONV_W]
            for s in range(1, CONV_W):
                head = jnp.broadcast_to(pltpu.roll(prev, s, 0)[None], (sb // SUBLANES, SUBLANES, HEAD_DIM))
                xs = jnp.where(row < s, head.reshape(sb, HEAD_DIM), pltpu.roll(x, s, 0))
                y = y + xs * w[CONV_W - 1 - s:CONV_W - s]
            return _silu(y)

        q = l2norm(conv_silu(q_ref, cq_ref)) * (HEAD_DIM ** -0.5)
        k = l2norm(conv_silu(k_ref, ck_ref))
        v = conv_silu(v_ref, cv_ref)
        gc = gc_ref[hh, rows]
        beta = beta_ref[hh, rows]
        kb = k * beta
        eg = jnp.exp(gc)
        gct = jnp.concatenate([gc[:half].T, gc[half:].T], axis=1)
        gc_j = jnp.broadcast_to(gct[0:1], (sb, sb))
        gc_i = jnp.concatenate([gc] * (sb // HEAD_DIM), axis=1)
        dec = jnp.exp(jnp.where(lower, gc_i - gc_j, -jnp.inf))
        kq = _dot_nt(jnp.concatenate([kb, q], axis=0), k)
        yield
        mm = jnp.where(strict, kq[:sb] * dec, 0.0)
        attn = kq[sb:] * dec
        x = jnp.where(blk_xor < base, -mm, 0.0)
        p = jnp.where(ii == jj, 1.0, x)
        xb = x.astype(BF16)
        x = jnp.dot(xb, xb, preferred_element_type=F32)
        yield
        for lvl in range(1, base_levels):
            xb = x.astype(BF16)
            if lvl < base_levels - 1:
                y = jnp.dot(jnp.concatenate([p.astype(BF16), xb], axis=0), xb, preferred_element_type=F32)
                yield
                p = p + y[:sb]
                x = y[sb:]
            else:
                y = jnp.dot(p.astype(BF16), xb, preferred_element_type=F32)
                yield
                p = p + y
        size = base
        while size < c:
            pb = p.astype(BF16)
            off = jnp.where((blk_xor >= size) & (blk_xor < 2 * size), mm, 0.0).astype(BF16)
            e = jnp.dot(off, pb, preferred_element_type=F32)
            yield
            y = jnp.dot(pb, e.astype(BF16), preferred_element_type=F32)
            yield
            p = p - y
            size *= 2
        rhs = jnp.concatenate([v * beta, kb * eg], axis=1)
        pb = p.astype(BF16)
        uw = jnp.dot(pb, rhs.astype(BF16), preferred_element_type=F32)
        yield
        mm_hi = mm.astype(BF16)
        mm_lo = (mm - mm_hi.astype(F32)).astype(BF16)
        uw_hi = uw.astype(BF16)
        uw_lo = (uw - uw_hi.astype(F32)).astype(BF16)
        resid = rhs - uw - (jnp.dot(mm_hi, uw_hi, preferred_element_type=F32)
                            + jnp.dot(mm_hi, uw_lo, preferred_element_type=F32)
                            + jnp.dot(mm_lo, uw_hi, preferred_element_type=F32))
        yield
        uw = uw + jnp.dot(pb, resid.astype(BF16), preferred_element_type=F32)
        yield
        u_ref[hh, rows] = uw[:, :HEAD_DIM]
        w = uw[:, HEAD_DIM:]
        qd = q * eg
        gl = jnp.concatenate([jnp.broadcast_to(gc[ci * c + c - 1:ci * c + c], (c, HEAD_DIM))
                              for ci in range(cps)], axis=0)
        kd = k * jnp.exp(gl - gc)
        kdt = jnp.concatenate([kd[:half].T, kd[half:].T], axis=1)
        for ci in range(cps):
            cc = n * cps + ci
            cr = slice(ci * c, (ci + 1) * c)
            pair = slice((ci // 2) * 2 * c, (ci // 2 + 1) * 2 * c)
            wq_ref[hh, cc] = jnp.concatenate([w[cr], qd[cr]], axis=0).astype(BF16)
            ka_ref[hh, cc] = jnp.concatenate([jnp.where(pair_chunk == ci % 2, kdt[:, pair], 0.0),
                                              attn[cr, pair]], axis=0).astype(BF16)
            eg_ref[hh, cc] = jnp.exp(gl[ci * c:ci * c + SUBLANES])

    def recur(n, states):
        static = isinstance(n, int)
        for ci in range(cps):
            cc = n * cps + ci
            crow = pl.ds(cc * c if static else pl.multiple_of(cc * c, c), c)
            r1 = [jnp.dot(wq_ref[hh, cc], states[hh].astype(BF16), preferred_element_type=F32) for hh in range(nh)]
            yield
            r2 = []
            for hh in range(nh):
                v_new = (u_ref[hh, crow] - r1[hh][:c]).astype(BF16)
                r2.append(jnp.dot(ka_ref[hh, cc], jnp.concatenate([v_new, v_new], axis=0),
                                  preferred_element_type=F32))
            yield
            for hh in range(nh):
                lanes = slice(hh * HEAD_DIM, (hh + 1) * HEAD_DIM)
                o = r1[hh][c:] + r2[hh][HEAD_DIM:]
                states[hh] = states[hh] * eg_ref[hh, cc][0:1] + r2[hh][:HEAD_DIM]
                o_ref[crow, lanes] = (_rms_rows(o) * norm_ref[...] * _silu(za_ref[crow, lanes])).astype(o_ref.dtype)

    def interleave(gens):
        live = list(gens)
        while live:
            live = [g for g in live if next(g, live) is not live]

    def chain(*gens):
        for g in gens:
            yield from g

    bpi = 2
    assert n_sb % bpi == 0

    def body(it, states):
        states = list(states)
        first = it * bpi
        interleave([prepare(first + r, hh) for r in range(bpi) for hh in range(nh)]
                   + [chain(*[recur(first - bpi + r, states) for r in range(bpi)])])
        return tuple(states)

    interleave([prepare(r, hh) for r in range(bpi) for hh in range(nh)])
    states = lax.fori_loop(1, n_sb // bpi, body, tuple(jnp.zeros((HEAD_DIM, HEAD_DIM), F32) for _ in range(nh)))
    states = list(states)
    interleave([chain(*[recur(n_sb - bpi + r, states) for r in range(bpi)])])
    for hh in range(nh):
        s_ref[hh] = states[hh]


def _delta_prompt(z_main, conv_w, gc, beta, norm_w, n_heads, mix_width, hpb=2):
    b, t, _ = z_main.shape
    wl = hpb * HEAD_DIM
    ng = n_heads // hpb
    n_chunks = t // DELTA_CHUNK
    col = lambda off: pl.BlockSpec((None, t, wl), lambda a, h: (a, 0, off + h))
    cw = lambda off: pl.BlockSpec((CONV_W, wl), lambda a, h: (0, off + h))
    gate = pl.BlockSpec((None, hpb, t, HEAD_DIM), lambda a, h: (a, h, 0, 0))
    return pl.pallas_call(
        functools.partial(_delta_prompt_kernel, hpb),
        out_shape=[jax.ShapeDtypeStruct((b, t, mix_width), BF16),
                   jax.ShapeDtypeStruct((b, n_heads, HEAD_DIM, HEAD_DIM), F32)],
        grid=(b, ng),
        in_specs=[col(0), col(ng), col(2 * ng), col(3 * ng),
                  cw(0), cw(ng), cw(2 * ng), gate, gate,
                  pl.BlockSpec((1, HEAD_DIM), lambda a, h: (0, 0))],
        out_specs=[pl.BlockSpec((None, t, wl), lambda a, h: (a, 0, h)),
                   pl.BlockSpec((None, hpb, HEAD_DIM, HEAD_DIM), lambda a, h: (a, h, 0, 0))],
        scratch_shapes=[pltpu.VMEM((hpb, n_chunks, 2 * DELTA_CHUNK, HEAD_DIM), BF16),
                        pltpu.VMEM((hpb, n_chunks, HEAD_DIM + DELTA_CHUNK, 2 * DELTA_CHUNK), BF16),
                        pltpu.VMEM((hpb, t, HEAD_DIM), F32),
                        pltpu.VMEM((hpb, n_chunks, SUBLANES, HEAD_DIM), F32)],
        compiler_params=_cparams("arbitrary", "arbitrary"),
        name="delta_prompt",
    )(z_main, z_main, z_main, z_main, conv_w, conv_w, conv_w, gc, beta, norm_w.reshape(1, HEAD_DIM))


def _xpos(x, cos, sin_e, sin_o):
    return x * cos + pltpu.roll(x, HEAD_DIM - 1, 1) * sin_e + pltpu.roll(x, 1, 1) * sin_o


def _log_gamma(h):
    return jnp.log1p(-jnp.exp2(-5.0 - h))


def _ret_prompt_kernel(q_ref, k_ref, v_ref, g_ref, cos_ref, sine_ref, sino_ref, mix_ref, o_ref, s_ref,
                       qs_ref, ks_ref):
    del mix_ref
    t = q_ref.shape[0]
    c = RET_CHUNK
    cos, sin_e, sin_o = cos_ref[...], sine_ref[...], sino_ref[...]
    qs_ref[...] = _xpos(q_ref[...], cos, sin_e, sin_o)
    ks_ref[...] = _xpos(k_ref[...], cos, sin_e, sin_o) * (HEAD_DIM ** -0.5)

    h = jnp.full((1, 1), pl.program_id(1), jnp.int32).astype(F32)
    lg = _log_gamma(h)
    ii = lax.broadcasted_iota(jnp.int32, (c, c), 0)
    jj = lax.broadcasted_iota(jnp.int32, (c, c), 1)
    dec = jnp.exp(jnp.where(ii >= jj, (ii - jj).astype(F32) * lg, -jnp.inf))
    pos = lax.broadcasted_iota(jnp.int32, (c, HEAD_DIM), 0).astype(F32)
    e_in = jnp.exp((pos + 1.0) * lg)
    e_out = jnp.exp((c - 1.0 - pos) * lg)
    e_all = jnp.exp(c * lg)

    unroll = 2

    def body(n, s):
        rows = [pl.ds(pl.multiple_of((n * unroll + r) * c, c), c) for r in range(unroll)]
        q = [qs_ref[rw] for rw in rows]
        k = [ks_ref[rw] for rw in rows]
        v = [v_ref[rw].astype(BF16) for rw in rows]
        qk = [_dot_nt(q[r], k[r]) for r in range(unroll)]
        upd = [_dot((k[r] * e_out).T, v[r]) for r in range(unroll)]
        for r in range(unroll):
            o = _dot(q[r] * e_in, s) + _dot(qk[r] * dec, v[r])
            s = s * e_all + upd[r]
            o_ref[rows[r]] = (_rms_rows(o) * _silu(g_ref[rows[r]])).astype(o_ref.dtype)
        return s

    s_ref[...] = lax.fori_loop(0, t // (c * unroll), body, jnp.zeros((HEAD_DIM, HEAD_DIM), F32))


def _ret_prompt(z_tail, cos, sin_e, sin_o, mix, mix_col0, n_heads):
    b, t, _ = z_tail.shape
    c0 = mix_col0 // HEAD_DIM
    col = lambda off: pl.BlockSpec((None, t, HEAD_DIM), lambda a, h: (a, 0, off + h))
    tab = pl.BlockSpec((t, HEAD_DIM), lambda a, h: (0, 0))
    return pl.pallas_call(
        _ret_prompt_kernel,
        out_shape=[jax.ShapeDtypeStruct(mix.shape, mix.dtype),
                   jax.ShapeDtypeStruct((b, n_heads, HEAD_DIM, HEAD_DIM), F32)],
        grid=(b, n_heads),
        in_specs=[col(0), col(n_heads), col(2 * n_heads), col(3 * n_heads), tab, tab, tab,
                  pl.BlockSpec(memory_space=pl.ANY)],
        out_specs=[pl.BlockSpec((None, t, HEAD_DIM), lambda a, h: (a, 0, c0 + h)),
                   pl.BlockSpec((None, None, HEAD_DIM, HEAD_DIM), lambda a, h: (a, h, 0, 0))],
        scratch_shapes=[pltpu.VMEM((t, HEAD_DIM), F32)] * 2,
        input_output_aliases={7: 0},
        compiler_params=_cparams("arbitrary", "arbitrary"),
        name="ret_prompt",
    )(z_tail, z_tail, z_tail, z_tail, cos, sin_e, sin_o, mix)


def _lambda(lq, lam_init):
    l01 = jnp.sum(lq[0:1] * lq[1:2], axis=-1, keepdims=True)
    l23 = jnp.sum(lq[2:3] * lq[3:4], axis=-1, keepdims=True)
    return jnp.exp(l01) - jnp.exp(l23) + lam_init


def _attn_prompt_kernel(lam_init, nh, tq, n_alias, q_ref, k_ref, v_ref, lq_ref, sub_ref, *rest):
    o_ref, kout_ref, vout_ref, q2_ref, s_ref, p_ref, m_ref, l_ref, acc_ref = rest[1 + n_alias:]
    i = pl.program_id(2)

    @pl.when(i == 0)
    def _():
        kout_ref[...] = k_ref[...]
        vout_ref[...] = v_ref[...]

    strip = 64
    lane = lax.broadcasted_iota(jnp.int32, (tq, HEAD_DIM), 1)
    scale = HALF_DIM ** -0.5
    for hh in range(nh):
        q = q_ref[:, hh * HEAD_DIM:(hh + 1) * HEAD_DIM]
        q2_ref[hh] = (jnp.concatenate([jnp.where(lane < HALF_DIM, q, 0.0), jnp.where(lane >= HALF_DIM, q, 0.0)],
                                      axis=0) * scale).astype(BF16)
    m_ref[...] = jnp.full(m_ref.shape, -jnp.inf, F32)
    l_ref[...] = jnp.zeros_like(l_ref)
    acc_ref[...] = jnp.zeros_like(acc_ref)
    key_idx = lax.broadcasted_iota(jnp.int32, (strip, tq), 1)
    qry_idx = lax.broadcasted_iota(jnp.int32, (strip, tq), 0)

    def step(j, diagonal):
        rows = pl.ds(pl.multiple_of(j * tq, tq), tq)
        for hh in range(nh):
            s_ref[hh] = _dot_nt(q2_ref[hh], k_ref[rows, hh * HEAD_DIM:(hh + 1) * HEAD_DIM])
        for hh in range(nh):
            for c in range(2 * tq // strip):
                sr = slice(c * strip, (c + 1) * strip)
                s = s_ref[hh, sr]
                if diagonal:
                    q0 = (c * strip) % tq
                    s = jnp.where(key_idx <= qry_idx + q0, s, -jnp.inf)
                m_old = m_ref[hh, sr]
                m_new = jnp.maximum(m_old, jnp.max(s, axis=-1, keepdims=True))
                alpha = jnp.exp(m_old - m_new)
                p = jnp.exp(s - jnp.concatenate([m_new] * (tq // HEAD_DIM), axis=1))
                l_ref[hh, sr] = alpha * l_ref[hh, sr] + jnp.sum(p, axis=-1, keepdims=True)
                m_ref[hh, sr] = m_new
                p_ref[hh, sr] = p.astype(BF16)
                acc_ref[hh, sr] = alpha * acc_ref[hh, sr]
            acc_ref[hh] += jnp.dot(p_ref[hh], v_ref[rows, hh * HEAD_DIM:(hh + 1) * HEAD_DIM].astype(BF16),
                                   preferred_element_type=F32)

    def body(j, carry):
        step(j, False)
        return carry

    lax.fori_loop(0, i, body, 0)
    step(i, True)
    lam = _lambda(lq_ref[...], lam_init)
    for hh in range(nh):
        o2 = acc_ref[hh] / l_ref[hh]
        o = o2[:tq] - lam * o2[tq:]
        o_ref[:, hh * HEAD_DIM:(hh + 1) * HEAD_DIM] = (_rms_rows(o) * sub_ref[...] * (1.0 - lam_init)).astype(o_ref.dtype)


def _attn_prompt(z_tail, lambda_qk, subln, lam_init, mix, mix_col0, kv_cache, layer, depth, n_heads, col0,
                 tq=256, hpb=2):
    b, t, _ = z_tail.shape
    wl = hpb * HEAD_DIM
    ng = n_heads // hpb
    c0 = col0 // hpb
    m0 = mix_col0 // wl
    kv_shape = jax.ShapeDtypeStruct((b, depth, t, n_heads * HEAD_DIM), z_tail.dtype)
    kv_spec = pl.BlockSpec((None, None, t, wl), lambda a, h, i: (a, layer, 0, h))
    kv_in = () if kv_cache is None else tuple(kv_cache)
    aliases = {5: 0}
    aliases.update({6 + r: 1 + r for r in range(len(kv_in))})
    return pl.pallas_call(
        functools.partial(_attn_prompt_kernel, lam_init, hpb, tq, len(kv_in)),
        out_shape=[jax.ShapeDtypeStruct(mix.shape, mix.dtype), kv_shape, kv_shape],
        grid=(b, ng, t // tq),
        in_specs=[pl.BlockSpec((None, tq, wl), lambda a, h, i: (a, i, c0 + h)),
                  pl.BlockSpec((None, t, wl), lambda a, h, i: (a, 0, c0 + ng + h)),
                  pl.BlockSpec((None, t, wl), lambda a, h, i: (a, 0, c0 + 2 * ng + h)),
                  pl.BlockSpec((4, HALF_DIM), lambda a, h, i: (0, 0)),
                  pl.BlockSpec((1, HEAD_DIM), lambda a, h, i: (0, 0))]
                 + [pl.BlockSpec(memory_space=pl.ANY)] * (1 + len(kv_in)),
        out_specs=[pl.BlockSpec((None, tq, wl), lambda a, h, i: (a, i, m0 + h)), kv_spec, kv_spec],
        scratch_shapes=[pltpu.VMEM((hpb, 2 * tq, HEAD_DIM), BF16),
                        pltpu.VMEM((hpb, 2 * tq, tq), F32),
                        pltpu.VMEM((hpb, 2 * tq, tq), BF16),
                        pltpu.VMEM((hpb, 2 * tq, HEAD_DIM), F32),
                        pltpu.VMEM((hpb, 2 * tq, HEAD_DIM), F32),
                        pltpu.VMEM((hpb, 2 * tq, HEAD_DIM), F32)],
        input_output_aliases=aliases,
        compiler_params=_cparams("arbitrary", "arbitrary", "arbitrary"),
        name="attn_prompt",
    )(z_tail, z_tail, z_tail, lambda_qk, subln.reshape(1, HEAD_DIM), mix, *kv_in)


def _lane_pick(x, h):
    lane = lax.broadcasted_iota(jnp.int32, x.shape, 1)
    return jnp.sum(jnp.where(lane == h, x, 0.0), axis=-1, keepdims=True)


def _row_to_col(x):
    ii = lax.broadcasted_iota(jnp.int32, (HEAD_DIM, HEAD_DIM), 0)
    jj = lax.broadcasted_iota(jnp.int32, (HEAD_DIM, HEAD_DIM), 1)
    return jnp.sum(jnp.where(ii == jj, jnp.broadcast_to(x, (HEAD_DIM, HEAD_DIM)), 0.0), axis=-1, keepdims=True)


def _rows8(*rows):
    pad = jnp.zeros((SUBLANES - len(rows), HEAD_DIM), F32)
    return jnp.concatenate(list(rows) + [pad], axis=0)


def _delta_sample_kernel(hps, q_ref, k_ref, v_ref, za_ref, bq_ref, bk_ref, bv_ref, cq_ref, ck_ref, cv_ref,
                         zg_ref, alog_ref, dtb_ref, norm_ref, s_ref, o_ref, so_ref):
    n_heads = alog_ref.shape[1]
    zg = zg_ref[...]

    def l2norm(x):
        return x * lax.rsqrt(jnp.sum(x * x, axis=-1, keepdims=True) + 1e-6)

    for hh in range(hps):
        h = pl.program_id(1) * hps + hh
        lanes = slice(hh * HEAD_DIM, (hh + 1) * HEAD_DIM)

        def conv_silu(x_ref, buf_ref, w_ref):
            w = w_ref[:, lanes]
            buf = buf_ref[:, lanes]
            y = x_ref[:, lanes] * w[CONV_W - 1:CONV_W]
            for j in range(CONV_W - 1):
                y = y + buf[j:j + 1] * w[j:j + 1]
            return _silu(y)

        q = l2norm(conv_silu(q_ref, bq_ref, cq_ref)) * (HEAD_DIM ** -0.5)
        k = l2norm(conv_silu(k_ref, bk_ref, ck_ref))
        v = conv_silu(v_ref, bv_ref, cv_ref)
        a_a = _lane_pick(zg, h)
        b_a = _lane_pick(zg, h + n_heads)
        g = -jnp.exp(_lane_pick(alog_ref[...], h)) * _softplus(a_a + _lane_pick(dtb_ref[...], h))
        beta = _sigmoid(b_a)
        eg = jnp.exp(g)
        s = s_ref[hh]
        ks_qs = _dot(_rows8(k, q), s)
        v_new = beta * (v - eg * ks_qs[0:1])
        qk = jnp.sum(q * k, axis=-1, keepdims=True)
        o = eg * ks_qs[1:2] + qk * v_new
        so_ref[hh] = s * eg + _row_to_col(k) * v_new
        o_ref[:, lanes] = (_rms_rows(o) * norm_ref[...] * _silu(za_ref[:, lanes])).astype(o_ref.dtype)


def _delta_sample(zs_main, zs_gate, conv_buf, conv_w, a_log, dt_bias, norm_w, state, layer, n_heads, hps=8):
    nb = state.shape[0]
    wl = hps * HEAD_DIM
    ng = n_heads // hps
    col = lambda off: pl.BlockSpec((None, 1, wl), lambda a, h: (a, 0, off + h))
    buf = lambda off: pl.BlockSpec((None, None, CONV_W - 1, wl), lambda a, h: (a, layer, 0, off + h))
    cw = lambda off: pl.BlockSpec((CONV_W, wl), lambda a, h: (0, off + h))
    vec = pl.BlockSpec((1, n_heads), lambda a, h: (0, 0))
    return pl.pallas_call(
        functools.partial(_delta_sample_kernel, hps),
        out_shape=[jax.ShapeDtypeStruct((nb, 1, n_heads * HEAD_DIM), BF16),
                   jax.ShapeDtypeStruct((nb, n_heads, HEAD_DIM, HEAD_DIM), F32)],
        grid=(nb, ng),
        in_specs=[col(0), col(ng), col(2 * ng), col(3 * ng),
                  buf(0), buf(ng), buf(2 * ng), cw(0), cw(ng), cw(2 * ng),
                  pl.BlockSpec((None, 1, 2 * n_heads), lambda a, h: (a, 0, 0)), vec, vec,
                  pl.BlockSpec((1, HEAD_DIM), lambda a, h: (0, 0)),
                  pl.BlockSpec((None, None, hps, HEAD_DIM, HEAD_DIM), lambda a, h: (a, layer, h, 0, 0))],
        out_specs=[pl.BlockSpec((None, 1, wl), lambda a, h: (a, 0, h)),
                   pl.BlockSpec((None, hps, HEAD_DIM, HEAD_DIM), lambda a, h: (a, h, 0, 0))],
        compiler_params=_cparams("arbitrary", "arbitrary"),
        name="delta_sample",
    )(zs_main, zs_main, zs_main, zs_main, conv_buf, conv_buf, conv_buf, conv_w, conv_w, conv_w,
      zs_gate, a_log.reshape(1, n_heads), dt_bias.reshape(1, n_heads), norm_w.reshape(1, HEAD_DIM), state)


def _ret_sample_kernel(hps, q_ref, k_ref, v_ref, g_ref, cos_ref, sine_ref, sino_ref, s_ref, o_ref, so_ref):
    cos, sin_e, sin_o = cos_ref[...], sine_ref[...], sino_ref[...]
    for hh in range(hps):
        lanes = slice(hh * HEAD_DIM, (hh + 1) * HEAD_DIM)
        h = jnp.full((1, 1), pl.program_id(1) * hps + hh, jnp.int32).astype(F32)
        gamma = jnp.exp(_log_gamma(h))
        q = _xpos(q_ref[:, lanes], cos, sin_e, sin_o)
        k = _xpos(k_ref[:, lanes], cos, sin_e, sin_o) * (HEAD_DIM ** -0.5)
        v = v_ref[:, lanes]
        s = s_ref[hh]
        qs = _dot(_rows8(q), s)
        qk = jnp.sum(q * k, axis=-1, keepdims=True)
        o = qk * v + gamma * qs[0:1]
        so_ref[hh] = s * gamma + _row_to_col(k) * v
        o_ref[:, lanes] = (_rms_rows(o) * _silu(g_ref[:, lanes])).astype(o_ref.dtype)


def _ret_sample(zs_tail, cos, sin_e, sin_o, state, layer, n_heads, hps=8):
    nb = state.shape[0]
    wl = hps * HEAD_DIM
    ng = n_heads // hps
    col = lambda off: pl.BlockSpec((None, 1, wl), lambda a, h: (a, 0, off + h))
    tab = pl.BlockSpec((1, HEAD_DIM), lambda a, h: (0, 0))
    return pl.pallas_call(
        functools.partial(_ret_sample_kernel, hps),
        out_shape=[jax.ShapeDtypeStruct((nb, 1, n_heads * HEAD_DIM), BF16),
                   jax.ShapeDtypeStruct((nb, n_heads, HEAD_DIM, HEAD_DIM), F32)],
        grid=(nb, ng),
        in_specs=[col(0), col(ng), col(2 * ng), col(3 * ng), tab, tab, tab,
                  pl.BlockSpec((None, None, hps, HEAD_DIM, HEAD_DIM), lambda a, h: (a, layer, h, 0, 0))],
        out_specs=[pl.BlockSpec((None, 1, wl), lambda a, h: (a, 0, h)),
                   pl.BlockSpec((None, hps, HEAD_DIM, HEAD_DIM), lambda a, h: (a, h, 0, 0))],
        compiler_params=_cparams("arbitrary", "arbitrary"),
        name="ret_sample",
    )(zs_tail, zs_tail, zs_tail, zs_tail, cos, sin_e, sin_o, state)


def _attn_decode_kernel(lam_init, ppb, pt_ref, q_ref, kn_ref, vn_ref, *rest):
    kc_refs, vc_refs = rest[:ppb], rest[ppb:2 * ppb]
    lq_ref, sub_ref, o_ref, m_ref, l_ref, acc_ref, s_ref = rest[2 * ppb:]
    j = pl.program_id(1)
    n_steps = pl.num_programs(1)
    nh = q_ref.shape[0]
    scale = HALF_DIM ** -0.5
    di = lax.broadcasted_iota(jnp.int32, (HEAD_DIM, 2 * HEAD_DIM), 0)
    ci = lax.broadcasted_iota(jnp.int32, (HEAD_DIM, 2 * HEAD_DIM), 1)
    expand = ((di // HALF_DIM) == (ci // HEAD_DIM)).astype(BF16)
    q = q_ref[...] * (scale * math.log2(math.e))
    strip = 8
    n_strips = PAGE_SIZE // strip

    @pl.when(j == 0)
    def _():
        m_ref[...] = jnp.full(m_ref.shape, -jnp.inf, F32)
        l_ref[...] = jnp.zeros_like(l_ref)
        acc_ref[...] = jnp.zeros_like(acc_ref)

    low = lax.broadcasted_iota(jnp.int32, (nh, HEAD_DIM), 1) < HALF_DIM

    def swap_halves(x):
        return pltpu.roll(x.reshape(-1, HEAD_DIM), HALF_DIM, 1).reshape(x.shape)

    def scores(kt):
        n_tok = kt.shape[0]
        prod = (kt * q[None]).reshape(n_tok * nh, HEAD_DIM)
        s = jnp.dot(prod.astype(BF16), expand, preferred_element_type=F32).reshape(n_tok, nh, 2 * HEAD_DIM)
        return jnp.where(low[None], s[:, :, :HEAD_DIM], s[:, :, HEAD_DIM:])

    def weighted(p, vt):
        return jnp.concatenate([jnp.sum(p * vt, axis=0), jnp.sum(swap_halves(p) * vt, axis=0)], axis=-1)

    m, l, acc = m_ref[...], l_ref[...], acc_ref[...]
    s_ref[0] = scores(kc_refs[0][...])
    for pg in range(ppb):
        if pg + 1 < ppb:
            s_ref[pg + 1] = scores(kc_refs[pg + 1][...])
        m_new = m
        for i in range(n_strips):
            m_new = jnp.maximum(m_new, jnp.max(s_ref[pg, i * strip:(i + 1) * strip], axis=0))
        alpha = jnp.exp2(m - m_new)
        l, acc, m = alpha * l, jnp.concatenate([alpha, swap_halves(alpha)], axis=-1) * acc, m_new
        for i in range(n_strips):
            p = jnp.exp2(s_ref[pg, i * strip:(i + 1) * strip] - m[None])
            l = l + jnp.sum(p, axis=0)
            acc = acc + weighted(p, vc_refs[pg][i * strip:(i + 1) * strip])
    m_ref[...] = m
    l_ref[...] = l
    acc_ref[...] = acc

    @pl.when(j == n_steps - 1)
    def _():
        s = scores(kn_ref[...][None])[0]
        m_old = m_ref[...]
        m_new = jnp.maximum(m_old, s)
        alpha = jnp.exp2(m_old - m_new)
        p = jnp.exp2(s - m_new)
        l = alpha * l_ref[...] + p
        acc = (jnp.concatenate([alpha, swap_halves(alpha)], axis=-1) * acc_ref[...]
               + weighted(p[None], vn_ref[...][None]))
        direct, cross = acc[:, :HEAD_DIM], acc[:, HEAD_DIM:]
        l_sw = swap_halves(l)
        o1 = jnp.where(low, direct, cross) / jnp.where(low, l, l_sw)
        o2 = jnp.where(low, cross, direct) / jnp.where(low, l_sw, l)
        lam = _lambda(lq_ref[...], lam_init)
        o = o1 - lam * o2
        o_ref[...] = (_rms_rows(o) * sub_ref[...] * (1.0 - lam_init)).astype(o_ref.dtype)


def _attn_decode(q, k_new, v_new, cache_k, cache_v, page_table, lambda_qk, subln, lam_init, layer, ppb=16):
    nb, nh, _ = q.shape
    n_pages = page_table.shape[1]
    assert n_pages % ppb == 0
    tok = pl.BlockSpec((None, nh, HEAD_DIM), lambda b, j, pt: (b, 0, 0))

    def page(r):
        return pl.BlockSpec((None, None, PAGE_SIZE, nh, HEAD_DIM),
                            lambda b, j, pt: (pt[b, j * ppb + r], layer, 0, 0, 0))

    pages = [page(r) for r in range(ppb)]
    grid_spec = pltpu.PrefetchScalarGridSpec(
        num_scalar_prefetch=1,
        grid=(nb, n_pages // ppb),
        in_specs=[tok, tok, tok] + pages + pages +
                 [pl.BlockSpec((4, HALF_DIM), lambda b, j, pt: (0, 0)),
                  pl.BlockSpec((1, HEAD_DIM), lambda b, j, pt: (0, 0))],
        out_specs=tok,
        scratch_shapes=[pltpu.VMEM((nh, HEAD_DIM), F32),
                        pltpu.VMEM((nh, HEAD_DIM), F32),
                        pltpu.VMEM((nh, 2 * HEAD_DIM), F32),
                        pltpu.VMEM((ppb, PAGE_SIZE, nh, HEAD_DIM), F32)],
    )
    return pl.pallas_call(
        functools.partial(_attn_decode_kernel, lam_init, ppb),
        out_shape=jax.ShapeDtypeStruct((nb, nh, HEAD_DIM), BF16),
        grid_spec=grid_spec,
        compiler_params=_cparams("arbitrary", "arbitrary"),
        name="attn_decode",
    )(page_table, q, k_new, v_new, *([cache_k] * ppb), *([cache_v] * ppb), lambda_qk, subln.reshape(1, HEAD_DIM))


def _xpos_tables(pos):
    half = HEAD_DIM // 2
    angle = jnp.repeat(1.0 / (10000.0 ** jnp.linspace(0.0, 1.0, half, dtype=F32)), 2)
    ph = pos.astype(F32)[:, None] * angle
    sin, cos = jnp.sin(ph), jnp.cos(ph)
    even = (jnp.arange(HEAD_DIM) % 2) == 0
    return cos, jnp.where(even, -sin, 0.0), jnp.where(even, 0.0, sin)


def kernel(x_prompt, x_sample, state_conv_a, state_delta, state_ret, cache_k, cache_v, page_table, c_prompt, c_sample, w_ada, b_ada, w_in, conv_a, a_log, dt_bias, norm_a, lambda_qk, subln_c, w_out, ln1_g, ln1_b, w_up, w_down, ln2_g, ln2_b):
    bp, t, d = x_prompt.shape
    nb = x_sample.shape[0]
    depth = w_in.shape[0]
    h_a = state_delta.shape[2]
    h_b = state_ret.shape[2]
    h_c = cache_k.shape[3]
    a_w, b_w, c_w = h_a * HEAD_DIM, h_b * HEAD_DIM, h_c * HEAD_DIM
    n_main = 4 * a_w
    n_gate = 2 * h_a
    past_len = page_table.shape[1] * PAGE_SIZE
    alpha = (2 * depth) ** 0.25
    rs = SAMPLE_ROWS
    assert nb + bp <= rs and x_sample.shape[1] == 1

    c_all = jnp.concatenate([c_sample, c_prompt, jnp.zeros((rs - nb - bp, d), F32)], axis=0)
    mod = _ada(c_all, w_ada, b_ada)

    def mod_p(l, i):
        return mod[l, nb:nb + bp, None, i * d:(i + 1) * d]

    def mod_s(l, i):
        return mod[l][None, :, i * d:(i + 1) * d]

    cos_p, sine_p, sino_p = _xpos_tables(jnp.arange(t))
    cos_s, sine_s, sino_s = _xpos_tables(past_len + jnp.arange(1))

    xp = x_prompt
    xs = jnp.concatenate([x_sample.reshape(nb, d), jnp.zeros((rs - nb, d), F32)], axis=0)[None]
    hp = _modulate(xp, mod_p(0, 1), mod_p(0, 0), 256)
    hs = _modulate(xs, mod_s(0, 1), mod_s(0, 0), rs)

    w_in_t = jnp.swapaxes(w_in, 1, 2)
    n_tail = w_in.shape[2] - n_main - n_gate
    w_gate_b = w_in[:, :, n_main:n_main + n_gate].astype(BF16)
    w_down_b = w_down.astype(BF16)

    st_p, st_s = [], []
    kv_p = None
    for l in range(depth):
        lam_init = 0.8 - 0.6 * math.exp(-0.3 * l)
        hp2 = hp.reshape(bp * t, d)
        hs2 = hs.reshape(rs, d)
        zp_main, zs_main = _panel_matmul(hp2, hs2, w_in_t, l, 0, n_main, transposed=True)
        zp_gate, zs_gate = _matmul(hp2, hs2, w_gate_b, l)
        zp_tail, zs_tail = _panel_matmul(hp2, hs2, w_in_t, l, n_main + n_gate, n_tail, transposed=True)
        zp_main = zp_main.reshape(bp, t, n_main)
        zp_tail = zp_tail.reshape(bp, t, -1)

        k_off = 4 * b_w + c_w

        zs_m = zs_main[:nb]
        zs_t = zs_tail[:nb]
        oa_s, delta_s = _delta_sample(zs_m[:, None], zs_gate[:nb, None], state_conv_a, conv_a[l], a_log[l],
                                      dt_bias[l], norm_a[l], state_delta, l, h_a)
        ob_s, ret_s = _ret_sample(zs_t[:, None], cos_s, sine_s, sino_s, state_ret, l, h_b)
        q_s = zs_t[:, 4 * b_w:4 * b_w + c_w].reshape(nb, h_c, HEAD_DIM)
        k_s = zs_t[:, k_off:k_off + c_w].reshape(nb, h_c, HEAD_DIM)
        v_s = zs_t[:, k_off + c_w:k_off + 2 * c_w].reshape(nb, h_c, HEAD_DIM)
        oc_s = _attn_decode(q_s, k_s, v_s, cache_k, cache_v, page_table, lambda_qk[l], subln_c[l], lam_init, l)
        conv_s = jnp.concatenate([state_conv_a[:, l, 1:], zs_m[:, None, :3 * a_w]], axis=1)
        st_s.append((conv_s, delta_s, ret_s, k_s[:, None], v_s[:, None]))

        gc, beta = _gates(zp_gate.reshape(bp, t, n_gate), a_log[l], dt_bias[l])
        mix_p, delta_p = _delta_prompt(zp_main, conv_a[l], gc, beta, norm_a[l], h_a, a_w + b_w + c_w)
        mix_p, ret_p = _ret_prompt(zp_tail, cos_p, sine_p, sino_p, mix_p, a_w, h_b)
        mix_p, *kv_p = _attn_prompt(zp_tail, lambda_qk[l], subln_c[l], lam_init, mix_p, a_w + b_w, kv_p, l, depth,
                                    h_c, 4 * h_b)
        conv_p = zp_main[:, t - (CONV_W - 1):, :3 * a_w]
        st_p.append((conv_p, delta_p, ret_p))

        mix_p = mix_p.reshape(bp * t, -1)
        mix_s = jnp.concatenate([oa_s.reshape(nb, a_w), ob_s.reshape(nb, b_w), oc_s.reshape(nb, c_w)], axis=-1)
        mix_s = jnp.concatenate([mix_s, jnp.zeros((rs - nb, mix_s.shape[1]), BF16)], axis=0)
        yp, ys = _panel_matmul(mix_p, mix_s, w_out, l, 0, d)
        xp, hp = _postnorm(xp, yp.reshape(bp, t, d), mod_p(l, 2), ln1_g[l], ln1_b[l], mod_p(l, 4), mod_p(l, 3),
                           alpha, 256)
        xs, hs = _postnorm(xs, ys[None], mod_s(l, 2), ln1_g[l], ln1_b[l], mod_s(l, 4), mod_s(l, 3), alpha, rs)

        up, us = _panel_matmul(hp.reshape(bp * t, d), hs.reshape(rs, d), w_up, l, 0, w_up.shape[2],
                               out_dtype=BF16, relu2=True)
        fp, fs = _matmul(up, us, w_down_b, l)
        last = l == depth - 1
        nl = min(l + 1, depth - 1)
        xp, hp = _postnorm(xp, fp.reshape(bp, t, d), mod_p(l, 5), ln2_g[l], ln2_b[l], mod_p(nl, 1), mod_p(nl, 0),
                           alpha, 256, with_h=not last)
        xs, hs = _postnorm(xs, fs[None], mod_s(l, 5), ln2_g[l], ln2_b[l], mod_s(nl, 1), mod_s(nl, 0), alpha, rs,
                           with_h=not last)

    outs_p = tuple(jnp.stack([s[i] for s in st_p], axis=1) for i in range(3))
    outs_p += tuple(o.reshape(bp, depth, t, h_c, HEAD_DIM) for o in kv_p)
    outs_s = tuple(jnp.stack([s[i] for s in st_s], axis=1) for i in range(5))
    y_sample = xs[0, :nb].reshape(nb, 1, d)
    return (xp, y_sample) + outs_p + outs_s
```
